```python
import jax, jax.numpy as jnp
from jax import lax

D_MODEL = 1024
BATCH = 8
SEQ = 2048
DEPTH = 4

GRID_W = 64
CTX_LEN = 256
N_MIXERS = 2
N_LRU_LAYERS = (DEPTH + 1) // 2
N_ATTN_LAYERS = DEPTH // 2
LRU_WIDTH = D_MODEL
LRU_BLOCKS = 8
LRU_BLOCK_W = LRU_WIDTH // LRU_BLOCKS
CONV_W = 4
LRU_C = 8.0
HEAD_DIM = 128
N_HEADS = D_MODEL // HEAD_DIM
N_KV_HEADS = 2
GROUP = N_HEADS // N_KV_HEADS
WINDOW = 128
QBLOCK = 128
QKV_WIDTH = (N_HEADS + 2 * N_KV_HEADS) * HEAD_DIM
ROPE_BASE = 10000.0
N_EXPERTS = 16
EXPERT_FF = 1024
CAPACITY_FACTOR = 2
NORM_EPS = 1e-6
NEG_INF = -1e30

kernel_name = "hybrid_rglru_swa_ecmoe_diffusion"


def rmsnorm(x, g):
    xf = x.astype(jnp.float32)
    y = xf * lax.rsqrt(jnp.mean(xf * xf, axis=-1, keepdims=True) + NORM_EPS)
    return y.astype(x.dtype) * g


def modulate(h, shift, scale):
    return h * (1 + scale) + shift


def short_conv(x, w, b):
    n = x.shape[1]
    left = (CONV_W - 1) // 2
    right = CONV_W - 1 - left
    xp = jnp.pad(x, ((0, 0), (left, right), (0, 0)))
    return sum(xp[:, k:k + n] * w[k] for k in range(CONV_W)) + b


def _lin_combine(e1, e2):
    a1, b1 = e1
    a2, b2 = e2
    return a1 * a2, a2 * b1 + b2


def linear_scan(a, b, h0, reverse):
    A, H = lax.associative_scan(_lin_combine, (a, b), axis=1, reverse=reverse)
    if h0 is None:
        return H
    return H + A * h0[:, None]


def rglru_gates(xb, gate_w, gate_b, lam):
    bsz, n = xb.shape[:2]
    xg = xb.reshape(bsz, n, LRU_BLOCKS, LRU_BLOCK_W)
    gates = (jnp.einsum('bnhi,hij->bnhj', xg, gate_w.astype(jnp.float32))
             + gate_b.astype(jnp.float32))
    r = jax.nn.sigmoid(gates[..., :LRU_BLOCK_W]).reshape(bsz, n, LRU_WIDTH)
    i = jax.nn.sigmoid(gates[..., LRU_BLOCK_W:]).reshape(bsz, n, LRU_WIDTH)
    log_a = LRU_C * r * jax.nn.log_sigmoid(lam.astype(jnp.float32))
    a = jnp.exp(log_a)
    b = jnp.sqrt(-jnp.expm1(2.0 * log_a)) * (i * xb)
    return a, b


def rglru_mixer(h_lat, h_ctx, w_in, conv_w, conv_b, gate_w, gate_b, lam, w_out, need_ctx):
    def branches(h):
        u = h @ w_in
        y = jax.nn.gelu(u[..., :LRU_WIDTH])
        xb = short_conv(u[..., LRU_WIDTH:], conv_w, conv_b).astype(jnp.float32)
        return y, xb

    y_l, x_l = branches(h_lat)
    y_c, x_c = branches(h_ctx)
    s_l = 0.0
    s_c = 0.0
    for d, rev in enumerate((False, True)):
        a_c, b_c = rglru_gates(x_c, gate_w[d], gate_b[d], lam[d])
        hc = linear_scan(a_c, b_c, None, rev)
        h0 = hc[:, 0] if rev else hc[:, -1]
        a_l, b_l = rglru_gates(x_l, gate_w[d], gate_b[d], lam[d])
        hl = linear_scan(a_l, b_l, h0, rev)
        s_l = s_l + hl
        s_c = s_c + hc
    out_l = (s_l.astype(h_lat.dtype) * y_l) @ w_out
    if not need_ctx:
        return out_l, None
    out_c = (s_c.astype(h_ctx.dtype) * y_c) @ w_out
    return out_l, out_c


def rope_axial(x, rows, cols):
    half = HEAD_DIM // 2
    freqs = ROPE_BASE ** (-jnp.arange(0, half, 2, dtype=jnp.float32) / half)

    def rot(xp, pos):
        ang = pos.astype(jnp.float32)[:, None] * freqs
        cos = jnp.cos(ang)[None, :, None].astype(x.dtype)
        sin = jnp.sin(ang)[None, :, None].astype(x.dtype)
        x1, x2 = jnp.split(xp, 2, axis=-1)
        return jnp.concatenate([x1 * cos - x2 * sin, x1 * sin + x2 * cos], axis=-1)

    return jnp.concatenate([rot(x[..., :half], rows), rot(x[..., half:], cols)], axis=-1)


def split_qkv(h, w_qkv):
    bsz, n = h.shape[:2]
    u = h @ w_qkv
    nq = N_HEADS * HEAD_DIM
    nk = N_KV_HEADS * HEAD_DIM
    q = u[..., :nq].reshape(bsz, n, N_HEADS, HEAD_DIM)
    k = u[..., nq:nq + nk].reshape(bsz, n, N_KV_HEADS, HEAD_DIM)
    v = u[..., nq + nk:].reshape(bsz, n, N_KV_HEADS, HEAD_DIM)
    return q, k, v


def attn_mixer(h_lat, h_ctx, w_qkv, sink, w_o, need_ctx):
    bsz, seq, _ = h_lat.shape
    lc = h_ctx.shape[1]
    rows_n = seq // GRID_W
    rows = jnp.repeat(jnp.arange(rows_n), GRID_W)
    cols = jnp.tile(jnp.arange(GRID_W), rows_n)
    scale = HEAD_DIM ** -0.5

    q, k, v = split_qkv(h_lat, w_qkv)
    q = rope_axial(q, rows, cols)
    k = rope_axial(k, rows, cols)
    qc, kc, vc = split_qkv(h_ctx, w_qkv)

    nblk = seq // QBLOCK
    qb = (q * scale).reshape(bsz, nblk, QBLOCK, N_KV_HEADS, GROUP, HEAD_DIM)

    def band(t):
        tp = jnp.pad(t, ((0, 0), (QBLOCK, QBLOCK), (0, 0), (0, 0)))
        tp = tp.reshape(bsz, nblk + 2, QBLOCK, N_KV_HEADS, HEAD_DIM)
        return jnp.concatenate([tp[:, j:j + nblk] for j in range(3)], axis=2)

    kb, vb = band(k), band(v)
    s_band = jnp.einsum('bnqkgd,bnskd->bnkgqs', qb, kb).astype(jnp.float32)
    qi = jnp.arange(QBLOCK)[:, None]
    sj = jnp.arange(3 * QBLOCK)[None, :]
    in_win = jnp.abs(sj - QBLOCK - qi) <= WINDOW
    kpos = jnp.arange(nblk)[:, None] * QBLOCK - QBLOCK + jnp.arange(3 * QBLOCK)[None, :]
    valid = (kpos >= 0) & (kpos < seq)
    mask = in_win[None] & valid[:, None, :]
    s_band = jnp.where(mask[None, :, None, None], s_band, NEG_INF)
    s_ctx = jnp.einsum('bnqkgd,bckd->bnkgqc', qb, kc).astype(jnp.float32)
    sink_h = sink.astype(jnp.float32).reshape(N_KV_HEADS, GROUP)
    sink_l = jnp.broadcast_to(sink_h[None, None, :, :, None, None], s_band.shape[:-1] + (1,))
    p = jax.nn.softmax(jnp.concatenate([s_band, s_ctx, sink_l], axis=-1), axis=-1)
    p_band = p[..., :3 * QBLOCK].astype(v.dtype)
    p_ctx = p[..., 3 * QBLOCK:3 * QBLOCK + lc].astype(v.dtype)
    o = (jnp.einsum('bnkgqs,bnskd->bnqkgd', p_band, vb)
         + jnp.einsum('bnkgqc,bckd->bnqkgd', p_ctx, vc))
    out_l = o.reshape(bsz, seq, N_HEADS * HEAD_DIM) @ w_o
    if not need_ctx:
        return out_l, None
    qcg = (qc * scale).reshape(bsz, lc, N_KV_HEADS, GROUP, HEAD_DIM)
    sc = jnp.einsum('bqkgd,bckd->bkgqc', qcg, kc).astype(jnp.float32)
    sink_c = jnp.broadcast_to(sink_h[None, :, :, None, None], sc.shape[:-1] + (1,))
    pc = jax.nn.softmax(jnp.concatenate([sc, sink_c], axis=-1), axis=-1)[..., :lc].astype(vc.dtype)
    oc = jnp.einsum('bkgqc,bckd->bqkgd', pc, vc).reshape(bsz, lc, N_HEADS * HEAD_DIM) @ w_o
    return out_l, oc


def ec_moe(h, router, w_gate, w_up, w_down):
    bsz, n, d = h.shape
    cap = CAPACITY_FACTOR * n // N_EXPERTS
    aff = jax.nn.softmax((h @ router).astype(jnp.float32), axis=-1)
    g, idx = lax.top_k(jnp.swapaxes(aff, 1, 2), cap)
    xg = jax.vmap(lambda hb, ib: hb[ib])(h, idx)
    a = jnp.einsum('becd,edf->becf', xg, w_gate)
    u = jnp.einsum('becd,edf->becf', xg, w_up)
    y = jnp.einsum('becf,efd->becd', jax.nn.silu(a) * u, w_down) * g[..., None].astype(h.dtype)
    return jax.vmap(lambda yb, ib: jnp.zeros((n, d), yb.dtype).at[ib.reshape(-1)].add(yb.reshape(-1, d)))(y, idx)


def setup_inputs(seed: int = 0) -> dict:
    key = jax.random.key(seed)
    ks = jax.random.split(key, 24)
    f32 = jnp.float32
    nrm = lambda k, shape, s: jax.random.normal(k, shape, f32) * s
    u = jax.random.uniform(ks[12], (N_LRU_LAYERS, 2, LRU_WIDTH), f32, minval=0.9, maxval=0.999)
    p = u ** (1.0 / LRU_C)
    lru_lambda = jnp.log(p) - jnp.log1p(-p)
    return {
        "x": nrm(ks[0], (BATCH, SEQ, D_MODEL), 1.0),
        "c": nrm(ks[1], (BATCH, D_MODEL), 1.0),
        "ctx": nrm(ks[2], (BATCH, CTX_LEN, D_MODEL), 1.0),
        "c_ctx": nrm(ks[3], (D_MODEL,), 1.0),
        "ada_w": nrm(ks[4], (DEPTH, D_MODEL, 6 * D_MODEL), 0.5 * D_MODEL ** -0.5),
        "ada_b": nrm(ks[5], (DEPTH, 6 * D_MODEL), 0.02),
        "norm1_g": 1.0 + nrm(ks[6], (DEPTH, D_MODEL), 0.02),
        "norm2_g": 1.0 + nrm(ks[7], (DEPTH, D_MODEL), 0.02),
        "lru_w_in": nrm(ks[8], (N_LRU_LAYERS, D_MODEL, 2 * LRU_WIDTH), D_MODEL ** -0.5),
        "lru_conv_w": nrm(ks[9], (N_LRU_LAYERS, CONV_W, LRU_WIDTH), CONV_W ** -0.5),
        "lru_conv_b": nrm(ks[10], (N_LRU_LAYERS, LRU_WIDTH), 0.02),
        "lru_gate_w": nrm(ks[11], (N_LRU_LAYERS, 2, LRU_BLOCKS, LRU_BLOCK_W, 2 * LRU_BLOCK_W), LRU_BLOCK_W ** -0.5),
        "lru_gate_b": nrm(ks[13], (N_LRU_LAYERS, 2, LRU_BLOCKS, 2 * LRU_BLOCK_W), 0.02),
        "lru_lambda": lru_lambda,
        "lru_w_out": nrm(ks[14], (N_LRU_LAYERS, LRU_WIDTH, D_MODEL), LRU_WIDTH ** -0.5),
        "attn_w_qkv": nrm(ks[15], (N_ATTN_LAYERS, D_MODEL, QKV_WIDTH), D_MODEL ** -0.5),
        "attn_sink": nrm(ks[16], (N_ATTN_LAYERS, N_HEADS), 0.5),
        "attn_w_o": nrm(ks[17], (N_ATTN_LAYERS, N_HEADS * HEAD_DIM, D_MODEL), (N_HEADS * HEAD_DIM) ** -0.5),
        "moe_router": nrm(ks[18], (DEPTH, D_MODEL, N_EXPERTS), D_MODEL ** -0.5),
        "moe_w_gate": nrm(ks[19], (DEPTH, N_EXPERTS, D_MODEL, EXPERT_FF), D_MODEL ** -0.5),
        "moe_w_up": nrm(ks[20], (DEPTH, N_EXPERTS, D_MODEL, EXPERT_FF), D_MODEL ** -0.5),
        "moe_w_down": nrm(ks[21], (DEPTH, N_EXPERTS, EXPERT_FF, D_MODEL), EXPERT_FF ** -0.5),
        "final_g": 1.0 + nrm(ks[22], (D_MODEL,), 0.02),
    }


def reference(x, c, ctx, c_ctx, ada_w, ada_b, norm1_g, norm2_g,
              lru_w_in, lru_conv_w, lru_conv_b, lru_gate_w, lru_gate_b, lru_lambda, lru_w_out,
              attn_w_qkv, attn_sink, attn_w_o,
              moe_router, moe_w_gate, moe_w_up, moe_w_down, final_g):
    sc = jax.nn.silu(c)[:, None, :]
    sc_ctx = jax.nn.silu(c_ctx)
    for l in range(DEPTH):
        need_ctx = l < DEPTH - 1
        sh1, sc1, g1, sh2, sc2, g2 = jnp.split(sc @ ada_w[l] + ada_b[l], 6, axis=-1)
        ch1, cs1, cg1, ch2, cs2, cg2 = jnp.split(sc_ctx @ ada_w[l] + ada_b[l], 6, axis=-1)

        h_l = modulate(rmsnorm(x, norm1_g[l]), sh1, sc1)
        h_c = modulate(rmsnorm(ctx, norm1_g[l]), ch1, cs1)
        j = l // N_MIXERS
        if l % N_MIXERS == 0:
            y_l, y_c = rglru_mixer(h_l, h_c, lru_w_in[j], lru_conv_w[j], lru_conv_b[j],
                                   lru_gate_w[j], lru_gate_b[j], lru_lambda[j], lru_w_out[j], need_ctx)
        else:
            y_l, y_c = attn_mixer(h_l, h_c, attn_w_qkv[j], attn_sink[j], attn_w_o[j], need_ctx)
        x = x + g1 * y_l
        if need_ctx:
            ctx = ctx + cg1 * y_c

        h_l = modulate(rmsnorm(x, norm2_g[l]), sh2, sc2)
        x = x + g2 * ec_moe(h_l, moe_router[l], moe_w_gate[l], moe_w_up[l], moe_w_down[l])
        if need_ctx:
            h_c = modulate(rmsnorm(ctx, norm2_g[l]), ch2, cs2)
            ctx = ctx + cg2 * ec_moe(h_c, moe_router[l], moe_w_gate[l], moe_w_up[l], moe_w_down[l])
    return rmsnorm(x, final_g)
```

```python
import functools

import jax
import jax.numpy as jnp
from jax import lax
from jax.experimental import pallas as pl
from jax.experimental.pallas import tpu as pltpu

F32 = jnp.float32
BF16 = jnp.bfloat16

LANES = 128
SUBLANES = 8
VMEM_LIMIT_BYTES = 58 * 1024 * 1024

NORM_EPS = 1e-6
NEG_INF = -1e30
LRU_C = 8.0
LRU_BLOCK_W = 128
CONV_W = 4
HEAD_DIM = 128
N_KV_HEADS = 2
QBLOCK = 128
WINDOW = 128
GRID_W = 64
ROPE_BASE = 10000.0
CAPACITY_FACTOR = 2

ROW_CHUNK = 256
MOD_TILE = 1536


def _cparams(n_axes):
    return pltpu.CompilerParams(
        dimension_semantics=("arbitrary",) * n_axes, vmem_limit_bytes=VMEM_LIMIT_BYTES)


def _dot(a, b):
    return jnp.dot(a, b, preferred_element_type=F32)


def _dot_nt(a, b):
    return lax.dot_general(a, b, (((1,), (1,)), ((), ())), preferred_element_type=F32)


def _norm_mod(x, g, shift, scale):
    ms = jnp.mean(x * x, axis=-1, keepdims=True)
    y = x * lax.rsqrt(ms + NORM_EPS) * g
    return y * (1.0 + scale) + shift


def _ada_kernel(c_ref, w_ref, b_ref, o_ref):
    s = jax.nn.silu(c_ref[...]).astype(BF16)
    o_ref[0] = _dot(s, w_ref[0].astype(BF16)) + b_ref[0]


def _ada_call(cond, ada_w, ada_b):
    n_layers, d, n_out = ada_w.shape
    rows = cond.shape[0]
    return pl.pallas_call(
        _ada_kernel,
        grid=(n_layers, n_out // MOD_TILE),
        in_specs=[
            pl.BlockSpec((rows, d), lambda l, n: (0, 0)),
            pl.BlockSpec((1, d, MOD_TILE), lambda l, n: (l, 0, n)),
            pl.BlockSpec((1, 1, MOD_TILE), lambda l, n: (l, 0, n)),
        ],
        out_specs=pl.BlockSpec((1, rows, MOD_TILE), lambda l, n: (l, 0, n)),
        out_shape=jax.ShapeDtypeStruct((n_layers, rows, n_out), F32),
        compiler_params=_cparams(2),
        name="ada",
    )(cond, ada_w, ada_b.reshape(n_layers, 1, n_out))


def _resid_kernel(a_ref, w_ref, x_ref, g_ref, o_ref):
    o_ref[0] = x_ref[0] + g_ref[0] * _dot(a_ref[0], w_ref[...])


def _resid_call(act, w, x, gate):
    bsz, t, d = x.shape
    k = act.shape[-1]
    tm = min(t, 512)
    return pl.pallas_call(
        _resid_kernel,
        grid=(bsz, t // tm),
        in_specs=[
            pl.BlockSpec((1, tm, k), lambda b, i: (b, i, 0)),
            pl.BlockSpec((k, d), lambda b, i: (0, 0)),
            pl.BlockSpec((1, tm, d), lambda b, i: (b, i, 0)),
            pl.BlockSpec((1, 1, d), lambda b, i: (b, 0, 0)),
        ],
        out_specs=pl.BlockSpec((1, tm, d), lambda b, i: (b, i, 0)),
        out_shape=jax.ShapeDtypeStruct(x.shape, F32),
        compiler_params=_cparams(2),
        name="resid",
    )(act, w, x, gate)


LRU_CB = 2 * LRU_BLOCK_W


def _log_sigmoid(x):
    return jnp.minimum(x, 0.0) - jnp.log1p(jnp.exp(-jnp.abs(x)))


def _conv_time(u, cw, cb):
    n = u.shape[0]
    row = lax.broadcasted_iota(jnp.int32, u.shape, 0)
    left = (CONV_W - 1) // 2
    acc = cb + u * cw[left:left + 1]
    for k in range(CONV_W):
        off = k - left
        if off == 0:
            continue
        shifted = pltpu.roll(u, (-off) % n, 0)
        valid = (row + off >= 0) & (row + off < n)
        acc = acc + jnp.where(valid, shifted, 0.0) * cw[k:k + 1]
    return acc


def _tile_scan(a, b, carry, reverse):
    row = lax.broadcasted_iota(jnp.int32, a.shape, 0)
    for k in (1, 2, 4):
        shift = (SUBLANES - k) if reverse else k
        a_sh = pltpu.roll(a, shift, 0)
        b_sh = pltpu.roll(b, shift, 0)
        m = (row < SUBLANES - k) if reverse else (row >= k)
        b = jnp.where(m, a * b_sh + b, b)
        a = jnp.where(m, a * a_sh, a)
    h = a * carry + b
    new_carry = h[0:1] if reverse else h[SUBLANES - 1:SUBLANES]
    return h, new_carry


def _lru_kernel(xl_ref, xc_ref, shl_ref, scl_ref, shc_ref, scc_ref, ng_ref, wy_ref, wx_ref,
                cw_ref, cb_ref, gw_ref, gb_ref, lam_ref, ol_ref, oc_ref,
                hl_s, hc_s, a_s, b_s, y_s):
    j = pl.program_id(1)
    t_lat = xl_ref.shape[1]
    t_ctx = xc_ref.shape[1]

    @pl.when(j == 0)
    def _():
        g = ng_ref[0]
        for x_ref, h_s, sh_ref, sc_ref in ((xl_ref, hl_s, shl_ref, scl_ref),
                                           (xc_ref, hc_s, shc_ref, scc_ref)):
            n = x_ref.shape[1]
            step = min(n, ROW_CHUNK)
            for r0 in range(0, n, step):
                h_s[r0:r0 + step] = _norm_mod(
                    x_ref[0, r0:r0 + step], g, sh_ref[0], sc_ref[0]).astype(BF16)

    log_lam = _log_sigmoid(lam_ref[...])

    def branches(h_s, n):
        step = min(n, ROW_CHUNK)
        for r0 in range(0, n, step):
            y_s[r0:r0 + step] = jax.nn.gelu(_dot(h_s[r0:r0 + step], wy_ref[...]))
        xb = _conv_time(_dot(h_s[0:n], wx_ref[...]), cw_ref[...], cb_ref[...])
        for kb in range(2):
            xk = xb[:, kb * LRU_BLOCK_W:(kb + 1) * LRU_BLOCK_W]
            xk16 = xk.astype(BF16)
            for d in range(2):
                gates = _dot(xk16, gw_ref[d, kb]) + gb_ref[d, kb]
                r = jax.nn.sigmoid(gates[:, :LRU_BLOCK_W])
                i = jax.nn.sigmoid(gates[:, LRU_BLOCK_W:])
                log_a = LRU_C * r * log_lam[d:d + 1, kb * LRU_BLOCK_W:(kb + 1) * LRU_BLOCK_W]
                a = jnp.exp(log_a)
                a_s[d, kb, 0:n] = a
                b_s[d, kb, 0:n] = jnp.sqrt(1.0 - a * a) * (i * xk)

    def scan(n, carries):
        nt = n // SUBLANES

        def body(i, cs):
            out = []
            for d in range(2):
                ti = (nt - 1 - i) if d == 1 else i
                rows = pl.ds(pl.multiple_of(ti * SUBLANES, SUBLANES), SUBLANES)
                for kb in range(2):
                    h, c = _tile_scan(a_s[d, kb, rows], b_s[d, kb, rows], cs[2 * d + kb], d == 1)
                    b_s[d, kb, rows] = h
                    out.append(c)
            return tuple(out)

        return lax.fori_loop(0, nt, body, carries)

    def emit(o_ref, n):
        for kb in range(2):
            s = b_s[0, kb, 0:n] + b_s[1, kb, 0:n]
            y = y_s[0:n, kb * LRU_BLOCK_W:(kb + 1) * LRU_BLOCK_W]
            o_ref[0, :, kb * LRU_BLOCK_W:(kb + 1) * LRU_BLOCK_W] = (s * y).astype(BF16)

    zero = jnp.zeros((1, LRU_BLOCK_W), F32)
    branches(hc_s, t_ctx)
    carries = scan(t_ctx, (zero,) * 4)
    emit(oc_ref, t_ctx)
    branches(hl_s, t_lat)
    scan(t_lat, carries)
    emit(ol_ref, t_lat)


def _lru_call(x, ctx, sh_l, sc_l, sh_c, sc_c, norm_g, l, w_in, conv_w, conv_b, gate_w, gate_b, lam):
    bsz, t, d = x.shape
    tc = ctx.shape[1]
    width = w_in.shape[1] // 2
    nblk = width // LRU_CB
    gbk = LRU_CB // LRU_BLOCK_W
    vec = lambda b, c: (b, 0, 0)
    return pl.pallas_call(
        _lru_kernel,
        grid=(bsz, nblk),
        in_specs=[
            pl.BlockSpec((1, t, d), vec),
            pl.BlockSpec((1, tc, d), vec),
            pl.BlockSpec((1, 1, d), vec),
            pl.BlockSpec((1, 1, d), vec),
            pl.BlockSpec((1, 1, d), vec),
            pl.BlockSpec((1, 1, d), vec),
            pl.BlockSpec((1, 1, d), lambda b, c: (l, 0, 0)),
            pl.BlockSpec((d, LRU_CB), lambda b, c: (0, c)),
            pl.BlockSpec((d, LRU_CB), lambda b, c: (0, nblk + c)),
            pl.BlockSpec((CONV_W, LRU_CB), lambda b, c: (0, c)),
            pl.BlockSpec((1, LRU_CB), lambda b, c: (0, c)),
            pl.BlockSpec((2, gbk, LRU_BLOCK_W, 2 * LRU_BLOCK_W), lambda b, c: (0, c, 0, 0)),
            pl.BlockSpec((2, gbk, 1, 2 * LRU_BLOCK_W), lambda b, c: (0, c, 0, 0)),
            pl.BlockSpec((2, LRU_CB), lambda b, c: (0, c)),
        ],
        out_specs=[
            pl.BlockSpec((1, t, LRU_CB), lambda b, c: (b, 0, c)),
            pl.BlockSpec((1, tc, LRU_CB), lambda b, c: (b, 0, c)),
        ],
        out_shape=[
            jax.ShapeDtypeStruct((bsz, t, width), BF16),
            jax.ShapeDtypeStruct((bsz, tc, width), BF16),
        ],
        scratch_shapes=[
            pltpu.VMEM((t, d), BF16),
            pltpu.VMEM((tc, d), BF16),
            pltpu.VMEM((2, gbk, t, LRU_BLOCK_W), F32),
            pltpu.VMEM((2, gbk, t, LRU_BLOCK_W), F32),
            pltpu.VMEM((t, LRU_CB), F32),
        ],
        compiler_params=_cparams(2),
        name="lru",
    )(x, ctx, sh_l, sc_l, sh_c, sc_c, norm_g, w_in, w_in, conv_w, conv_b.reshape(1, width),
      gate_w, gate_b.reshape(2, -1, 1, 2 * LRU_BLOCK_W), lam)


def _rope(x, cos, sin_signed):
    lane = lax.broadcasted_iota(jnp.int32, x.shape, 1)
    quarter = HEAD_DIM // 4
    partner = jnp.where((lane & quarter) == 0,
                        pltpu.roll(x, HEAD_DIM - quarter, 1), pltpu.roll(x, quarter, 1))
    return x * cos + partner * sin_signed


def _qkv_kernel(x_ref, sh_ref, sc_ref, ng_ref, w_ref, cos_ref, sin_ref, q_ref, k_ref, v_ref,
                *, n_heads, rotary):
    h = _norm_mod(x_ref[0], ng_ref[0], sh_ref[0], sc_ref[0]).astype(BF16)
    u = _dot(h, w_ref[...])
    scale = HEAD_DIM ** -0.5
    nq = n_heads * HEAD_DIM
    nk = N_KV_HEADS * HEAD_DIM
    for hd in range(n_heads + N_KV_HEADS):
        c = u[:, hd * HEAD_DIM:(hd + 1) * HEAD_DIM]
        if rotary:
            c = _rope(c, cos_ref[...], sin_ref[...])
        if hd < n_heads:
            q_ref[0, :, hd * HEAD_DIM:(hd + 1) * HEAD_DIM] = (c * scale).astype(BF16)
        else:
            kk = hd - n_heads
            k_ref[0, :, kk * HEAD_DIM:(kk + 1) * HEAD_DIM] = c.astype(BF16)
    v_ref[0] = u[:, nq + nk:].astype(BF16)


def _qkv_call(x, sh, sc, norm_g, l, w_qkv, cos, sin_signed, rotary):
    bsz, t, d = x.shape
    nk = N_KV_HEADS * HEAD_DIM
    nq = w_qkv.shape[1] - 2 * nk
    tm = min(t, 512)
    vec = lambda b, i: (b, 0, 0)
    return pl.pallas_call(
        functools.partial(_qkv_kernel, n_heads=nq // HEAD_DIM, rotary=rotary),
        grid=(bsz, t // tm),
        in_specs=[
            pl.BlockSpec((1, tm, d), lambda b, i: (b, i, 0)),
            pl.BlockSpec((1, 1, d), vec),
            pl.BlockSpec((1, 1, d), vec),
            pl.BlockSpec((1, 1, d), lambda b, i: (l, 0, 0)),
            pl.BlockSpec(w_qkv.shape, lambda b, i: (0, 0)),
            pl.BlockSpec((tm, HEAD_DIM), lambda b, i: (i, 0)),
            pl.BlockSpec((tm, HEAD_DIM), lambda b, i: (i, 0)),
        ],
        out_specs=[
            pl.BlockSpec((1, tm, nq), lambda b, i: (b, i, 0)),
            pl.BlockSpec((1, tm, nk), lambda b, i: (b, i, 0)),
            pl.BlockSpec((1, tm, nk), lambda b, i: (b, i, 0)),
        ],
        out_shape=[
            jax.ShapeDtypeStruct((bsz, t, nq), BF16),
            jax.ShapeDtypeStruct((bsz, t, nk), BF16),
            jax.ShapeDtypeStruct((bsz, t, nk), BF16),
        ],
        compiler_params=_cparams(2),
        name="qkv",
    )(x, sh, sc, norm_g, w_qkv, cos, sin_signed)


def _stack_heads(x, group):
    return jnp.concatenate([x[:, g * HEAD_DIM:(g + 1) * HEAD_DIM] for g in range(group)], axis=0)


def _attn_kernel(sink_ref, q_ref, k_ref, v_ref, kc_ref, vc_ref, *rest, group, need_ctx):
    if need_ctx:
        qc_ref, o_ref, oc_ref, kp_s, vp_s = rest
    else:
        o_ref, kp_s, vp_s = rest
    kh = pl.program_id(1)
    t = k_ref.shape[1]
    nblk = t // QBLOCK

    def sink_col(rows_per_head):
        row = lax.broadcasted_iota(jnp.int32, (group * rows_per_head, 1), 0)
        col = jnp.zeros((group * rows_per_head, 1), F32)
        for g in range(group):
            col = jnp.where(row >= g * rows_per_head, sink_ref[kh * group + g], col)
        return col

    zpad = jnp.zeros((QBLOCK, HEAD_DIM), BF16)
    for ref, src in ((kp_s, k_ref), (vp_s, v_ref)):
        ref[0:QBLOCK] = zpad
        ref[QBLOCK:QBLOCK + t] = src[0]
        ref[QBLOCK + t:] = zpad
    kc = kc_ref[0]
    vc = vc_ref[0]
    sink_q = sink_col(QBLOCK)

    band = 3 * QBLOCK
    row = lax.broadcasted_iota(jnp.int32, (group * QBLOCK, band), 0)
    sj = lax.broadcasted_iota(jnp.int32, (group * QBLOCK, band), 1)
    qi = row & (QBLOCK - 1)
    in_win = jnp.abs(sj - QBLOCK - qi) <= WINDOW

    def body(n, carry):
        r0 = pl.multiple_of(n * QBLOCK, QBLOCK)
        qs = _stack_heads(q_ref[0, pl.ds(r0, QBLOCK), :], group)
        kb = kp_s[pl.ds(r0, band), :]
        vb = vp_s[pl.ds(r0, band), :]
        s = _dot_nt(qs, kb)
        sc = _dot_nt(qs, kc)
        kpos = sj + (n - 1) * QBLOCK
        ok = in_win & (kpos >= 0) & (kpos < t)
        s = jnp.where(ok, s, NEG_INF)
        m = jnp.maximum(jnp.maximum(jnp.max(s, axis=1, keepdims=True),
                                    jnp.max(sc, axis=1, keepdims=True)), sink_q)
        p = jnp.exp(s - m)
        pc = jnp.exp(sc - m)
        den = (jnp.sum(p, axis=1, keepdims=True) + jnp.sum(pc, axis=1, keepdims=True)
               + jnp.exp(sink_q - m))
        o = (_dot(p.astype(BF16), vb) + _dot(pc.astype(BF16), vc)) / den
        for g in range(group):
            o_ref[0, pl.ds(r0, QBLOCK), g * HEAD_DIM:(g + 1) * HEAD_DIM] = (
                o[g * QBLOCK:(g + 1) * QBLOCK].astype(BF16))
        return carry

    lax.fori_loop(0, nblk, body, 0)

    if need_ctx:
        tc = qc_ref.shape[1]
        qs = _stack_heads(qc_ref[0], group)
        sc = _dot_nt(qs, kc)
        sink_c = sink_col(tc)
        m = jnp.maximum(jnp.max(sc, axis=1, keepdims=True), sink_c)
        pc = jnp.exp(sc - m)
        den = jnp.sum(pc, axis=1, keepdims=True) + jnp.exp(sink_c - m)
        o = _dot(pc.astype(BF16), vc) / den
        for g in range(group):
            oc_ref[0, :, g * HEAD_DIM:(g + 1) * HEAD_DIM] = o[g * tc:(g + 1) * tc].astype(BF16)


def _attn_call(sink, q, k, v, qc, kc, vc, need_ctx):
    bsz, t, nq = q.shape
    tc = kc.shape[1]
    group = nq // HEAD_DIM // N_KV_HEADS
    gw = group * HEAD_DIM
    in_specs = [
        pl.BlockSpec(memory_space=pltpu.SMEM),
        pl.BlockSpec((1, t, gw), lambda b, h: (b, 0, h)),
        pl.BlockSpec((1, t, HEAD_DIM), lambda b, h: (b, 0, h)),
        pl.BlockSpec((1, t, HEAD_DIM), lambda b, h: (b, 0, h)),
        pl.BlockSpec((1, tc, HEAD_DIM), lambda b, h: (b, 0, h)),
        pl.BlockSpec((1, tc, HEAD_DIM), lambda b, h: (b, 0, h)),
    ]
    out_specs = [pl.BlockSpec((1, t, gw), lambda b, h: (b, 0, h))]
    out_shape = [jax.ShapeDtypeStruct((bsz, t, nq), BF16)]
    args = [sink, q, k, v, kc, vc]
    if need_ctx:
        in_specs.append(pl.BlockSpec((1, tc, gw), lambda b, h: (b, 0, h)))
        out_specs.append(pl.BlockSpec((1, tc, gw), lambda b, h: (b, 0, h)))
        out_shape.append(jax.ShapeDtypeStruct((bsz, tc, nq), BF16))
        args.append(qc)
    outs = pl.pallas_call(
        functools.partial(_attn_kernel, group=group, need_ctx=need_ctx),
        grid=(bsz, N_KV_HEADS),
        in_specs=in_specs,
        out_specs=out_specs,
        out_shape=out_shape,
        scratch_shapes=[
            pltpu.VMEM((t + 2 * QBLOCK, HEAD_DIM), BF16),
            pltpu.VMEM((t + 2 * QBLOCK, HEAD_DIM), BF16),
        ],
        compiler_params=_cparams(2),
        name="attn",
    )(*args)
    return (outs[0], outs[1]) if need_ctx else (outs[0], None)


def _rope_tables(t):
    half = HEAD_DIM // 2
    freqs = ROPE_BASE ** (-jnp.arange(0, half, 2, dtype=F32) / half)
    pos = jnp.arange(t)
    rows = (pos // GRID_W).astype(F32)[:, None] * freqs
    cols = (pos % GRID_W).astype(F32)[:, None] * freqs
    cos = jnp.concatenate([jnp.cos(rows), jnp.cos(rows), jnp.cos(cols), jnp.cos(cols)], axis=1)
    sin = jnp.concatenate([-jnp.sin(rows), jnp.sin(rows), -jnp.sin(cols), jnp.sin(cols)], axis=1)
    return cos, sin


def _count(mask):
    return jnp.sum(jnp.where(mask, 1.0, 0.0), axis=1, keepdims=True)


def _route_kernel(x_ref, sh_ref, sc_ref, ng_ref, rw_ref, *rest, cap, aliased):
    if aliased:
        _, _, xg_ref, gs_ref, slot_ref, h_s, aff_s, slotp_s, slot_s, gate_s = rest
    else:
        xg_ref, gs_ref, slot_ref, h_s, aff_s, slotp_s, slot_s, gate_s = rest
    e = pl.program_id(1)
    t = x_ref.shape[1]
    n_exp = slot_s.shape[0]
    chunk = min(t, ROW_CHUNK)

    @pl.when(e == 0)
    def _():
        rw = rw_ref[0]
        rw_hi = rw.astype(BF16)
        rw_lo = (rw - rw_hi.astype(F32)).astype(BF16)
        lane = lax.broadcasted_iota(jnp.int32, (chunk, LANES), 1)
        for r0 in range(0, t, chunk):
            h = _norm_mod(x_ref[0, r0:r0 + chunk], ng_ref[0], sh_ref[0], sc_ref[0])
            h_hi = h.astype(BF16)
            h_lo = (h - h_hi.astype(F32)).astype(BF16)
            h_s[r0:r0 + chunk] = h_hi
            logits = _dot(h_hi, rw_hi) + (_dot(h_lo, rw_hi) + _dot(h_hi, rw_lo))
            logits = jnp.where(lane < n_exp, logits, NEG_INF)
            ex = jnp.exp(logits - jnp.max(logits, axis=1, keepdims=True))
            aff = ex / jnp.sum(ex, axis=1, keepdims=True)
            aff_s[:, r0:r0 + chunk] = aff.T
        aff_t = aff_s[0:n_exp, :]

        def as_f32(word):
            return lax.bitcast_convert_type(word, F32)

        def kth_body(i, v):
            cand = v | jnp.left_shift(jnp.int32(1), 29 - i)
            return jnp.where(_count(aff_t >= as_f32(cand)) >= cap, cand, v)

        kth = lax.fori_loop(0, 30, kth_body, jnp.zeros((n_exp, 1), jnp.int32))
        above = aff_t >= as_f32(kth + 1)
        tie = (aff_t >= as_f32(kth)) & jnp.logical_not(above)
        need = cap - _count(above)
        idx = lax.broadcasted_iota(jnp.int32, aff_t.shape, 1)
        idx_bits = max(1, (t - 1).bit_length())

        def tie_body(i, j):
            cand = j | jnp.left_shift(jnp.int32(1), idx_bits - 1 - i)
            return jnp.where(_count(tie & (idx < cand)) < need, cand, j)

        last = lax.fori_loop(0, idx_bits, tie_body, jnp.zeros((n_exp, 1), jnp.int32))
        sel = jnp.where(above | (tie & (idx <= last)), 1.0, 0.0)

        upper = jnp.where(lax.broadcasted_iota(jnp.int32, (chunk, chunk), 0)
                          < lax.broadcasted_iota(jnp.int32, (chunk, chunk), 1), 1.0, 0.0).astype(BF16)
        running = jnp.zeros((n_exp, 1), F32)
        slotp_s[...] = jnp.full(slotp_s.shape, -1.0, F32)
        for r0 in range(0, t, chunk):
            sel_c = sel[:, r0:r0 + chunk]
            pos = _dot(sel_c.astype(BF16), upper) + running
            running = running + jnp.sum(sel_c, axis=1, keepdims=True)
            slotp_s[0:n_exp, r0:r0 + chunk] = jnp.where(sel_c > 0.5, pos, -1.0)
        gate_t = sel * aff_t
        for ee in range(n_exp):
            slot_s[ee] = slotp_s[ee:ee + 1, :]
            gate_s[ee] = gate_t[ee:ee + 1, :]
        for r0 in range(0, t, chunk):
            slot_ref[0, r0:r0 + chunk] = slotp_s[:, r0:r0 + chunk].T

    srow = slot_s[e]
    onehot = lax.broadcasted_iota(jnp.int32, (cap, t), 0).astype(F32) == srow
    xg_ref[0] = _dot(jnp.where(onehot, 1.0, 0.0).astype(BF16), h_s[...]).astype(BF16)
    g = jnp.sum(jnp.where(onehot, gate_s[e], 0.0), axis=1, keepdims=True)
    gs_ref[0] = jnp.broadcast_to(g, (cap, LANES))


def _route_call(x, sh, sc, norm_g, l, router_p, n_exp, cap, total_rows, row_off, prev=None):
    bsz, t, d = x.shape
    vec = lambda b, e: (b, 0, 0)
    blk_off = row_off // cap
    in_specs = [
        pl.BlockSpec((1, t, d), vec),
        pl.BlockSpec((1, 1, d), vec),
        pl.BlockSpec((1, 1, d), vec),
        pl.BlockSpec((1, 1, d), lambda b, e: (l, 0, 0)),
        pl.BlockSpec((1, d, LANES), lambda b, e: (l, 0, 0)),
    ]
    args = [x, sh, sc, norm_g, router_p]
    aliases = {}
    aliased = prev is not None
    if aliased:
        in_specs += [pl.BlockSpec(memory_space=pl.ANY), pl.BlockSpec(memory_space=pl.ANY)]
        args += list(prev)
        aliases = {5: 0, 6: 1}
    return pl.pallas_call(
        functools.partial(_route_kernel, cap=cap, aliased=aliased),
        grid=(bsz, n_exp),
        in_specs=in_specs,
        out_specs=[
            pl.BlockSpec((1, cap, d), lambda b, e: (e, blk_off + b, 0)),
            pl.BlockSpec((1, cap, LANES), lambda b, e: (e, blk_off + b, 0)),
            pl.BlockSpec((1, t, LANES), vec),
        ],
        out_shape=[
            jax.ShapeDtypeStruct((n_exp, total_rows, d), BF16),
            jax.ShapeDtypeStruct((n_exp, total_rows, LANES), F32),
            jax.ShapeDtypeStruct((bsz, t, LANES), F32),
        ],
        scratch_shapes=[
            pltpu.VMEM((t, d), BF16),
            pltpu.VMEM((LANES, t), F32),
            pltpu.VMEM((LANES, t), F32),
            pltpu.VMEM((n_exp, 1, t), F32),
            pltpu.VMEM((n_exp, 1, t), F32),
        ],
        input_output_aliases=aliases,
        compiler_params=_cparams(2),
        name="route",
    )(*args)


def _ffn_kernel(x_ref, gs_ref, wg_ref, wu_ref, wd_ref, y_ref, w_s):
    @pl.when(pl.program_id(1) == 0)
    def _():
        w_s[0] = wg_ref[0, 0].astype(BF16)
        w_s[1] = wu_ref[0, 0].astype(BF16)
        w_s[2] = wd_ref[0, 0].astype(BF16)

    x = x_ref[0]
    a = _dot(x, w_s[0])
    u = _dot(x, w_s[1])
    hmid = (jax.nn.silu(a) * u).astype(BF16)
    y_ref[0] = (_dot(hmid, w_s[2]) * gs_ref[0][:, 0:1]).astype(BF16)


def _ffn_call(xg, gs, w_gate, w_up, w_down, l):
    n_exp, rows, d = xg.shape
    f = w_gate.shape[-1]
    tm = 768 if rows % 768 == 0 else min(rows, 512)
    return pl.pallas_call(
        _ffn_kernel,
        grid=(n_exp, rows // tm),
        in_specs=[
            pl.BlockSpec((1, tm, d), lambda e, r: (e, r, 0)),
            pl.BlockSpec((1, tm, LANES), lambda e, r: (e, r, 0)),
            pl.BlockSpec((1, 1, d, f), lambda e, r: (l, e, 0, 0)),
            pl.BlockSpec((1, 1, d, f), lambda e, r: (l, e, 0, 0)),
            pl.BlockSpec((1, 1, f, d), lambda e, r: (l, e, 0, 0)),
        ],
        out_specs=pl.BlockSpec((1, tm, d), lambda e, r: (e, r, 0)),
        out_shape=jax.ShapeDtypeStruct((n_exp, rows, d), BF16),
        scratch_shapes=[pltpu.VMEM((3, d, f), BF16)],
        compiler_params=_cparams(2),
        name="ffn",
    )(xg, gs, w_gate, w_up, w_down)


def _combine_kernel(y_ref, slot_ref, x_ref, g_ref, fg_ref, o_ref, *, final_norm):
    n_exp, cap, _ = y_ref.shape
    tk = x_ref.shape[1]
    slots = slot_ref[0]
    lane = lax.broadcasted_iota(jnp.int32, (tk, cap), 1).astype(F32)
    acc = jnp.zeros(x_ref.shape[1:], F32)
    for e in range(n_exp):
        onehot = jnp.where(lane == slots[:, e:e + 1], 1.0, 0.0).astype(BF16)
        acc = acc + _dot(onehot, y_ref[e])
    out = x_ref[0] + g_ref[0] * acc
    if final_norm:
        ms = jnp.mean(out * out, axis=-1, keepdims=True)
        out = out * lax.rsqrt(ms + NORM_EPS) * fg_ref[...]
    o_ref[0] = out


def _combine_call(y, slot, x, gate, final_g, cap, row_off, final_norm):
    bsz, t, d = x.shape
    n_exp = y.shape[0]
    tk = min(t, 512)
    blk_off = row_off // cap
    return pl.pallas_call(
        functools.partial(_combine_kernel, final_norm=final_norm),
        grid=(bsz, t // tk),
        in_specs=[
            pl.BlockSpec((n_exp, cap, d), lambda b, i: (0, blk_off + b, 0)),
            pl.BlockSpec((1, tk, LANES), lambda b, i: (b, i, 0)),
            pl.BlockSpec((1, tk, d), lambda b, i: (b, i, 0)),
            pl.BlockSpec((1, 1, d), lambda b, i: (b, 0, 0)),
            pl.BlockSpec((1, d), lambda b, i: (0, 0)),
        ],
        out_specs=pl.BlockSpec((1, tk, d), lambda b, i: (b, i, 0)),
        out_shape=jax.ShapeDtypeStruct(x.shape, F32),
        compiler_params=_cparams(2),
        name="combine",
    )(y, slot, x, gate, final_g)


def kernel(x, c, ctx, c_ctx, ada_w, ada_b, norm1_g, norm2_g, lru_w_in, lru_conv_w, lru_conv_b,
           lru_gate_w, lru_gate_b, lru_lambda, lru_w_out, attn_w_qkv, attn_sink, attn_w_o,
           moe_router, moe_w_gate, moe_w_up, moe_w_down, final_g):
    bsz, t, d = x.shape
    tc = ctx.shape[1]
    depth = ada_w.shape[0]
    n_exp = moe_router.shape[-1]
    cap_l = CAPACITY_FACTOR * t // n_exp
    cap_c = CAPACITY_FACTOR * tc // n_exp

    cond_rows = 2 * SUBLANES
    cond = jnp.zeros((cond_rows, d), F32).at[:bsz].set(c).at[bsz].set(c_ctx)
    mod = _ada_call(cond, ada_w, ada_b).reshape(depth, cond_rows, 6, 1, d)
    norm1 = norm1_g.reshape(depth, 1, d)
    norm2 = norm2_g.reshape(depth, 1, d)
    router_p = jnp.pad(moe_router, ((0, 0), (0, 0), (0, LANES - n_exp)))
    final_g2 = final_g.reshape(1, d)
    cos, sin_signed = _rope_tables(t)
    ones_c = jnp.ones((tc, HEAD_DIM), F32)

    for l in range(depth):
        need_ctx = l < depth - 1
        lat = [mod[l, :bsz, i] for i in range(6)]
        con = [jnp.broadcast_to(mod[l, bsz, i], (bsz, 1, d)) for i in range(6)]
        j = l // 2
        if l % 2 == 0:
            s_l, s_c = _lru_call(x, ctx, lat[0], lat[1], con[0], con[1], norm1, l,
                                 lru_w_in[j].astype(BF16), lru_conv_w[j], lru_conv_b[j],
                                 lru_gate_w[j].astype(BF16), lru_gate_b[j], lru_lambda[j])
            w_out = lru_w_out[j].astype(BF16)
        else:
            w_qkv = attn_w_qkv[j].astype(BF16)
            q, k, v = _qkv_call(x, lat[0], lat[1], norm1, l, w_qkv, cos, sin_signed, True)
            qc, kc, vc = _qkv_call(ctx, con[0], con[1], norm1, l, w_qkv, ones_c, ones_c, False)
            s_l, s_c = _attn_call(attn_sink[j], q, k, v, qc, kc, vc, need_ctx)
            w_out = attn_w_o[j].astype(BF16)
        x = _resid_call(s_l, w_out, x, lat[2])
        if need_ctx:
            ctx = _resid_call(s_c, w_out, ctx, con[2])

        rows = bsz * cap_l + (bsz * cap_c if need_ctx else 0)
        xg, gs, slot_l = _route_call(x, lat[3], lat[4], norm2, l, router_p, n_exp, cap_l, rows, 0)
        if need_ctx:
            xg, gs, slot_c = _route_call(ctx, con[3], con[4], norm2, l, router_p, n_exp, cap_c,
                                         rows, bsz * cap_l, prev=(xg, gs))
        y = _ffn_call(xg, gs, moe_w_gate, moe_w_up, moe_w_down, l)
        x = _combine_call(y, slot_l, x, lat[5], final_g2, cap_l, 0, not need_ctx)
        if need_ctx:
            ctx = _combine_call(y, slot_c, ctx, con[5], final_g2, cap_c, bsz * cap_l, False)
    return x
```

```python
import functools
import math

import jax
import jax.numpy as jnp
from jax import lax
from jax.experimental import pallas as pl
from jax.experimental.pallas import tpu as pltpu

F32 = jnp.float32
BF16 = jnp.bfloat16

LANES = 128
SUBLANES = 8
VMEM_LIMIT_BYTES = 58 * 1024 * 1024

NORM_EPS = 1e-6
NEG_INF = -1e30
LRU_C = 8.0
LRU_BLOCK_W = 128
CONV_W = 4
CONV_HALO = 16
HEAD_DIM = 128
N_KV_HEADS = 2
QBLOCK = 128
WINDOW = 128
GRID_W = 64
ROPE_BASE = 10000.0
CAPACITY_FACTOR = 2

ROW_CHUNK = 256
SCAN_UNROLL = 4
MOD_TILE = 1536
SEARCH_BITS = 3
ROUTE_OUT_BYTES = 4 * 1024 * 1024


def _cparams(n_axes):
    return pltpu.CompilerParams(
        dimension_semantics=("arbitrary",) * n_axes, vmem_limit_bytes=VMEM_LIMIT_BYTES)


def _dot(a, b):
    return jnp.dot(a, b, preferred_element_type=F32)


def _dot_nt(a, b):
    return lax.dot_general(a, b, (((1,), (1,)), ((), ())), preferred_element_type=F32)


def _sigmoid(x):
    return 0.5 * jnp.tanh(0.5 * x) + 0.5


def _sqrt_nonneg(z):
    return jnp.where(z > 0.0, z * lax.rsqrt(z), 0.0)


def _norm_mod(x, g, shift, scale):
    ms = jnp.mean(x * x, axis=-1, keepdims=True)
    y = x * lax.rsqrt(ms + NORM_EPS) * g
    return y * (1.0 + scale) + shift


def _ada_kernel(c_ref, w_ref, b_ref, o_ref):
    c = c_ref[...]
    s = (c * _sigmoid(c)).astype(BF16)
    o_ref[0] = _dot(s, w_ref[0].astype(BF16)) + b_ref[0]


def _ada_call(cond, ada_w, ada_b):
    n_layers, d, n_out = ada_w.shape
    rows = cond.shape[0]
    return pl.pallas_call(
        _ada_kernel,
        grid=(n_layers, n_out // MOD_TILE),
        in_specs=[
            pl.BlockSpec((rows, d), lambda l, n: (0, 0)),
            pl.BlockSpec((1, d, MOD_TILE), lambda l, n: (l, 0, n)),
            pl.BlockSpec((1, 1, MOD_TILE), lambda l, n: (l, 0, n)),
        ],
        out_specs=pl.BlockSpec((1, rows, MOD_TILE), lambda l, n: (l, 0, n)),
        out_shape=jax.ShapeDtypeStruct((n_layers, rows, n_out), F32),
        compiler_params=_cparams(2),
        name="ada",
    )(cond, ada_w, ada_b.reshape(n_layers, 1, n_out))


def _resid_kernel(a_ref, w_ref, x_ref, g_ref, o_ref):
    o_ref[0] = x_ref[0] + g_ref[0] * _dot(a_ref[0], w_ref[...])


def _resid_call(act, w, x, gate):
    bsz, t, d = x.shape
    k = act.shape[-1]
    tm = min(t, 512)
    return pl.pallas_call(
        _resid_kernel,
        grid=(bsz, t // tm),
        in_specs=[
            pl.BlockSpec((1, tm, k), lambda b, i: (b, i, 0)),
            pl.BlockSpec((k, d), lambda b, i: (0, 0)),
            pl.BlockSpec((1, tm, d), lambda b, i: (b, i, 0)),
            pl.BlockSpec((1, 1, d), lambda b, i: (b, 0, 0)),
        ],
        out_specs=pl.BlockSpec((1, tm, d), lambda b, i: (b, i, 0)),
        out_shape=jax.ShapeDtypeStruct(x.shape, F32),
        compiler_params=_cparams(2),
        name="resid",
    )(act, w, x, gate)


LRU_CB = 2 * LRU_BLOCK_W


def _log_sigmoid(x):
    return jnp.minimum(x, 0.0) - jnp.log1p(jnp.exp(-jnp.abs(x)))


def _conv_time(u, cw, cb):
    n = u.shape[0]
    left = (CONV_W - 1) // 2
    acc = cb + u * cw[left:left + 1]
    for k in range(CONV_W):
        off = k - left
        if off != 0:
            acc = acc + pltpu.roll(u, (-off) % n, 0) * cw[k:k + 1]
    return acc[0:n - CONV_HALO]


def _tile_scan(a, b, carry, reverse):
    row = lax.broadcasted_iota(jnp.int32, a.shape, 0)
    for k in (1, 2, 4):
        shift = (SUBLANES - k) if reverse else k
        a_sh = pltpu.roll(a, shift, 0)
        b_sh = pltpu.roll(b, shift, 0)
        m = (row < SUBLANES - k) if reverse else (row >= k)
        b = jnp.where(m, a * b_sh + b, b)
        a = jnp.where(m, a * a_sh, a)
    h = a * carry + b
    new_carry = h[0:1] if reverse else h[SUBLANES - 1:SUBLANES]
    return h, new_carry


def _lru_kernel(xl_ref, xc_ref, shl_ref, scl_ref, shc_ref, scc_ref, ng_ref, wy_ref, wx_ref,
                cw_ref, cb_ref, gw_ref, gb_ref, lam_ref, ol_ref, oc_ref,
                hl_s, hc_s, a_s, b_s, hs_s, y_s):
    j = pl.program_id(1)
    t_lat = xl_ref.shape[1]
    t_ctx = xc_ref.shape[1]

    @pl.when(j == 0)
    def _():
        g = ng_ref[0]
        for x_ref, h_s, sh_ref, sc_ref in ((xl_ref, hl_s, shl_ref, scl_ref),
                                           (xc_ref, hc_s, shc_ref, scc_ref)):
            n = x_ref.shape[1]
            step = min(n, ROW_CHUNK)
            for r0 in range(0, n, step):
                h_s[r0:r0 + step] = _norm_mod(
                    x_ref[0, r0:r0 + step], g, sh_ref[0], sc_ref[0]).astype(BF16)
            h_s[n:n + CONV_HALO] = jnp.zeros((CONV_HALO, h_s.shape[1]), BF16)

    rate = (LRU_C / math.log(2.0)) * _log_sigmoid(lam_ref[...])

    def branches(h_s, n):
        step = min(n, ROW_CHUNK)
        for r0 in range(0, n, step):
            y_s[r0:r0 + step] = jax.nn.gelu(_dot(h_s[r0:r0 + step], wy_ref[...]))
        xb = _conv_time(_dot(h_s[0:n + CONV_HALO], wx_ref[...]), cw_ref[...], cb_ref[...])
        for kb in range(2):
            xk = xb[:, kb * LRU_BLOCK_W:(kb + 1) * LRU_BLOCK_W]
            xk16 = xk.astype(BF16)
            for d in range(2):
                gates = _dot(xk16, gw_ref[d, kb]) + gb_ref[d, kb]
                r = _sigmoid(gates[:, :LRU_BLOCK_W])
                i = _sigmoid(gates[:, LRU_BLOCK_W:])
                a = jnp.exp2(r * rate[d:d + 1, kb * LRU_BLOCK_W:(kb + 1) * LRU_BLOCK_W])
                a_s[d, kb, 0:n] = a
                b_s[d, kb, 0:n] = _sqrt_nonneg(1.0 - a * a) * (i * xk)

    def scan(n, carries):
        nt = n // SUBLANES

        def body(i, cs):
            out = []
            for d in range(2):
                ti = (nt - 1 - i) if d == 1 else i
                rows = pl.ds(pl.multiple_of(ti * SUBLANES, SUBLANES), SUBLANES)
                for kb in range(2):
                    h, c = _tile_scan(a_s[d, kb, rows], b_s[d, kb, rows], cs[2 * d + kb], d == 1)
                    hs_s[d, kb, rows] = h
                    out.append(c)
            return tuple(out)

        return lax.fori_loop(0, nt, body, carries, unroll=SCAN_UNROLL)

    def emit(o_ref, n):
        for kb in range(2):
            s = hs_s[0, kb, 0:n] + hs_s[1, kb, 0:n]
            y = y_s[0:n, kb * LRU_BLOCK_W:(kb + 1) * LRU_BLOCK_W]
            o_ref[0, :, kb * LRU_BLOCK_W:(kb + 1) * LRU_BLOCK_W] = (s * y).astype(BF16)

    zero = jnp.zeros((1, LRU_BLOCK_W), F32)
    branches(hc_s, t_ctx)
    carries = scan(t_ctx, (zero,) * 4)
    emit(oc_ref, t_ctx)
    branches(hl_s, t_lat)
    scan(t_lat, carries)
    emit(ol_ref, t_lat)


def _lru_call(x, ctx, sh_l, sc_l, sh_c, sc_c, norm_g, l, w_in, conv_w, conv_b, gate_w, gate_b, lam):
    bsz, t, d = x.shape
    tc = ctx.shape[1]
    width = w_in.shape[1] // 2
    nblk = width // LRU_CB
    gbk = LRU_CB // LRU_BLOCK_W
    vec = lambda b, c: (b, 0, 0)
    return pl.pallas_call(
        _lru_kernel,
        grid=(bsz, nblk),
        in_specs=[
            pl.BlockSpec((1, t, d), vec),
            pl.BlockSpec((1, tc, d), vec),
            pl.BlockSpec((1, 1, d), vec),
            pl.BlockSpec((1, 1, d), vec),
            pl.BlockSpec((1, 1, d), vec),
            pl.BlockSpec((1, 1, d), vec),
            pl.BlockSpec((1, 1, d), lambda b, c: (l, 0, 0)),
            pl.BlockSpec((d, LRU_CB), lambda b, c: (0, c)),
            pl.BlockSpec((d, LRU_CB), lambda b, c: (0, nblk + c)),
            pl.BlockSpec((CONV_W, LRU_CB), lambda b, c: (0, c)),
            pl.BlockSpec((1, LRU_CB), lambda b, c: (0, c)),
            pl.BlockSpec((2, gbk, LRU_BLOCK_W, 2 * LRU_BLOCK_W), lambda b, c: (0, c, 0, 0)),
            pl.BlockSpec((2, gbk, 1, 2 * LRU_BLOCK_W), lambda b, c: (0, c, 0, 0)),
            pl.BlockSpec((2, LRU_CB), lambda b, c: (0, c)),
        ],
        out_specs=[
            pl.BlockSpec((1, t, LRU_CB), lambda b, c: (b, 0, c)),
            pl.BlockSpec((1, tc, LRU_CB), lambda b, c: (b, 0, c)),
        ],
        out_shape=[
            jax.ShapeDtypeStruct((bsz, t, width), BF16),
            jax.ShapeDtypeStruct((bsz, tc, width), BF16),
        ],
        scratch_shapes=[
            pltpu.VMEM((t + CONV_HALO, d), BF16),
            pltpu.VMEM((tc + CONV_HALO, d), BF16),
            pltpu.VMEM((2, gbk, t, LRU_BLOCK_W), F32),
            pltpu.VMEM((2, gbk, t, LRU_BLOCK_W), F32),
            pltpu.VMEM((2, gbk, t, LRU_BLOCK_W), F32),
            pltpu.VMEM((t, LRU_CB), F32),
        ],
        compiler_params=_cparams(2),
        name="lru",
    )(x, ctx, sh_l, sc_l, sh_c, sc_c, norm_g, w_in, w_in, conv_w, conv_b.reshape(1, width),
      gate_w, gate_b.reshape(2, -1, 1, 2 * LRU_BLOCK_W), lam)


def _rope(x, cos, sin_signed):
    lane = lax.broadcasted_iota(jnp.int32, x.shape, 1)
    quarter = HEAD_DIM // 4
    partner = jnp.where((lane & quarter) == 0,
                        pltpu.roll(x, HEAD_DIM - quarter, 1), pltpu.roll(x, quarter, 1))
    return x * cos + partner * sin_signed


def _qkv_kernel(x_ref, sh_ref, sc_ref, ng_ref, w_ref, cos_ref, sin_ref, q_ref, k_ref, v_ref,
                *, n_heads, rotary):
    h = _norm_mod(x_ref[0], ng_ref[0], sh_ref[0], sc_ref[0]).astype(BF16)
    u = _dot(h, w_ref[...])
    scale = HEAD_DIM ** -0.5
    nq = n_heads * HEAD_DIM
    nk = N_KV_HEADS * HEAD_DIM
    for hd in range(n_heads + N_KV_HEADS):
        c = u[:, hd * HEAD_DIM:(hd + 1) * HEAD_DIM]
        if rotary:
            c = _rope(c, cos_ref[...], sin_ref[...])
        if hd < n_heads:
            q_ref[0, :, hd * HEAD_DIM:(hd + 1) * HEAD_DIM] = (c * scale).astype(BF16)
        else:
            kk = hd - n_heads
            k_ref[0, :, kk * HEAD_DIM:(kk + 1) * HEAD_DIM] = c.astype(BF16)
    v_ref[0] = u[:, nq + nk:].astype(BF16)


def _qkv_call(x, sh, sc, norm_g, l, w_qkv, cos, sin_signed, rotary):
    bsz, t, d = x.shape
    nk = N_KV_HEADS * HEAD_DIM
    nq = w_qkv.shape[1] - 2 * nk
    tm = min(t, 512)
    vec = lambda b, i: (b, 0, 0)
    return pl.pallas_call(
        functools.partial(_qkv_kernel, n_heads=nq // HEAD_DIM, rotary=rotary),
        grid=(bsz, t // tm),
        in_specs=[
            pl.BlockSpec((1, tm, d), lambda b, i: (b, i, 0)),
            pl.BlockSpec((1, 1, d), vec),
            pl.BlockSpec((1, 1, d), vec),
            pl.BlockSpec((1, 1, d), lambda b, i: (l, 0, 0)),
            pl.BlockSpec(w_qkv.shape, lambda b, i: (0, 0)),
            pl.BlockSpec((tm, HEAD_DIM), lambda b, i: (i, 0)),
            pl.BlockSpec((tm, HEAD_DIM), lambda b, i: (i, 0)),
        ],
        out_specs=[
            pl.BlockSpec((1, tm, nq), lambda b, i: (b, i, 0)),
            pl.BlockSpec((1, tm, nk), lambda b, i: (b, i, 0)),
            pl.BlockSpec((1, tm, nk), lambda b, i: (b, i, 0)),
        ],
        out_shape=[
            jax.ShapeDtypeStruct((bsz, t, nq), BF16),
            jax.ShapeDtypeStruct((bsz, t, nk), BF16),
            jax.ShapeDtypeStruct((bsz, t, nk), BF16),
        ],
        compiler_params=_cparams(2),
        name="qkv",
    )(x, sh, sc, norm_g, w_qkv, cos, sin_signed)


def _stack_heads(x, group):
    return jnp.concatenate([x[:, g * HEAD_DIM:(g + 1) * HEAD_DIM] for g in range(group)], axis=0)


def _attn_kernel(sink_ref, q_ref, k_ref, v_ref, kc_ref, vc_ref, *rest, group, need_ctx):
    if need_ctx:
        qc_ref, o_ref, oc_ref = rest
    else:
        (o_ref,) = rest
    kh = pl.program_id(1)
    t = k_ref.shape[1]
    nblk = t // QBLOCK

    def sink_col(rows_per_head):
        row = lax.broadcasted_iota(jnp.int32, (group * rows_per_head, 1), 0)
        col = jnp.zeros((group * rows_per_head, 1), F32)
        for g in range(group):
            col = jnp.where(row >= g * rows_per_head, sink_ref[kh * group + g], col)
        return col

    kc = kc_ref[0]
    vc = vc_ref[0]
    sink_q = sink_col(QBLOCK)

    def softmax_pv(scores_values, sink):
        m = sink
        for sc, _ in scores_values:
            m = jnp.maximum(m, jnp.max(sc, axis=1, keepdims=True))
        den = jnp.exp(sink - m)
        acc = None
        for sc, val in scores_values:
            p = jnp.exp(sc - m)
            den = den + jnp.sum(p, axis=1, keepdims=True)
            pv = _dot(p.astype(BF16), val)
            acc = pv if acc is None else acc + pv
        return acc / den

    def attend(r0, has_prev, has_next):
        lo = r0 - QBLOCK if has_prev else r0
        if not isinstance(lo, int):
            lo = pl.multiple_of(lo, QBLOCK)
        width = QBLOCK * (1 + has_prev + has_next)
        qs = _stack_heads(q_ref[0, pl.ds(r0, QBLOCK), :], group)
        s = _dot_nt(qs, k_ref[0, pl.ds(lo, width), :])
        qi = lax.broadcasted_iota(jnp.int32, (group * QBLOCK, QBLOCK), 0) & (QBLOCK - 1)
        kj = lax.broadcasted_iota(jnp.int32, (group * QBLOCK, QBLOCK), 1)
        pieces = []
        if has_prev:
            pieces.append(jnp.where(QBLOCK + qi - kj <= WINDOW, s[:, :QBLOCK], NEG_INF))
        c0 = QBLOCK * has_prev
        pieces.append(s[:, c0:c0 + QBLOCK])
        if has_next:
            pieces.append(jnp.where(QBLOCK + kj - qi <= WINDOW, s[:, c0 + QBLOCK:], NEG_INF))
        s = jnp.concatenate(pieces, axis=1)
        o = softmax_pv([(s, v_ref[0, pl.ds(lo, width), :]), (_dot_nt(qs, kc), vc)], sink_q)
        for g in range(group):
            o_ref[0, pl.ds(r0, QBLOCK), g * HEAD_DIM:(g + 1) * HEAD_DIM] = (
                o[g * QBLOCK:(g + 1) * QBLOCK].astype(BF16))

    attend(0, False, nblk > 1)
    if nblk > 1:
        attend((nblk - 1) * QBLOCK, True, False)

    def body(n, carry):
        attend(pl.multiple_of(n * QBLOCK, QBLOCK), True, True)
        return carry

    lax.fori_loop(1, nblk - 1, body, 0, unroll=2)

    if need_ctx:
        tc = qc_ref.shape[1]
        o = softmax_pv([(_dot_nt(_stack_heads(qc_ref[0], group), kc), vc)], sink_col(tc))
        for g in range(group):
            oc_ref[0, :, g * HEAD_DIM:(g + 1) * HEAD_DIM] = o[g * tc:(g + 1) * tc].astype(BF16)


def _attn_call(sink, q, k, v, qc, kc, vc, need_ctx):
    bsz, t, nq = q.shape
    tc = kc.shape[1]
    group = nq // HEAD_DIM // N_KV_HEADS
    gw = group * HEAD_DIM
    in_specs = [
        pl.BlockSpec(memory_space=pltpu.SMEM),
        pl.BlockSpec((1, t, gw), lambda b, h: (b, 0, h)),
        pl.BlockSpec((1, t, HEAD_DIM), lambda b, h: (b, 0, h)),
        pl.BlockSpec((1, t, HEAD_DIM), lambda b, h: (b, 0, h)),
        pl.BlockSpec((1, tc, HEAD_DIM), lambda b, h: (b, 0, h)),
        pl.BlockSpec((1, tc, HEAD_DIM), lambda b, h: (b, 0, h)),
    ]
    out_specs = [pl.BlockSpec((1, t, gw), lambda b, h: (b, 0, h))]
    out_shape = [jax.ShapeDtypeStruct((bsz, t, nq), BF16)]
    args = [sink, q, k, v, kc, vc]
    if need_ctx:
        in_specs.append(pl.BlockSpec((1, tc, gw), lambda b, h: (b, 0, h)))
        out_specs.append(pl.BlockSpec((1, tc, gw), lambda b, h: (b, 0, h)))
        out_shape.append(jax.ShapeDtypeStruct((bsz, tc, nq), BF16))
        args.append(qc)
    outs = pl.pallas_call(
        functools.partial(_attn_kernel, group=group, need_ctx=need_ctx),
        grid=(bsz, N_KV_HEADS),
        in_specs=in_specs,
        out_specs=out_specs,
        out_shape=out_shape,
        compiler_params=_cparams(2),
        name="attn",
    )(*args)
    return (outs[0], outs[1]) if need_ctx else (outs[0], None)


def _rope_tables(t):
    half = HEAD_DIM // 2
    freqs = ROPE_BASE ** (-jnp.arange(0, half, 2, dtype=F32) / half)
    pos = jnp.arange(t)
    rows = (pos // GRID_W).astype(F32)[:, None] * freqs
    cols = (pos % GRID_W).astype(F32)[:, None] * freqs
    cos = jnp.concatenate([jnp.cos(rows), jnp.cos(rows), jnp.cos(cols), jnp.cos(cols)], axis=1)
    sin = jnp.concatenate([-jnp.sin(rows), jnp.sin(rows), -jnp.sin(cols), jnp.sin(cols)], axis=1)
    return cos, sin


def _count(mask):
    return jnp.sum(jnp.where(mask, 1.0, 0.0), axis=1, keepdims=True)


def _route_kernel(x_ref, sh_ref, sc_ref, ng_ref, rw_ref, *rest, cap, aliased):
    if aliased:
        _, _, xg_ref, gs_ref, slot_ref, h_s, aff_s, slotp_s, slot_s, gate_s = rest
    else:
        xg_ref, gs_ref, slot_ref, h_s, aff_s, slotp_s, slot_s, gate_s = rest
    e = pl.program_id(1)
    t = x_ref.shape[1]
    n_exp = slot_s.shape[0]
    chunk = min(t, ROW_CHUNK)

    @pl.when(e == 0)
    def _():
        rw = rw_ref[0]
        rw_hi = rw.astype(BF16)
        rw_lo = (rw - rw_hi.astype(F32)).astype(BF16)
        lane = lax.broadcasted_iota(jnp.int32, (chunk, LANES), 1)
        for r0 in range(0, t, chunk):
            h = _norm_mod(x_ref[0, r0:r0 + chunk], ng_ref[0], sh_ref[0], sc_ref[0])
            h_hi = h.astype(BF16)
            h_lo = (h - h_hi.astype(F32)).astype(BF16)
            h_s[r0:r0 + chunk] = h_hi
            logits = _dot(h_hi, rw_hi) + (_dot(h_lo, rw_hi) + _dot(h_hi, rw_lo))
            logits = jnp.where(lane < n_exp, logits, NEG_INF)
            ex = jnp.exp(logits - jnp.max(logits, axis=1, keepdims=True))
            aff = ex / jnp.sum(ex, axis=1, keepdims=True)
            aff_s[:, r0:r0 + chunk] = aff.T
        aff_t = aff_s[0:n_exp, :]

        def as_f32(word):
            return lax.bitcast_convert_type(word, F32)

        def search(n_digits, accept):
            def body(i, w):
                shift = SEARCH_BITS * (n_digits - 1 - i)
                digit = jnp.zeros_like(w)
                for c in range(1, 1 << SEARCH_BITS):
                    ok = accept(w | jnp.left_shift(jnp.int32(c), shift))
                    digit = digit + jnp.where(ok, 1, 0)
                return w | jnp.left_shift(digit, shift)

            return lax.fori_loop(0, n_digits, body, jnp.zeros((n_exp, 1), jnp.int32))

        kth = search(30 // SEARCH_BITS, lambda w: _count(aff_t >= as_f32(w)) >= cap)
        above = aff_t >= as_f32(kth + 1)
        tie = (aff_t >= as_f32(kth)) & jnp.logical_not(above)
        need = cap - _count(above)
        idx = lax.broadcasted_iota(jnp.int32, aff_t.shape, 1)
        idx_digits = -(-max(1, (t - 1).bit_length()) // SEARCH_BITS)
        last = search(idx_digits, lambda w: _count(tie & (idx < w)) < need)
        sel = jnp.where(above | (tie & (idx <= last)), 1.0, 0.0)

        upper = jnp.where(lax.broadcasted_iota(jnp.int32, (chunk, chunk), 0)
                          < lax.broadcasted_iota(jnp.int32, (chunk, chunk), 1), 1.0, 0.0).astype(BF16)
        running = jnp.zeros((n_exp, 1), F32)
        slotp_s[...] = jnp.full(slotp_s.shape, -1.0, F32)
        for r0 in range(0, t, chunk):
            sel_c = sel[:, r0:r0 + chunk]
            pos = _dot(sel_c.astype(BF16), upper) + running
            running = running + jnp.sum(sel_c, axis=1, keepdims=True)
            slotp_s[0:n_exp, r0:r0 + chunk] = jnp.where(sel_c > 0.5, pos, -1.0)
        gate_t = sel * aff_t
        for ee in range(n_exp):
            slot_s[ee] = slotp_s[ee:ee + 1, :]
            gate_s[ee] = gate_t[ee:ee + 1, :]
        for r0 in range(0, t, chunk):
            slot_ref[0, r0:r0 + chunk] = slotp_s[:, r0:r0 + chunk].T

    eps = xg_ref.shape[0]
    slot_id = lax.broadcasted_iota(jnp.int32, (cap, t), 0).astype(F32)
    for i in range(eps):
        onehot = slot_id == slot_s[e * eps + i]
        xg_ref[i] = _dot(jnp.where(onehot, 1.0, 0.0).astype(BF16), h_s[...]).astype(BF16)
        g = jnp.sum(jnp.where(onehot, gate_s[e * eps + i], 0.0), axis=1, keepdims=True)
        gs_ref[i] = jnp.broadcast_to(g, (cap, LANES))


def _route_call(x, sh, sc, norm_g, l, router_p, n_exp, cap, total_rows, row_off, prev=None):
    bsz, t, d = x.shape
    eps = max(1, min(n_exp, ROUTE_OUT_BYTES // (cap * d * 2)))
    vec = lambda b, e: (b, 0, 0)
    blk_off = row_off // cap
    in_specs = [
        pl.BlockSpec((1, t, d), vec),
        pl.BlockSpec((1, 1, d), vec),
        pl.BlockSpec((1, 1, d), vec),
        pl.BlockSpec((1, 1, d), lambda b, e: (l, 0, 0)),
        pl.BlockSpec((1, d, LANES), lambda b, e: (l, 0, 0)),
    ]
    args = [x, sh, sc, norm_g, router_p]
    aliases = {}
    aliased = prev is not None
    if aliased:
        in_specs += [pl.BlockSpec(memory_space=pl.ANY), pl.BlockSpec(memory_space=pl.ANY)]
        args += list(prev)
        aliases = {5: 0, 6: 1}
    return pl.pallas_call(
        functools.partial(_route_kernel, cap=cap, aliased=aliased),
        grid=(bsz, n_exp // eps),
        in_specs=in_specs,
        out_specs=[
            pl.BlockSpec((eps, cap, d), lambda b, e: (e, blk_off + b, 0)),
            pl.BlockSpec((eps, cap, LANES), lambda b, e: (e, blk_off + b, 0)),
            pl.BlockSpec((1, t, LANES), vec),
        ],
        out_shape=[
            jax.ShapeDtypeStruct((n_exp, total_rows, d), BF16),
            jax.ShapeDtypeStruct((n_exp, total_rows, LANES), F32),
            jax.ShapeDtypeStruct((bsz, t, LANES), F32),
        ],
        scratch_shapes=[
            pltpu.VMEM((t, d), BF16),
            pltpu.VMEM((LANES, t), F32),
            pltpu.VMEM((LANES, t), F32),
            pltpu.VMEM((n_exp, 1, t), F32),
            pltpu.VMEM((n_exp, 1, t), F32),
        ],
        input_output_aliases=aliases,
        compiler_params=_cparams(2),
        name="route",
    )(*args)


def _ffn_kernel(x_ref, gs_ref, wg_ref, wu_ref, wd_ref, y_ref, w_s):
    @pl.when(pl.program_id(1) == 0)
    def _():
        w_s[0] = wg_ref[0, 0].astype(BF16)
        w_s[1] = wu_ref[0, 0].astype(BF16)
        w_s[2] = wd_ref[0, 0].astype(BF16)

    x = x_ref[0]
    a = _dot(x, w_s[0])
    u = _dot(x, w_s[1])
    hmid = (a * _sigmoid(a) * u).astype(BF16)
    y_ref[0] = (_dot(hmid, w_s[2]) * gs_ref[0][:, 0:1]).astype(BF16)


def _ffn_call(xg, gs, w_gate, w_up, w_down, l):
    n_exp, rows, d = xg.shape
    f = w_gate.shape[-1]
    tm = 768 if rows % 768 == 0 else min(rows, 512)
    return pl.pallas_call(
        _ffn_kernel,
        grid=(n_exp, rows // tm),
        in_specs=[
            pl.BlockSpec((1, tm, d), lambda e, r: (e, r, 0)),
            pl.BlockSpec((1, tm, LANES), lambda e, r: (e, r, 0)),
            pl.BlockSpec((1, 1, d, f), lambda e, r: (l, e, 0, 0)),
            pl.BlockSpec((1, 1, d, f), lambda e, r: (l, e, 0, 0)),
            pl.BlockSpec((1, 1, f, d), lambda e, r: (l, e, 0, 0)),
        ],
        out_specs=pl.BlockSpec((1, tm, d), lambda e, r: (e, r, 0)),
        out_shape=jax.ShapeDtypeStruct((n_exp, rows, d), BF16),
        scratch_shapes=[pltpu.VMEM((3, d, f), BF16)],
        compiler_params=_cparams(2),
        name="ffn",
    )(xg, gs, w_gate, w_up, w_down)


def _combine_kernel(y_ref, slot_ref, x_ref, g_ref, fg_ref, o_ref, *, final_norm):
    n_exp, cap, _ = y_ref.shape
    tk = x_ref.shape[1]
    slots = slot_ref[0]
    lane = lax.broadcasted_iota(jnp.int32, (tk, cap), 1).astype(F32)
    acc = jnp.zeros(x_ref.shape[1:], F32)
    for e in range(n_exp):
        onehot = jnp.where(lane == slots[:, e:e + 1], 1.0, 0.0).astype(BF16)
        acc = acc + _dot(onehot, y_ref[e])
    out = x_ref[0] + g_ref[0] * acc
    if final_norm:
        ms = jnp.mean(out * out, axis=-1, keepdims=True)
        out = out * lax.rsqrt(ms + NORM_EPS) * fg_ref[...]
    o_ref[0] = out


def _combine_call(y, slot, x, gate, final_g, cap, row_off, final_norm):
    bsz, t, d = x.shape
    n_exp = y.shape[0]
    tk = min(t, 512)
    blk_off = row_off // cap
    return pl.pallas_call(
        functools.partial(_combine_kernel, final_norm=final_norm),
        grid=(bsz, t // tk),
        in_specs=[
            pl.BlockSpec((n_exp, cap, d), lambda b, i: (0, blk_off + b, 0)),
            pl.BlockSpec((1, tk, LANES), lambda b, i: (b, i, 0)),
            pl.BlockSpec((1, tk, d), lambda b, i: (b, i, 0)),
            pl.BlockSpec((1, 1, d), lambda b, i: (b, 0, 0)),
            pl.BlockSpec((1, d), lambda b, i: (0, 0)),
        ],
        out_specs=pl.BlockSpec((1, tk, d), lambda b, i: (b, i, 0)),
        out_shape=jax.ShapeDtypeStruct(x.shape, F32),
        compiler_params=_cparams(2),
        name="combine",
    )(y, slot, x, gate, final_g)


def kernel(x, c, ctx, c_ctx, ada_w, ada_b, norm1_g, norm2_g, lru_w_in, lru_conv_w, lru_conv_b,
           lru_gate_w, lru_gate_b, lru_lambda, lru_w_out, attn_w_qkv, attn_sink, attn_w_o,
           moe_router, moe_w_gate, moe_w_up, moe_w_down, final_g):
    bsz, t, d = x.shape
    tc = ctx.shape[1]
    depth = ada_w.shape[0]
    n_exp = moe_router.shape[-1]
    cap_l = CAPACITY_FACTOR * t // n_exp
    cap_c = CAPACITY_FACTOR * tc // n_exp

    cond_rows = 2 * SUBLANES
    cond = jnp.zeros((cond_rows, d), F32).at[:bsz].set(c).at[bsz].set(c_ctx)
    mod = _ada_call(cond, ada_w, ada_b).reshape(depth, cond_rows, 6, 1, d)
    norm1 = norm1_g.reshape(depth, 1, d)
    norm2 = norm2_g.reshape(depth, 1, d)
    router_p = jnp.pad(moe_router, ((0, 0), (0, 0), (0, LANES - n_exp)))
    final_g2 = final_g.reshape(1, d)
    cos, sin_signed = _rope_tables(t)
    ones_c = jnp.ones((tc, HEAD_DIM), F32)

    for l in range(depth):
        need_ctx = l < depth - 1
        lat = [mod[l, :bsz, i] for i in range(6)]
        con = [jnp.broadcast_to(mod[l, bsz, i], (bsz, 1, d)) for i in range(6)]
        j = l // 2
        if l % 2 == 0:
            s_l, s_c = _lru_call(x, ctx, lat[0], lat[1], con[0], con[1], norm1, l,
                                 lru_w_in[j].astype(BF16), lru_conv_w[j], lru_conv_b[j],
                                 lru_gate_w[j].astype(BF16), lru_gate_b[j], lru_lambda[j])
            w_out = lru_w_out[j].astype(BF16)
        else:
            w_qkv = attn_w_qkv[j].astype(BF16)
            q, k, v = _qkv_call(x, lat[0], lat[1], norm1, l, w_qkv, cos, sin_signed, True)
            qc, kc, vc = _qkv_call(ctx, con[0], con[1], norm1, l, w_qkv, ones_c, ones_c, False)
            s_l, s_c = _attn_call(attn_sink[j], q, k, v, qc, kc, vc, need_ctx)
            w_out = attn_w_o[j].astype(BF16)
        x = _resid_call(s_l, w_out, x, lat[2])
        if need_ctx:
            ctx = _resid_call(s_c, w_out, ctx, con[2])

        rows = bsz * cap_l + (bsz * cap_c if need_ctx else 0)
        xg, gs, slot_l = _route_call(x, lat[3], lat[4], norm2, l, router_p, n_exp, cap_l, rows, 0)
        if need_ctx:
            xg, gs, slot_c = _route_call(ctx, con[3], con[4], norm2, l, router_p, n_exp, cap_c,
                                         rows, bsz * cap_l, prev=(xg, gs))
        y = _ffn_call(xg, gs, moe_w_gate, moe_w_up, moe_w_down, l)
        x = _combine_call(y, slot_l, x, lat[5], final_g2, cap_l, 0, not need_ctx)
        if need_ctx:
            ctx = _combine_call(y, slot_c, ctx, con[5], final_g2, cap_c, bsz * cap_l, False)
    return x
```

```python
import functools
import math

import jax
import jax.numpy as jnp
from jax import lax
from jax.experimental import pallas as pl
from jax.experimental.pallas import tpu as pltpu

F32 = jnp.float32
BF16 = jnp.bfloat16

LANES = 128
SUBLANES = 8
VMEM_LIMIT_BYTES = 58 * 1024 * 1024

NORM_EPS = 1e-6
NEG_INF = -1e30
LRU_C = 8.0
LRU_BLOCK_W = 128
CONV_W = 4
CONV_HALO = 16
HEAD_DIM = 128
N_KV_HEADS = 2
QBLOCK = 128
WINDOW = 128
GRID_W = 64
ROPE_BASE = 10000.0
CAPACITY_FACTOR = 2

ROW_CHUNK = 256
SCAN_UNROLL = 4
MOD_TILE = 1536
SEARCH_BITS = 3
ROUTE_OUT_BYTES = 4 * 1024 * 1024
FFN_ROWS = 512


def _cparams(n_axes):
    return pltpu.CompilerParams(
        dimension_semantics=("arbitrary",) * n_axes, vmem_limit_bytes=VMEM_LIMIT_BYTES)


def _dot(a, b):
    return jnp.dot(a, b, preferred_element_type=F32)


def _dot_nt(a, b):
    return lax.dot_general(a, b, (((1,), (1,)), ((), ())), preferred_element_type=F32)


def _sigmoid(x):
    return 0.5 * jnp.tanh(0.5 * x) + 0.5


def _sqrt_nonneg(z):
    return jnp.where(z > 0.0, z * lax.rsqrt(z), 0.0)


def _norm_mod(x, g, shift, scale):
    ms = jnp.mean(x * x, axis=-1, keepdims=True)
    y = x * lax.rsqrt(ms + NORM_EPS) * g
    return y * (1.0 + scale) + shift


def _ada_kernel(c_ref, w_ref, b_ref, o_ref):
    c = c_ref[...]
    s = (c * _sigmoid(c)).astype(BF16)
    o_ref[0] = _dot(s, w_ref[0].astype(BF16)) + b_ref[0]


def _ada_call(cond, ada_w, ada_b):
    n_layers, d, n_out = ada_w.shape
    rows = cond.shape[0]
    return pl.pallas_call(
        _ada_kernel,
        grid=(n_layers, n_out // MOD_TILE),
        in_specs=[
            pl.BlockSpec((rows, d), lambda l, n: (0, 0)),
            pl.BlockSpec((1, d, MOD_TILE), lambda l, n: (l, 0, n)),
            pl.BlockSpec((1, 1, MOD_TILE), lambda l, n: (l, 0, n)),
        ],
        out_specs=pl.BlockSpec((1, rows, MOD_TILE), lambda l, n: (l, 0, n)),
        out_shape=jax.ShapeDtypeStruct((n_layers, rows, n_out), F32),
        compiler_params=_cparams(2),
        name="ada",
    )(cond, ada_w, ada_b.reshape(n_layers, 1, n_out))


def _resid_kernel(a_ref, w_ref, x_ref, g_ref, o_ref):
    o_ref[0] = x_ref[0] + g_ref[0] * _dot(a_ref[0], w_ref[...])


def _resid_call(act, w, x, gate):
    bsz, t, d = x.shape
    k = act.shape[-1]
    tm = min(t, 512)
    return pl.pallas_call(
        _resid_kernel,
        grid=(bsz, t // tm),
        in_specs=[
            pl.BlockSpec((1, tm, k), lambda b, i: (b, i, 0)),
            pl.BlockSpec((k, d), lambda b, i: (0, 0)),
            pl.BlockSpec((1, tm, d), lambda b, i: (b, i, 0)),
            pl.BlockSpec((1, 1, d), lambda b, i: (b, 0, 0)),
        ],
        out_specs=pl.BlockSpec((1, tm, d), lambda b, i: (b, i, 0)),
        out_shape=jax.ShapeDtypeStruct(x.shape, F32),
        compiler_params=_cparams(2),
        name="resid",
    )(act, w, x, gate)


LRU_CB = 2 * LRU_BLOCK_W


def _log_sigmoid(x):
    return jnp.minimum(x, 0.0) - jnp.log1p(jnp.exp(-jnp.abs(x)))


def _conv_time(u, cw, cb):
    n = u.shape[0]
    left = (CONV_W - 1) // 2
    acc = cb + u * cw[left:left + 1]
    for k in range(CONV_W):
        off = k - left
        if off != 0:
            acc = acc + pltpu.roll(u, (-off) % n, 0) * cw[k:k + 1]
    return acc[0:n - CONV_HALO]


N_SEG = SUBLANES
SEG_SLACK = N_SEG * 2 * SUBLANES


def _segments(n):
    seg = n // N_SEG
    assert n % (N_SEG * SUBLANES) == 0, n
    pad = SUBLANES if (seg // SUBLANES) % 2 == 0 else 2 * SUBLANES
    return seg, seg + pad


def _lru_kernel(xl_ref, xc_ref, shl_ref, scl_ref, shc_ref, scc_ref, ng_ref, wy_ref, wx_ref,
                cw_ref, cb_ref, gw_ref, gb_ref, lam_ref, ol_ref, oc_ref,
                hl_s, hc_s, a_s, b_s, hs_s, ps_s, y_s, o_s):
    j = pl.program_id(1)
    t_lat = xl_ref.shape[1]
    t_ctx = xc_ref.shape[1]
    gbk = LRU_CB // LRU_BLOCK_W

    @pl.when(j == 0)
    def _():
        g = ng_ref[0]
        for x_ref, h_s, sh_ref, sc_ref in ((xl_ref, hl_s, shl_ref, scl_ref),
                                           (xc_ref, hc_s, shc_ref, scc_ref)):
            n = x_ref.shape[1]
            step = min(n, ROW_CHUNK)
            for r0 in range(0, n, step):
                h_s[r0:r0 + step] = _norm_mod(
                    x_ref[0, r0:r0 + step], g, sh_ref[0], sc_ref[0]).astype(BF16)
            h_s[n:n + CONV_HALO] = jnp.zeros((CONV_HALO, h_s.shape[1]), BF16)

    half_rate = (0.5 * LRU_C / math.log(2.0)) * _log_sigmoid(lam_ref[...])

    def branches(h_s, n):
        seg, pitch = _segments(n)

        def put(store, val):
            for s in range(N_SEG):
                store(slice(s * pitch, s * pitch + seg), val[s * seg:(s + 1) * seg])

        y = jax.nn.gelu(_dot(h_s[0:n], wy_ref[...]))
        xb = _conv_time(_dot(h_s[0:n + CONV_HALO], wx_ref[...]), cw_ref[...], cb_ref[...])
        for kb in range(gbk):
            lanes = slice(kb * LRU_BLOCK_W, (kb + 1) * LRU_BLOCK_W)

            def put_y(rows, v, kb=kb):
                y_s[kb, rows] = v

            put(put_y, y[:, lanes])
            xk = xb[:, lanes]
            xk16 = xk.astype(BF16)
            half_x = 0.5 * xk
            for d in range(2):
                th = jnp.tanh(_dot(xk16, gw_ref[d, kb]) + gb_ref[d, kb])
                th_r = th[:, :LRU_BLOCK_W]
                th_i = th[:, LRU_BLOCK_W:]
                a = jnp.exp2(th_r * half_rate[d:d + 1, lanes] + half_rate[d:d + 1, lanes])

                def put_a(rows, v, d=d, kb=kb):
                    a_s[d, kb, rows] = v

                def put_b(rows, v, d=d, kb=kb):
                    b_s[d, kb, rows] = v

                put(put_a, a)
                put(put_b, _sqrt_nonneg(1.0 - a * a) * (th_i * half_x + half_x))

    def scan(n, carry_in, o_ref):
        seg, pitch = _segments(n)
        chains = [(d, kb) for d in range(2) for kb in range(gbk)]

        def advance(jstep, state):
            out = []
            for c, (d, kb) in enumerate(chains):
                jj = (seg - 1 - jstep) if d == 1 else jstep
                h, p = state[2 * c], state[2 * c + 1]
                a = a_s[d, kb, pl.ds(jj, N_SEG, stride=pitch), :]
                b = b_s[d, kb, pl.ds(jj, N_SEG, stride=pitch), :]
                h = a * h + b
                p = p * a
                rows = pl.ds(pl.multiple_of(jj * N_SEG, N_SEG), N_SEG)
                hs_s[d, kb, rows] = h
                ps_s[d, kb, rows] = p
                out += [h, p]
            return tuple(out)

        init = (jnp.zeros((N_SEG, LRU_BLOCK_W), F32), jnp.ones((N_SEG, LRU_BLOCK_W), F32))
        state = lax.fori_loop(0, seg, advance, init * len(chains), unroll=SCAN_UNROLL)

        row = lax.broadcasted_iota(jnp.int32, (N_SEG, LRU_BLOCK_W), 0)
        enter, carry_out = [], []
        for c, (d, kb) in enumerate(chains):
            h_end, p_end = state[2 * c], state[2 * c + 1]
            cur = carry_in[c]
            vec = jnp.zeros((N_SEG, LRU_BLOCK_W), F32)
            for s in (range(N_SEG - 1, -1, -1) if d == 1 else range(N_SEG)):
                vec = jnp.where(row == s, cur, vec)
                cur = h_end[s:s + 1] + p_end[s:s + 1] * cur
            enter.append(vec)
            carry_out.append(cur)

        def finish(jstep, carry):
            rows = pl.ds(pl.multiple_of(jstep * N_SEG, N_SEG), N_SEG)
            strided = pl.ds(jstep, N_SEG, stride=pitch)
            for kb in range(gbk):
                tot = None
                for c, (d, kb2) in enumerate(chains):
                    if kb2 == kb:
                        part = hs_s[d, kb, rows] + ps_s[d, kb, rows] * enter[c]
                        tot = part if tot is None else tot + part
                o_s[kb, strided, :] = tot * y_s[kb, strided, :]
            return carry

        lax.fori_loop(0, seg, finish, 0, unroll=SCAN_UNROLL)
        for kb in range(gbk):
            for s in range(N_SEG):
                o_ref[0, s * seg:(s + 1) * seg, kb * LRU_BLOCK_W:(kb + 1) * LRU_BLOCK_W] = (
                    o_s[kb, s * pitch:s * pitch + seg].astype(BF16))
        return carry_out

    branches(hc_s, t_ctx)
    carries = scan(t_ctx, [jnp.zeros((1, LRU_BLOCK_W), F32)] * (2 * gbk), oc_ref)
    branches(hl_s, t_lat)
    scan(t_lat, carries, ol_ref)


def _lru_call(x, ctx, sh_l, sc_l, sh_c, sc_c, norm_g, l, w_in, conv_w, conv_b, gate_w, gate_b, lam):
    bsz, t, d = x.shape
    tc = ctx.shape[1]
    width = w_in.shape[1] // 2
    nblk = width // LRU_CB
    gbk = LRU_CB // LRU_BLOCK_W
    vec = lambda b, c: (b, 0, 0)
    return pl.pallas_call(
        _lru_kernel,
        grid=(bsz, nblk),
        in_specs=[
            pl.BlockSpec((1, t, d), vec),
            pl.BlockSpec((1, tc, d), vec),
            pl.BlockSpec((1, 1, d), vec),
            pl.BlockSpec((1, 1, d), vec),
            pl.BlockSpec((1, 1, d), vec),
            pl.BlockSpec((1, 1, d), vec),
            pl.BlockSpec((1, 1, d), lambda b, c: (l, 0, 0)),
            pl.BlockSpec((d, LRU_CB), lambda b, c: (0, c)),
            pl.BlockSpec((d, LRU_CB), lambda b, c: (0, nblk + c)),
            pl.BlockSpec((CONV_W, LRU_CB), lambda b, c: (0, c)),
            pl.BlockSpec((1, LRU_CB), lambda b, c: (0, c)),
            pl.BlockSpec((2, gbk, LRU_BLOCK_W, 2 * LRU_BLOCK_W), lambda b, c: (0, c, 0, 0)),
            pl.BlockSpec((2, gbk, 1, 2 * LRU_BLOCK_W), lambda b, c: (0, c, 0, 0)),
            pl.BlockSpec((2, LRU_CB), lambda b, c: (0, c)),
        ],
        out_specs=[
            pl.BlockSpec((1, t, LRU_CB), lambda b, c: (b, 0, c)),
            pl.BlockSpec((1, tc, LRU_CB), lambda b, c: (b, 0, c)),
        ],
        out_shape=[
            jax.ShapeDtypeStruct((bsz, t, width), BF16),
            jax.ShapeDtypeStruct((bsz, tc, width), BF16),
        ],
        scratch_shapes=[
            pltpu.VMEM((t + CONV_HALO, d), BF16),
            pltpu.VMEM((tc + CONV_HALO, d), BF16),
            pltpu.VMEM((2, gbk, t + SEG_SLACK, LRU_BLOCK_W), F32),
            pltpu.VMEM((2, gbk, t + SEG_SLACK, LRU_BLOCK_W), F32),
            pltpu.VMEM((2, gbk, t, LRU_BLOCK_W), F32),
            pltpu.VMEM((2, gbk, t, LRU_BLOCK_W), F32),
            pltpu.VMEM((gbk, t + SEG_SLACK, LRU_BLOCK_W), F32),
            pltpu.VMEM((gbk, t + SEG_SLACK, LRU_BLOCK_W), F32),
        ],
        compiler_params=_cparams(2),
        name="lru",
    )(x, ctx, sh_l, sc_l, sh_c, sc_c, norm_g, w_in, w_in, conv_w, conv_b.reshape(1, width),
      gate_w, gate_b.reshape(2, -1, 1, 2 * LRU_BLOCK_W), lam)


def _rope(x, cos, sin_signed):
    lane = lax.broadcasted_iota(jnp.int32, x.shape, 1)
    quarter = HEAD_DIM // 4
    partner = jnp.where((lane & quarter) == 0,
                        pltpu.roll(x, HEAD_DIM - quarter, 1), pltpu.roll(x, quarter, 1))
    return x * cos + partner * sin_signed


def _qkv_kernel(x_ref, sh_ref, sc_ref, ng_ref, w_ref, cos_ref, sin_ref, q_ref, k_ref, v_ref,
                *, n_heads, rotary):
    h = _norm_mod(x_ref[0], ng_ref[0], sh_ref[0], sc_ref[0]).astype(BF16)
    u = _dot(h, w_ref[...])
    scale = HEAD_DIM ** -0.5
    nq = n_heads * HEAD_DIM
    nk = N_KV_HEADS * HEAD_DIM
    for hd in range(n_heads + N_KV_HEADS):
        c = u[:, hd * HEAD_DIM:(hd + 1) * HEAD_DIM]
        if rotary:
            c = _rope(c, cos_ref[...], sin_ref[...])
        if hd < n_heads:
            q_ref[0, :, hd * HEAD_DIM:(hd + 1) * HEAD_DIM] = (c * scale).astype(BF16)
        else:
            kk = hd - n_heads
            k_ref[0, :, kk * HEAD_DIM:(kk + 1) * HEAD_DIM] = c.astype(BF16)
    v_ref[0] = u[:, nq + nk:].astype(BF16)


def _qkv_call(x, sh, sc, norm_g, l, w_qkv, cos, sin_signed, rotary):
    bsz, t, d = x.shape
    nk = N_KV_HEADS * HEAD_DIM
    nq = w_qkv.shape[1] - 2 * nk
    tm = min(t, 512)
    vec = lambda b, i: (b, 0, 0)
    return pl.pallas_call(
        functools.partial(_qkv_kernel, n_heads=nq // HEAD_DIM, rotary=rotary),
        grid=(bsz, t // tm),
        in_specs=[
            pl.BlockSpec((1, tm, d), lambda b, i: (b, i, 0)),
            pl.BlockSpec((1, 1, d), vec),
            pl.BlockSpec((1, 1, d), vec),
            pl.BlockSpec((1, 1, d), lambda b, i: (l, 0, 0)),
            pl.BlockSpec(w_qkv.shape, lambda b, i: (0, 0)),
            pl.BlockSpec((tm, HEAD_DIM), lambda b, i: (i, 0)),
            pl.BlockSpec((tm, HEAD_DIM), lambda b, i: (i, 0)),
        ],
        out_specs=[
            pl.BlockSpec((1, tm, nq), lambda b, i: (b, i, 0)),
            pl.BlockSpec((1, tm, nk), lambda b, i: (b, i, 0)),
            pl.BlockSpec((1, tm, nk), lambda b, i: (b, i, 0)),
        ],
        out_shape=[
            jax.ShapeDtypeStruct((bsz, t, nq), BF16),
            jax.ShapeDtypeStruct((bsz, t, nk), BF16),
            jax.ShapeDtypeStruct((bsz, t, nk), BF16),
        ],
        compiler_params=_cparams(2),
        name="qkv",
    )(x, sh, sc, norm_g, w_qkv, cos, sin_signed)


def _stack_heads(x, group):
    return jnp.concatenate([x[:, g * HEAD_DIM:(g + 1) * HEAD_DIM] for g in range(group)], axis=0)


def _attn_kernel(sink_ref, q_ref, k_ref, v_ref, kc_ref, vc_ref, *rest, group, need_ctx):
    if need_ctx:
        qc_ref, o_ref, oc_ref = rest
    else:
        (o_ref,) = rest
    kh = pl.program_id(1)
    t = k_ref.shape[1]
    nblk = t // QBLOCK

    def sink_col(rows_per_head):
        row = lax.broadcasted_iota(jnp.int32, (group * rows_per_head, 1), 0)
        col = jnp.zeros((group * rows_per_head, 1), F32)
        for g in range(group):
            col = jnp.where(row >= g * rows_per_head, sink_ref[kh * group + g], col)
        return col

    kc = kc_ref[0]
    vc = vc_ref[0]
    sink_q = sink_col(QBLOCK)

    def softmax_pv(scores_values, sink):
        m = sink
        for sc, _ in scores_values:
            m = jnp.maximum(m, jnp.max(sc, axis=1, keepdims=True))
        den = jnp.exp(sink - m)
        acc = None
        for sc, val in scores_values:
            p = jnp.exp(sc - m)
            den = den + jnp.sum(p, axis=1, keepdims=True)
            pv = _dot(p.astype(BF16), val)
            acc = pv if acc is None else acc + pv
        return acc / den

    def attend(r0, has_prev, has_next):
        lo = r0 - QBLOCK if has_prev else r0
        if not isinstance(lo, int):
            lo = pl.multiple_of(lo, QBLOCK)
        width = QBLOCK * (1 + has_prev + has_next)
        qs = _stack_heads(q_ref[0, pl.ds(r0, QBLOCK), :], group)
        s = _dot_nt(qs, k_ref[0, pl.ds(lo, width), :])
        qi = lax.broadcasted_iota(jnp.int32, (group * QBLOCK, QBLOCK), 0) & (QBLOCK - 1)
        kj = lax.broadcasted_iota(jnp.int32, (group * QBLOCK, QBLOCK), 1)
        pieces = []
        if has_prev:
            pieces.append(jnp.where(QBLOCK + qi - kj <= WINDOW, s[:, :QBLOCK], NEG_INF))
        c0 = QBLOCK * has_prev
        pieces.append(s[:, c0:c0 + QBLOCK])
        if has_next:
            pieces.append(jnp.where(QBLOCK + kj - qi <= WINDOW, s[:, c0 + QBLOCK:], NEG_INF))
        s = jnp.concatenate(pieces, axis=1)
        o = softmax_pv([(s, v_ref[0, pl.ds(lo, width), :]), (_dot_nt(qs, kc), vc)], sink_q)
        for g in range(group):
            o_ref[0, pl.ds(r0, QBLOCK), g * HEAD_DIM:(g + 1) * HEAD_DIM] = (
                o[g * QBLOCK:(g + 1) * QBLOCK].astype(BF16))

    attend(0, False, nblk > 1)
    if nblk > 1:
        attend((nblk - 1) * QBLOCK, True, False)

    def body(n, carry):
        attend(pl.multiple_of(n * QBLOCK, QBLOCK), True, True)
        return carry

    lax.fori_loop(1, nblk - 1, body, 0, unroll=2)

    if need_ctx:
        tc = qc_ref.shape[1]
        o = softmax_pv([(_dot_nt(_stack_heads(qc_ref[0], group), kc), vc)], sink_col(tc))
        for g in range(group):
            oc_ref[0, :, g * HEAD_DIM:(g + 1) * HEAD_DIM] = o[g * tc:(g + 1) * tc].astype(BF16)


def _attn_call(sink, q, k, v, qc, kc, vc, need_ctx):
    bsz, t, nq = q.shape
    tc = kc.shape[1]
    group = nq // HEAD_DIM // N_KV_HEADS
    gw = group * HEAD_DIM
    in_specs = [
        pl.BlockSpec(memory_space=pltpu.SMEM),
        pl.BlockSpec((1, t, gw), lambda b, h: (b, 0, h)),
        pl.BlockSpec((1, t, HEAD_DIM), lambda b, h: (b, 0, h)),
        pl.BlockSpec((1, t, HEAD_DIM), lambda b, h: (b, 0, h)),
        pl.BlockSpec((1, tc, HEAD_DIM), lambda b, h: (b, 0, h)),
        pl.BlockSpec((1, tc, HEAD_DIM), lambda b, h: (b, 0, h)),
    ]
    out_specs = [pl.BlockSpec((1, t, gw), lambda b, h: (b, 0, h))]
    out_shape = [jax.ShapeDtypeStruct((bsz, t, nq), BF16)]
    args = [sink, q, k, v, kc, vc]
    if need_ctx:
        in_specs.append(pl.BlockSpec((1, tc, gw), lambda b, h: (b, 0, h)))
        out_specs.append(pl.BlockSpec((1, tc, gw), lambda b, h: (b, 0, h)))
        out_shape.append(jax.ShapeDtypeStruct((bsz, tc, nq), BF16))
        args.append(qc)
    outs = pl.pallas_call(
        functools.partial(_attn_kernel, group=group, need_ctx=need_ctx),
        grid=(bsz, N_KV_HEADS),
        in_specs=in_specs,
        out_specs=out_specs,
        out_shape=out_shape,
        compiler_params=_cparams(2),
        name="attn",
    )(*args)
    return (outs[0], outs[1]) if need_ctx else (outs[0], None)


def _rope_tables(t):
    half = HEAD_DIM // 2
    freqs = ROPE_BASE ** (-jnp.arange(0, half, 2, dtype=F32) / half)
    pos = jnp.arange(t)
    rows = (pos // GRID_W).astype(F32)[:, None] * freqs
    cols = (pos % GRID_W).astype(F32)[:, None] * freqs
    cos = jnp.concatenate([jnp.cos(rows), jnp.cos(rows), jnp.cos(cols), jnp.cos(cols)], axis=1)
    sin = jnp.concatenate([-jnp.sin(rows), jnp.sin(rows), -jnp.sin(cols), jnp.sin(cols)], axis=1)
    return cos, sin


def _count(mask):
    return jnp.sum(jnp.where(mask, 1.0, 0.0), axis=1, keepdims=True)


def _route_kernel(x_ref, sh_ref, sc_ref, ng_ref, rw_ref, xg_ref, gs_ref, slot_ref,
                  h_s, aff_s, slotp_s, slot_s, gate_s, *, cap):
    e = pl.program_id(1)
    t = x_ref.shape[1]
    n_exp = slot_s.shape[0]
    chunk = min(t, ROW_CHUNK)

    @pl.when(e == 0)
    def _():
        rw = rw_ref[0]
        rw_hi = rw.astype(BF16)
        rw_lo = (rw - rw_hi.astype(F32)).astype(BF16)
        lane = lax.broadcasted_iota(jnp.int32, (chunk, LANES), 1)
        for r0 in range(0, t, chunk):
            h = _norm_mod(x_ref[0, r0:r0 + chunk], ng_ref[0], sh_ref[0], sc_ref[0])
            h_hi = h.astype(BF16)
            h_lo = (h - h_hi.astype(F32)).astype(BF16)
            h_s[r0:r0 + chunk] = h_hi
            logits = _dot(h_hi, rw_hi) + (_dot(h_lo, rw_hi) + _dot(h_hi, rw_lo))
            logits = jnp.where(lane < n_exp, logits, NEG_INF)
            ex = jnp.exp(logits - jnp.max(logits, axis=1, keepdims=True))
            aff = ex / jnp.sum(ex, axis=1, keepdims=True)
            aff_s[:, r0:r0 + chunk] = aff.T
        aff_t = aff_s[0:n_exp, :]

        def as_f32(word):
            return lax.bitcast_convert_type(word, F32)

        def search(n_digits, accept):
            def body(i, w):
                shift = SEARCH_BITS * (n_digits - 1 - i)
                digit = jnp.zeros_like(w)
                for c in range(1, 1 << SEARCH_BITS):
                    ok = accept(w | jnp.left_shift(jnp.int32(c), shift))
                    digit = digit + jnp.where(ok, 1, 0)
                return w | jnp.left_shift(digit, shift)

            return lax.fori_loop(0, n_digits, body, jnp.zeros((n_exp, 1), jnp.int32))

        kth = search(30 // SEARCH_BITS, lambda w: _count(aff_t >= as_f32(w)) >= cap)
        above = aff_t >= as_f32(kth + 1)
        tie = (aff_t >= as_f32(kth)) & jnp.logical_not(above)
        need = cap - _count(above)
        idx = lax.broadcasted_iota(jnp.int32, aff_t.shape, 1)
        idx_digits = -(-max(1, (t - 1).bit_length()) // SEARCH_BITS)
        last = search(idx_digits, lambda w: _count(tie & (idx < w)) < need)
        sel = jnp.where(above | (tie & (idx <= last)), 1.0, 0.0)

        upper = jnp.where(lax.broadcasted_iota(jnp.int32, (chunk, chunk), 0)
                          < lax.broadcasted_iota(jnp.int32, (chunk, chunk), 1), 1.0, 0.0).astype(BF16)
        running = jnp.zeros((n_exp, 1), F32)
        slotp_s[...] = jnp.full(slotp_s.shape, -1.0, F32)
        for r0 in range(0, t, chunk):
            sel_c = sel[:, r0:r0 + chunk]
            pos = _dot(sel_c.astype(BF16), upper) + running
            running = running + jnp.sum(sel_c, axis=1, keepdims=True)
            slotp_s[0:n_exp, r0:r0 + chunk] = jnp.where(sel_c > 0.5, pos, -1.0)
        gate_t = sel * aff_t
        for ee in range(n_exp):
            slot_s[ee] = slotp_s[ee:ee + 1, :]
            gate_s[ee] = gate_t[ee:ee + 1, :]
        for r0 in range(0, t, chunk):
            slot_ref[0, r0:r0 + chunk] = slotp_s[:, r0:r0 + chunk].T

    eps = xg_ref.shape[0]
    slot_id = lax.broadcasted_iota(jnp.int32, (cap, t), 0).astype(F32)
    for i in range(eps):
        onehot = slot_id == slot_s[e * eps + i]
        xg_ref[i] = _dot(jnp.where(onehot, 1.0, 0.0).astype(BF16), h_s[...]).astype(BF16)
        g = jnp.sum(jnp.where(onehot, gate_s[e * eps + i], 0.0), axis=1, keepdims=True)
        gs_ref[i] = jnp.broadcast_to(g, (cap, LANES))


def _route_call(x, sh, sc, norm_g, l, router_p, n_exp, cap):
    bsz, t, d = x.shape
    eps = max(1, min(n_exp, ROUTE_OUT_BYTES // (cap * d * 2)))
    vec = lambda b, e: (b, 0, 0)
    in_specs = [
        pl.BlockSpec((1, t, d), vec),
        pl.BlockSpec((1, 1, d), vec),
        pl.BlockSpec((1, 1, d), vec),
        pl.BlockSpec((1, 1, d), lambda b, e: (l, 0, 0)),
        pl.BlockSpec((1, d, LANES), lambda b, e: (l, 0, 0)),
    ]
    return pl.pallas_call(
        functools.partial(_route_kernel, cap=cap),
        grid=(bsz, n_exp // eps),
        in_specs=in_specs,
        out_specs=[
            pl.BlockSpec((eps, cap, d), lambda b, e: (e, b, 0)),
            pl.BlockSpec((eps, cap, LANES), lambda b, e: (e, b, 0)),
            pl.BlockSpec((1, t, LANES), vec),
        ],
        out_shape=[
            jax.ShapeDtypeStruct((n_exp, bsz * cap, d), BF16),
            jax.ShapeDtypeStruct((n_exp, bsz * cap, LANES), F32),
            jax.ShapeDtypeStruct((bsz, t, LANES), F32),
        ],
        scratch_shapes=[
            pltpu.VMEM((t, d), BF16),
            pltpu.VMEM((LANES, t), F32),
            pltpu.VMEM((LANES, t), F32),
            pltpu.VMEM((n_exp, 1, t), F32),
            pltpu.VMEM((n_exp, 1, t), F32),
        ],
        compiler_params=_cparams(2),
        name="route",
    )(x, sh, sc, norm_g, router_p)


def _ffn_kernel(*refs, lat_steps, has_ctx):
    if has_ctx:
        xl_ref, gl_ref, xc_ref, gc_ref, wg_ref, wu_ref, wd_ref, yl_ref, yc_ref, w_s = refs
    else:
        xl_ref, gl_ref, wg_ref, wu_ref, wd_ref, yl_ref, w_s = refs
    r = pl.program_id(1)

    @pl.when(r == 0)
    def _():
        w_s[0] = wg_ref[0, 0].astype(BF16)
        w_s[1] = wu_ref[0, 0].astype(BF16)
        w_s[2] = wd_ref[0, 0].astype(BF16)

    def expert(x_ref, gs_ref, y_ref):
        x = x_ref[0]
        a = _dot(x, w_s[0])
        u = _dot(x, w_s[1])
        hmid = (a * _sigmoid(a) * u).astype(BF16)
        y_ref[0] = (_dot(hmid, w_s[2]) * gs_ref[0][:, 0:1]).astype(BF16)

    if has_ctx:
        pl.when(r < lat_steps)(lambda: expert(xl_ref, gl_ref, yl_ref))
        pl.when(r == lat_steps)(lambda: expert(xc_ref, gc_ref, yc_ref))
    else:
        expert(xl_ref, gl_ref, yl_ref)


def _ffn_call(xg_l, gs_l, xg_c, gs_c, w_gate, w_up, w_down, l):
    n_exp, rows, d = xg_l.shape
    f = w_gate.shape[-1]
    has_ctx = xg_c is not None
    tm = min(rows, FFN_ROWS)
    lat_steps = rows // tm
    lat_map = lambda e, r: (e, jnp.minimum(r, lat_steps - 1), 0)
    ctx_map = lambda e, r: (e, 0, 0)
    w_map = lambda e, r: (l, e, 0, 0)
    in_specs = [pl.BlockSpec((1, tm, d), lat_map), pl.BlockSpec((1, tm, LANES), lat_map)]
    out_specs = [pl.BlockSpec((1, tm, d), lat_map)]
    out_shape = [jax.ShapeDtypeStruct(xg_l.shape, BF16)]
    args = [xg_l, gs_l]
    if has_ctx:
        rows_c = xg_c.shape[1]
        in_specs += [pl.BlockSpec((1, rows_c, d), ctx_map), pl.BlockSpec((1, rows_c, LANES), ctx_map)]
        out_specs.append(pl.BlockSpec((1, rows_c, d), ctx_map))
        out_shape.append(jax.ShapeDtypeStruct(xg_c.shape, BF16))
        args += [xg_c, gs_c]
    in_specs += [pl.BlockSpec((1, 1, d, f), w_map), pl.BlockSpec((1, 1, d, f), w_map),
                 pl.BlockSpec((1, 1, f, d), w_map)]
    outs = pl.pallas_call(
        functools.partial(_ffn_kernel, lat_steps=lat_steps, has_ctx=has_ctx),
        grid=(n_exp, lat_steps + has_ctx),
        in_specs=in_specs,
        out_specs=out_specs,
        out_shape=out_shape,
        scratch_shapes=[pltpu.VMEM((3, d, f), BF16)],
        compiler_params=_cparams(2),
        name="ffn",
    )(*args, w_gate, w_up, w_down)
    return (outs[0], outs[1]) if has_ctx else (outs[0], None)


def _combine_kernel(y_ref, slot_ref, x_ref, g_ref, fg_ref, o_ref, *, final_norm):
    n_exp, cap, _ = y_ref.shape
    tk = x_ref.shape[1]
    slots = slot_ref[0]
    lane = lax.broadcasted_iota(jnp.int32, (tk, cap), 1).astype(F32)
    acc = jnp.zeros(x_ref.shape[1:], F32)
    for e in range(n_exp):
        onehot = jnp.where(lane == slots[:, e:e + 1], 1.0, 0.0).astype(BF16)
        acc = acc + _dot(onehot, y_ref[e])
    out = x_ref[0] + g_ref[0] * acc
    if final_norm:
        ms = jnp.mean(out * out, axis=-1, keepdims=True)
        out = out * lax.rsqrt(ms + NORM_EPS) * fg_ref[...]
    o_ref[0] = out


def _combine_call(y, slot, x, gate, final_g, final_norm):
    bsz, t, d = x.shape
    n_exp = y.shape[0]
    cap = y.shape[1] // bsz
    tk = min(t, 512)
    return pl.pallas_call(
        functools.partial(_combine_kernel, final_norm=final_norm),
        grid=(bsz, t // tk),
        in_specs=[
            pl.BlockSpec((n_exp, cap, d), lambda b, i: (0, b, 0)),
            pl.BlockSpec((1, tk, LANES), lambda b, i: (b, i, 0)),
            pl.BlockSpec((1, tk, d), lambda b, i: (b, i, 0)),
            pl.BlockSpec((1, 1, d), lambda b, i: (b, 0, 0)),
            pl.BlockSpec((1, d), lambda b, i: (0, 0)),
        ],
        out_specs=pl.BlockSpec((1, tk, d), lambda b, i: (b, i, 0)),
        out_shape=jax.ShapeDtypeStruct(x.shape, F32),
        compiler_params=_cparams(2),
        name="combine",
    )(y, slot, x, gate, final_g)


def kernel(x, c, ctx, c_ctx, ada_w, ada_b, norm1_g, norm2_g, lru_w_in, lru_conv_w, lru_conv_b,
           lru_gate_w, lru_gate_b, lru_lambda, lru_w_out, attn_w_qkv, attn_sink, attn_w_o,
           moe_router, moe_w_gate, moe_w_up, moe_w_down, final_g):
    bsz, t, d = x.shape
    tc = ctx.shape[1]
    depth = ada_w.shape[0]
    n_exp = moe_router.shape[-1]
    cap_l = CAPACITY_FACTOR * t // n_exp
    cap_c = CAPACITY_FACTOR * tc // n_exp

    cond_rows = 2 * SUBLANES
    cond = jnp.zeros((cond_rows, d), F32).at[:bsz].set(c).at[bsz].set(c_ctx)
    mod = _ada_call(cond, ada_w, ada_b).reshape(depth, cond_rows, 6, 1, d)
    norm1 = norm1_g.reshape(depth, 1, d)
    norm2 = norm2_g.reshape(depth, 1, d)
    router_p = jnp.pad(moe_router, ((0, 0), (0, 0), (0, LANES - n_exp)))
    final_g2 = final_g.reshape(1, d)
    cos, sin_signed = _rope_tables(t)
    ones_c = jnp.ones((tc, HEAD_DIM), F32)

    for l in range(depth):
        need_ctx = l < depth - 1
        lat = [mod[l, :bsz, i] for i in range(6)]
        con = [jnp.broadcast_to(mod[l, bsz, i], (bsz, 1, d)) for i in range(6)]
        j = l // 2
        if l % 2 == 0:
            s_l, s_c = _lru_call(x, ctx, lat[0], lat[1], con[0], con[1], norm1, l,
                                 lru_w_in[j].astype(BF16), lru_conv_w[j], lru_conv_b[j],
                                 (0.5 * lru_gate_w[j]).astype(BF16), 0.5 * lru_gate_b[j],
                                 lru_lambda[j])
            w_out = lru_w_out[j].astype(BF16)
        else:
            w_qkv = attn_w_qkv[j].astype(BF16)
            q, k, v = _qkv_call(x, lat[0], lat[1], norm1, l, w_qkv, cos, sin_signed, True)
            qc, kc, vc = _qkv_call(ctx, con[0], con[1], norm1, l, w_qkv, ones_c, ones_c, False)
            s_l, s_c = _attn_call(attn_sink[j], q, k, v, qc, kc, vc, need_ctx)
            w_out = attn_w_o[j].astype(BF16)
        x = _resid_call(s_l, w_out, x, lat[2])
        if need_ctx:
            ctx = _resid_call(s_c, w_out, ctx, con[2])

        xg_l, gs_l, slot_l = _route_call(x, lat[3], lat[4], norm2, l, router_p, n_exp, cap_l)
        xg_c = gs_c = None
        if need_ctx:
            xg_c, gs_c, slot_c = _route_call(ctx, con[3], con[4], norm2, l, router_p, n_exp, cap_c)
        y_l, y_c = _ffn_call(xg_l, gs_l, xg_c, gs_c, moe_w_gate, moe_w_up, moe_w_down, l)
        x = _combine_call(y_l, slot_l, x, lat[5], final_g2, not need_ctx)
        if need_ctx:
            ctx = _combine_call(y_c, slot_c, ctx, con[5], final_g2, False)
    return x
```

```python
import functools
import math

import jax
import jax.numpy as jnp
from jax import lax
from jax.experimental import pallas as pl
from jax.experimental.pallas import tpu as pltpu

F32 = jnp.float32
BF16 = jnp.bfloat16

LANES = 128
SUBLANES = 8
VMEM_LIMIT_BYTES = 58 * 1024 * 1024

NORM_EPS = 1e-6
NEG_INF = -1e30
LRU_C = 8.0
LRU_BLOCK_W = 128
CONV_W = 4
CONV_HALO = 16
HEAD_DIM = 128
N_KV_HEADS = 2
QBLOCK = 128
WINDOW = 128
GRID_W = 64
ROPE_BASE = 10000.0
CAPACITY_FACTOR = 2

ROW_CHUNK = 256
SCAN_UNROLL = 4
MOD_TILE = 1536
SEARCH_BITS = 3
ROUTE_OUT_BYTES = 5 * 1024 * 1024
FFN_ROWS = 1152
FFN_COLS = 256


def _cparams(n_axes):
    return pltpu.CompilerParams(
        dimension_semantics=("arbitrary",) * n_axes, vmem_limit_bytes=VMEM_LIMIT_BYTES)


def _dot(a, b):
    return jnp.dot(a, b, preferred_element_type=F32)


def _dot_nt(a, b):
    return lax.dot_general(a, b, (((1,), (1,)), ((), ())), preferred_element_type=F32)


def _sigmoid(x):
    return 0.5 * jnp.tanh(0.5 * x) + 0.5


def _sqrt_nonneg(z):
    return jnp.where(z > 0.0, z * lax.rsqrt(z), 0.0)


def _norm_mod(x, g, shift, scale):
    ms = jnp.mean(x * x, axis=-1, keepdims=True)
    y = x * lax.rsqrt(ms + NORM_EPS) * g
    return y * (1.0 + scale) + shift


def _ada_kernel(c_ref, w_ref, b_ref, o_ref):
    c = c_ref[...]
    s = (c * _sigmoid(c)).astype(BF16)
    o_ref[0] = _dot(s, w_ref[0].astype(BF16)) + b_ref[0]


def _ada_call(cond, ada_w, ada_b):
    n_layers, d, n_out = ada_w.shape
    rows = cond.shape[0]
    return pl.pallas_call(
        _ada_kernel,
        grid=(n_layers, n_out // MOD_TILE),
        in_specs=[
            pl.BlockSpec((rows, d), lambda l, n: (0, 0)),
            pl.BlockSpec((1, d, MOD_TILE), lambda l, n: (l, 0, n)),
            pl.BlockSpec((1, 1, MOD_TILE), lambda l, n: (l, 0, n)),
        ],
        out_specs=pl.BlockSpec((1, rows, MOD_TILE), lambda l, n: (l, 0, n)),
        out_shape=jax.ShapeDtypeStruct((n_layers, rows, n_out), F32),
        compiler_params=_cparams(2),
        name="ada",
    )(cond, ada_w, ada_b.reshape(n_layers, 1, n_out))


def _resid_kernel(a_ref, w_ref, x_ref, g_ref, o_ref):
    o_ref[0] = x_ref[0] + g_ref[0] * _dot(a_ref[0], w_ref[...])


def _resid_call(act, w, x, gate):
    bsz, t, d = x.shape
    k = act.shape[-1]
    tm = min(t, 512)
    return pl.pallas_call(
        _resid_kernel,
        grid=(bsz, t // tm),
        in_specs=[
            pl.BlockSpec((1, tm, k), lambda b, i: (b, i, 0)),
            pl.BlockSpec((k, d), lambda b, i: (0, 0)),
            pl.BlockSpec((1, tm, d), lambda b, i: (b, i, 0)),
            pl.BlockSpec((1, 1, d), lambda b, i: (b, 0, 0)),
        ],
        out_specs=pl.BlockSpec((1, tm, d), lambda b, i: (b, i, 0)),
        out_shape=jax.ShapeDtypeStruct(x.shape, F32),
        compiler_params=_cparams(2),
        name="resid",
    )(act, w, x, gate)


LRU_CB = 2 * LRU_BLOCK_W


def _log_sigmoid(x):
    return jnp.minimum(x, 0.0) - jnp.log1p(jnp.exp(-jnp.abs(x)))


def _conv_time(u, cw, cb):
    n = u.shape[0]
    left = (CONV_W - 1) // 2
    acc = cb + u * cw[left:left + 1]
    for k in range(CONV_W):
        off = k - left
        if off != 0:
            acc = acc + pltpu.roll(u, (-off) % n, 0) * cw[k:k + 1]
    return acc[0:n - CONV_HALO]


N_SEG = SUBLANES
SEG_SLACK = N_SEG * 2 * SUBLANES


def _segments(n):
    seg = n // N_SEG
    assert n % (N_SEG * SUBLANES) == 0, n
    pad = SUBLANES if (seg // SUBLANES) % 2 == 0 else 2 * SUBLANES
    return seg, seg + pad


def _lru_kernel(xl_ref, xc_ref, shl_ref, scl_ref, shc_ref, scc_ref, ng_ref, wy_ref, wx_ref,
                cw_ref, cb_ref, gw_ref, gb_ref, lam_ref, ol_ref, oc_ref,
                hl_s, hc_s, a_s, b_s, hs_s, ps_s, y_s, o_s):
    j = pl.program_id(1)
    t_lat = xl_ref.shape[1]
    t_ctx = xc_ref.shape[1]
    gbk = LRU_CB // LRU_BLOCK_W

    @pl.when(j == 0)
    def _():
        g = ng_ref[0]
        for x_ref, h_s, sh_ref, sc_ref in ((xl_ref, hl_s, shl_ref, scl_ref),
                                           (xc_ref, hc_s, shc_ref, scc_ref)):
            n = x_ref.shape[1]
            step = min(n, ROW_CHUNK)
            for r0 in range(0, n, step):
                h_s[r0:r0 + step] = _norm_mod(
                    x_ref[0, r0:r0 + step], g, sh_ref[0], sc_ref[0]).astype(BF16)
            h_s[n:n + CONV_HALO] = jnp.zeros((CONV_HALO, h_s.shape[1]), BF16)

    half_rate = (0.5 * LRU_C / math.log(2.0)) * _log_sigmoid(lam_ref[...])

    def branches(h_s, n):
        seg, pitch = _segments(n)

        def put(store, val):
            for s in range(N_SEG):
                store(slice(s * pitch, s * pitch + seg), val[s * seg:(s + 1) * seg])

        y = jax.nn.gelu(_dot(h_s[0:n], wy_ref[...]))
        xb = _conv_time(_dot(h_s[0:n + CONV_HALO], wx_ref[...]), cw_ref[...], cb_ref[...])
        for kb in range(gbk):
            lanes = slice(kb * LRU_BLOCK_W, (kb + 1) * LRU_BLOCK_W)

            def put_y(rows, v, kb=kb):
                y_s[kb, rows] = v

            put(put_y, y[:, lanes])
            xk = xb[:, lanes]
            xk16 = xk.astype(BF16)
            half_x = 0.5 * xk
            for d in range(2):
                th = jnp.tanh(_dot(xk16, gw_ref[d, kb]) + gb_ref[d, kb])
                th_r = th[:, :LRU_BLOCK_W]
                th_i = th[:, LRU_BLOCK_W:]
                a = jnp.exp2(th_r * half_rate[d:d + 1, lanes] + half_rate[d:d + 1, lanes])

                def put_a(rows, v, d=d, kb=kb):
                    a_s[d, kb, rows] = v

                def put_b(rows, v, d=d, kb=kb):
                    b_s[d, kb, rows] = v

                put(put_a, a)
                put(put_b, _sqrt_nonneg(1.0 - a * a) * (th_i * half_x + half_x))

    def scan(n, carry_in, o_ref):
        seg, pitch = _segments(n)
        chains = [(d, kb) for d in range(2) for kb in range(gbk)]

        def advance(jstep, state):
            out = []
            for c, (d, kb) in enumerate(chains):
                jj = (seg - 1 - jstep) if d == 1 else jstep
                h, p = state[2 * c], state[2 * c + 1]
                a = a_s[d, kb, pl.ds(jj, N_SEG, stride=pitch), :]
                b = b_s[d, kb, pl.ds(jj, N_SEG, stride=pitch), :]
                h = a * h + b
                p = p * a
                rows = pl.ds(pl.multiple_of(jj * N_SEG, N_SEG), N_SEG)
                hs_s[d, kb, rows] = h
                ps_s[d, kb, rows] = p
                out += [h, p]
            return tuple(out)

        init = (jnp.zeros((N_SEG, LRU_BLOCK_W), F32), jnp.ones((N_SEG, LRU_BLOCK_W), F32))
        state = lax.fori_loop(0, seg, advance, init * len(chains), unroll=SCAN_UNROLL)

        row = lax.broadcasted_iota(jnp.int32, (N_SEG, LRU_BLOCK_W), 0)
        enter, carry_out = [], []
        for c, (d, kb) in enumerate(chains):
            h_end, p_end = state[2 * c], state[2 * c + 1]
            cur = carry_in[c]
            vec = jnp.zeros((N_SEG, LRU_BLOCK_W), F32)
            for s in (range(N_SEG - 1, -1, -1) if d == 1 else range(N_SEG)):
                vec = jnp.where(row == s, cur, vec)
                cur = h_end[s:s + 1] + p_end[s:s + 1] * cur
            enter.append(vec)
            carry_out.append(cur)

        def finish(jstep, carry):
            rows = pl.ds(pl.multiple_of(jstep * N_SEG, N_SEG), N_SEG)
            strided = pl.ds(jstep, N_SEG, stride=pitch)
            for kb in range(gbk):
                tot = None
                for c, (d, kb2) in enumerate(chains):
                    if kb2 == kb:
                        part = hs_s[d, kb, rows] + ps_s[d, kb, rows] * enter[c]
                        tot = part if tot is None else tot + part
                o_s[kb, strided, :] = tot * y_s[kb, strided, :]
            return carry

        lax.fori_loop(0, seg, finish, 0, unroll=SCAN_UNROLL)
        for kb in range(gbk):
            for s in range(N_SEG):
                o_ref[0, s * seg:(s + 1) * seg, kb * LRU_BLOCK_W:(kb + 1) * LRU_BLOCK_W] = (
                    o_s[kb, s * pitch:s * pitch + seg].astype(BF16))
        return carry_out

    branches(hc_s, t_ctx)
    carries = scan(t_ctx, [jnp.zeros((1, LRU_BLOCK_W), F32)] * (2 * gbk), oc_ref)
    branches(hl_s, t_lat)
    scan(t_lat, carries, ol_ref)


def _lru_call(x, ctx, sh_l, sc_l, sh_c, sc_c, norm_g, l, w_in, conv_w, conv_b, gate_w, gate_b, lam):
    bsz, t, d = x.shape
    tc = ctx.shape[1]
    width = w_in.shape[1] // 2
    nblk = width // LRU_CB
    gbk = LRU_CB // LRU_BLOCK_W
    vec = lambda b, c: (b, 0, 0)
    return pl.pallas_call(
        _lru_kernel,
        grid=(bsz, nblk),
        in_specs=[
            pl.BlockSpec((1, t, d), vec),
            pl.BlockSpec((1, tc, d), vec),
            pl.BlockSpec((1, 1, d), vec),
            pl.BlockSpec((1, 1, d), vec),
            pl.BlockSpec((1, 1, d), vec),
            pl.BlockSpec((1, 1, d), vec),
            pl.BlockSpec((1, 1, d), lambda b, c: (l, 0, 0)),
            pl.BlockSpec((d, LRU_CB), lambda b, c: (0, c)),
            pl.BlockSpec((d, LRU_CB), lambda b, c: (0, nblk + c)),
            pl.BlockSpec((CONV_W, LRU_CB), lambda b, c: (0, c)),
            pl.BlockSpec((1, LRU_CB), lambda b, c: (0, c)),
            pl.BlockSpec((2, gbk, LRU_BLOCK_W, 2 * LRU_BLOCK_W), lambda b, c: (0, c, 0, 0)),
            pl.BlockSpec((2, gbk, 1, 2 * LRU_BLOCK_W), lambda b, c: (0, c, 0, 0)),
            pl.BlockSpec((2, LRU_CB), lambda b, c: (0, c)),
        ],
        out_specs=[
            pl.BlockSpec((1, t, LRU_CB), lambda b, c: (b, 0, c)),
            pl.BlockSpec((1, tc, LRU_CB), lambda b, c: (b, 0, c)),
        ],
        out_shape=[
            jax.ShapeDtypeStruct((bsz, t, width), BF16),
            jax.ShapeDtypeStruct((bsz, tc, width), BF16),
        ],
        scratch_shapes=[
            pltpu.VMEM((t + CONV_HALO, d), BF16),
            pltpu.VMEM((tc + CONV_HALO, d), BF16),
            pltpu.VMEM((2, gbk, t + SEG_SLACK, LRU_BLOCK_W), F32),
            pltpu.VMEM((2, gbk, t + SEG_SLACK, LRU_BLOCK_W), F32),
            pltpu.VMEM((2, gbk, t, LRU_BLOCK_W), F32),
            pltpu.VMEM((2, gbk, t, LRU_BLOCK_W), F32),
            pltpu.VMEM((gbk, t + SEG_SLACK, LRU_BLOCK_W), F32),
            pltpu.VMEM((gbk, t + SEG_SLACK, LRU_BLOCK_W), F32),
        ],
        compiler_params=_cparams(2),
        name="lru",
    )(x, ctx, sh_l, sc_l, sh_c, sc_c, norm_g, w_in, w_in, conv_w, conv_b.reshape(1, width),
      gate_w, gate_b.reshape(2, -1, 1, 2 * LRU_BLOCK_W), lam)


def _rope(x, cos, sin_signed):
    lane = lax.broadcasted_iota(jnp.int32, x.shape, 1)
    quarter = HEAD_DIM // 4
    partner = jnp.where((lane & quarter) == 0,
                        pltpu.roll(x, HEAD_DIM - quarter, 1), pltpu.roll(x, quarter, 1))
    return x * cos + partner * sin_signed


def _qkv_kernel(x_ref, sh_ref, sc_ref, ng_ref, w_ref, cos_ref, sin_ref, q_ref, k_ref, v_ref,
                *, n_heads, rotary):
    h = _norm_mod(x_ref[0], ng_ref[0], sh_ref[0], sc_ref[0]).astype(BF16)
    u = _dot(h, w_ref[...])
    scale = HEAD_DIM ** -0.5
    nq = n_heads * HEAD_DIM
    nk = N_KV_HEADS * HEAD_DIM
    for hd in range(n_heads + N_KV_HEADS):
        c = u[:, hd * HEAD_DIM:(hd + 1) * HEAD_DIM]
        if rotary:
            c = _rope(c, cos_ref[...], sin_ref[...])
        if hd < n_heads:
            q_ref[0, :, hd * HEAD_DIM:(hd + 1) * HEAD_DIM] = (c * scale).astype(BF16)
        else:
            kk = hd - n_heads
            k_ref[0, :, kk * HEAD_DIM:(kk + 1) * HEAD_DIM] = c.astype(BF16)
    v_ref[0] = u[:, nq + nk:].astype(BF16)


def _qkv_call(x, sh, sc, norm_g, l, w_qkv, cos, sin_signed, rotary):
    bsz, t, d = x.shape
    nk = N_KV_HEADS * HEAD_DIM
    nq = w_qkv.shape[1] - 2 * nk
    tm = min(t, 512)
    vec = lambda b, i: (b, 0, 0)
    return pl.pallas_call(
        functools.partial(_qkv_kernel, n_heads=nq // HEAD_DIM, rotary=rotary),
        grid=(bsz, t // tm),
        in_specs=[
            pl.BlockSpec((1, tm, d), lambda b, i: (b, i, 0)),
            pl.BlockSpec((1, 1, d), vec),
            pl.BlockSpec((1, 1, d), vec),
            pl.BlockSpec((1, 1, d), lambda b, i: (l, 0, 0)),
            pl.BlockSpec(w_qkv.shape, lambda b, i: (0, 0)),
            pl.BlockSpec((tm, HEAD_DIM), lambda b, i: (i, 0)),
            pl.BlockSpec((tm, HEAD_DIM), lambda b, i: (i, 0)),
        ],
        out_specs=[
            pl.BlockSpec((1, tm, nq), lambda b, i: (b, i, 0)),
            pl.BlockSpec((1, tm, nk), lambda b, i: (b, i, 0)),
            pl.BlockSpec((1, tm, nk), lambda b, i: (b, i, 0)),
        ],
        out_shape=[
            jax.ShapeDtypeStruct((bsz, t, nq), BF16),
            jax.ShapeDtypeStruct((bsz, t, nk), BF16),
            jax.ShapeDtypeStruct((bsz, t, nk), BF16),
        ],
        compiler_params=_cparams(2),
        name="qkv",
    )(x, sh, sc, norm_g, w_qkv, cos, sin_signed)


def _stack_heads(x, group):
    return jnp.concatenate([x[:, g * HEAD_DIM:(g + 1) * HEAD_DIM] for g in range(group)], axis=0)


def _attn_kernel(sink_ref, q_ref, k_ref, v_ref, kc_ref, vc_ref, *rest, group, need_ctx):
    if need_ctx:
        qc_ref, o_ref, oc_ref = rest
    else:
        (o_ref,) = rest
    kh = pl.program_id(1)
    t = k_ref.shape[1]
    nblk = t // QBLOCK

    def sink_col(rows_per_head):
        row = lax.broadcasted_iota(jnp.int32, (group * rows_per_head, 1), 0)
        col = jnp.zeros((group * rows_per_head, 1), F32)
        for g in range(group):
            col = jnp.where(row >= g * rows_per_head, sink_ref[kh * group + g], col)
        return col

    kc = kc_ref[0]
    vc = vc_ref[0]
    sink_q = sink_col(QBLOCK)

    def softmax_pv(scores_values, sink):
        m = sink
        for sc, _ in scores_values:
            m = jnp.maximum(m, jnp.max(sc, axis=1, keepdims=True))
        den = jnp.exp(sink - m)
        acc = None
        for sc, val in scores_values:
            p = jnp.exp(sc - m)
            den = den + jnp.sum(p, axis=1, keepdims=True)
            pv = _dot(p.astype(BF16), val)
            acc = pv if acc is None else acc + pv
        return acc / den

    def attend(r0, has_prev, has_next):
        lo = r0 - QBLOCK if has_prev else r0
        if not isinstance(lo, int):
            lo = pl.multiple_of(lo, QBLOCK)
        width = QBLOCK * (1 + has_prev + has_next)
        qs = _stack_heads(q_ref[0, pl.ds(r0, QBLOCK), :], group)
        s = _dot_nt(qs, k_ref[0, pl.ds(lo, width), :])
        qi = lax.broadcasted_iota(jnp.int32, (group * QBLOCK, QBLOCK), 0) & (QBLOCK - 1)
        kj = lax.broadcasted_iota(jnp.int32, (group * QBLOCK, QBLOCK), 1)
        pieces = []
        if has_prev:
            pieces.append(jnp.where(QBLOCK + qi - kj <= WINDOW, s[:, :QBLOCK], NEG_INF))
        c0 = QBLOCK * has_prev
        pieces.append(s[:, c0:c0 + QBLOCK])
        if has_next:
            pieces.append(jnp.where(QBLOCK + kj - qi <= WINDOW, s[:, c0 + QBLOCK:], NEG_INF))
        s = jnp.concatenate(pieces, axis=1)
        o = softmax_pv([(s, v_ref[0, pl.ds(lo, width), :]), (_dot_nt(qs, kc), vc)], sink_q)
        for g in range(group):
            o_ref[0, pl.ds(r0, QBLOCK), g * HEAD_DIM:(g + 1) * HEAD_DIM] = (
                o[g * QBLOCK:(g + 1) * QBLOCK].astype(BF16))

    attend(0, False, nblk > 1)
    if nblk > 1:
        attend((nblk - 1) * QBLOCK, True, False)

    def body(n, carry):
        attend(pl.multiple_of(n * QBLOCK, QBLOCK), True, True)
        return carry

    lax.fori_loop(1, nblk - 1, body, 0, unroll=2)

    if need_ctx:
        tc = qc_ref.shape[1]
        o = softmax_pv([(_dot_nt(_stack_heads(qc_ref[0], group), kc), vc)], sink_col(tc))
        for g in range(group):
            oc_ref[0, :, g * HEAD_DIM:(g + 1) * HEAD_DIM] = o[g * tc:(g + 1) * tc].astype(BF16)


def _attn_call(sink, q, k, v, qc, kc, vc, need_ctx):
    bsz, t, nq = q.shape
    tc = kc.shape[1]
    group = nq // HEAD_DIM // N_KV_HEADS
    gw = group * HEAD_DIM
    in_specs = [
        pl.BlockSpec(memory_space=pltpu.SMEM),
        pl.BlockSpec((1, t, gw), lambda b, h: (b, 0, h)),
        pl.BlockSpec((1, t, HEAD_DIM), lambda b, h: (b, 0, h)),
        pl.BlockSpec((1, t, HEAD_DIM), lambda b, h: (b, 0, h)),
        pl.BlockSpec((1, tc, HEAD_DIM), lambda b, h: (b, 0, h)),
        pl.BlockSpec((1, tc, HEAD_DIM), lambda b, h: (b, 0, h)),
    ]
    out_specs = [pl.BlockSpec((1, t, gw), lambda b, h: (b, 0, h))]
    out_shape = [jax.ShapeDtypeStruct((bsz, t, nq), BF16)]
    args = [sink, q, k, v, kc, vc]
    if need_ctx:
        in_specs.append(pl.BlockSpec((1, tc, gw), lambda b, h: (b, 0, h)))
        out_specs.append(pl.BlockSpec((1, tc, gw), lambda b, h: (b, 0, h)))
        out_shape.append(jax.ShapeDtypeStruct((bsz, tc, nq), BF16))
        args.append(qc)
    outs = pl.pallas_call(
        functools.partial(_attn_kernel, group=group, need_ctx=need_ctx),
        grid=(bsz, N_KV_HEADS),
        in_specs=in_specs,
        out_specs=out_specs,
        out_shape=out_shape,
        compiler_params=_cparams(2),
        name="attn",
    )(*args)
    return (outs[0], outs[1]) if need_ctx else (outs[0], None)


def _rope_tables(t):
    half = HEAD_DIM // 2
    freqs = ROPE_BASE ** (-jnp.arange(0, half, 2, dtype=F32) / half)
    pos = jnp.arange(t)
    rows = (pos // GRID_W).astype(F32)[:, None] * freqs
    cols = (pos % GRID_W).astype(F32)[:, None] * freqs
    cos = jnp.concatenate([jnp.cos(rows), jnp.cos(rows), jnp.cos(cols), jnp.cos(cols)], axis=1)
    sin = jnp.concatenate([-jnp.sin(rows), jnp.sin(rows), -jnp.sin(cols), jnp.sin(cols)], axis=1)
    return cos, sin


def _count(mask):
    return jnp.sum(jnp.where(mask, 1.0, 0.0), axis=1, keepdims=True)


def _route_plan(x_ref, sh_ref, sc_ref, ng_ref, rw_ref, slot_ref, h_s, aff_s, slotp_s, slot_s, gate_s,
                cap):
    t = x_ref.shape[1]
    n_exp = slot_s.shape[0]
    chunk = min(t, ROW_CHUNK)
    rw = rw_ref[0]
    rw_hi = rw.astype(BF16)
    rw_lo = (rw - rw_hi.astype(F32)).astype(BF16)
    lane = lax.broadcasted_iota(jnp.int32, (chunk, LANES), 1)
    for r0 in range(0, t, chunk):
        h = _norm_mod(x_ref[0, r0:r0 + chunk], ng_ref[0], sh_ref[0], sc_ref[0])
        h_hi = h.astype(BF16)
        h_lo = (h - h_hi.astype(F32)).astype(BF16)
        h_s[r0:r0 + chunk] = h_hi
        logits = _dot(h_hi, rw_hi) + (_dot(h_lo, rw_hi) + _dot(h_hi, rw_lo))
        logits = jnp.where(lane < n_exp, logits, NEG_INF)
        ex = jnp.exp(logits - jnp.max(logits, axis=1, keepdims=True))
        aff = ex / jnp.sum(ex, axis=1, keepdims=True)
        aff_s[:, r0:r0 + chunk] = aff.T
    aff_t = aff_s[0:n_exp, :]

    def as_f32(word):
        return lax.bitcast_convert_type(word, F32)

    def search(n_digits, accept):
        def body(i, w):
            shift = SEARCH_BITS * (n_digits - 1 - i)
            digit = jnp.zeros_like(w)
            for c in range(1, 1 << SEARCH_BITS):
                ok = accept(w | jnp.left_shift(jnp.int32(c), shift))
                digit = digit + jnp.where(ok, 1, 0)
            return w | jnp.left_shift(digit, shift)

        return lax.fori_loop(0, n_digits, body, jnp.zeros((n_exp, 1), jnp.int32))

    kth = search(30 // SEARCH_BITS, lambda w: _count(aff_t >= as_f32(w)) >= cap)
    above = aff_t >= as_f32(kth + 1)
    tie = (aff_t >= as_f32(kth)) & jnp.logical_not(above)
    need = cap - _count(above)
    idx = lax.broadcasted_iota(jnp.int32, aff_t.shape, 1)
    idx_digits = -(-max(1, (t - 1).bit_length()) // SEARCH_BITS)
    last = search(idx_digits, lambda w: _count(tie & (idx < w)) < need)
    sel = jnp.where(above | (tie & (idx <= last)), 1.0, 0.0)

    upper = jnp.where(lax.broadcasted_iota(jnp.int32, (chunk, chunk), 0)
                      < lax.broadcasted_iota(jnp.int32, (chunk, chunk), 1), 1.0, 0.0).astype(BF16)
    running = jnp.zeros((n_exp, 1), F32)
    slotp_s[...] = jnp.full(slotp_s.shape, -1.0, F32)
    for r0 in range(0, t, chunk):
        sel_c = sel[:, r0:r0 + chunk]
        pos = _dot(sel_c.astype(BF16), upper) + running
        running = running + jnp.sum(sel_c, axis=1, keepdims=True)
        slotp_s[0:n_exp, r0:r0 + chunk] = jnp.where(sel_c > 0.5, pos, -1.0)
    gate_t = sel * aff_t
    for ee in range(n_exp):
        slot_s[ee] = slotp_s[ee:ee + 1, :]
        gate_s[ee] = gate_t[ee:ee + 1, :]
    for r0 in range(0, t, chunk):
        slot_ref[0, r0:r0 + chunk] = slotp_s[:, r0:r0 + chunk].T


def _route_gather(e0, xg_ref, gs_ref, row0, h_s, slot_s, gate_s, cap):
    t = h_s.shape[0]
    slot_id = lax.broadcasted_iota(jnp.int32, (cap, t), 0).astype(F32)
    for i in range(xg_ref.shape[0]):
        onehot = slot_id == slot_s[e0 + i]
        xg_ref[i, 0, row0:row0 + cap] = _dot(
            jnp.where(onehot, 1.0, 0.0).astype(BF16), h_s[...]).astype(BF16)
        g = jnp.sum(jnp.where(onehot, gate_s[e0 + i], 0.0), axis=1, keepdims=True)
        gs_ref[i, 0, row0:row0 + cap] = jnp.broadcast_to(g, (cap, LANES))


def _route_kernel(*refs, caps):
    ns = len(caps)
    ins = [refs[3 * s:3 * s + 3] for s in range(ns)]
    ng_ref, rw_ref, xg_ref, gs_ref = refs[3 * ns:3 * ns + 4]
    slot_refs = refs[3 * ns + 4:4 * ns + 4]
    scratch = [refs[4 * ns + 4 + 5 * s:4 * ns + 9 + 5 * s] for s in range(ns)]
    e = pl.program_id(1)

    @pl.when(e == 0)
    def _():
        for s in range(ns):
            _route_plan(*ins[s], ng_ref, rw_ref, slot_refs[s], *scratch[s], caps[s])

    row0 = 0
    for s in range(ns):
        h_s, _, _, slot_s, gate_s = scratch[s]
        _route_gather(e * xg_ref.shape[0], xg_ref, gs_ref, row0, h_s, slot_s, gate_s, caps[s])
        row0 += caps[s]


def _route_call(streams, norm_g, l, router_p, n_exp):
    bsz, _, d = streams[0][0].shape
    caps = tuple(st[3] for st in streams)
    rows = sum(caps)
    fit = max(1, ROUTE_OUT_BYTES // (rows * d * 2))
    eps = max(k for k in range(1, n_exp + 1) if n_exp % k == 0 and k <= fit)
    vec = lambda b, e: (b, 0, 0)
    in_specs, args, slot_specs, slot_shapes, scratch = [], [], [], [], []
    for x, sh, sc, _ in streams:
        t = x.shape[1]
        in_specs += [pl.BlockSpec((1, t, d), vec), pl.BlockSpec((1, 1, d), vec),
                     pl.BlockSpec((1, 1, d), vec)]
        args += [x, sh, sc]
        slot_specs.append(pl.BlockSpec((1, t, LANES), vec))
        slot_shapes.append(jax.ShapeDtypeStruct((bsz, t, LANES), F32))
        scratch += [
            pltpu.VMEM((t, d), BF16),
            pltpu.VMEM((LANES, t), F32),
            pltpu.VMEM((LANES, t), F32),
            pltpu.VMEM((n_exp, 1, t), F32),
            pltpu.VMEM((n_exp, 1, t), F32),
        ]
    in_specs += [pl.BlockSpec((1, 1, d), lambda b, e: (l, 0, 0)),
                 pl.BlockSpec((1, d, LANES), lambda b, e: (l, 0, 0))]
    outs = pl.pallas_call(
        functools.partial(_route_kernel, caps=caps),
        grid=(bsz, n_exp // eps),
        in_specs=in_specs,
        out_specs=[
            pl.BlockSpec((eps, 1, rows, d), lambda b, e: (e, b, 0, 0)),
            pl.BlockSpec((eps, 1, rows, LANES), lambda b, e: (e, b, 0, 0)),
        ] + slot_specs,
        out_shape=[
            jax.ShapeDtypeStruct((n_exp, bsz, rows, d), BF16),
            jax.ShapeDtypeStruct((n_exp, bsz, rows, LANES), F32),
        ] + slot_shapes,
        scratch_shapes=scratch,
        compiler_params=_cparams(2),
        name="route",
    )(*args, norm_g, router_p)
    return outs[0], outs[1], outs[2:]


def _ffn_kernel(x_ref, gs_ref, wg_ref, wu_ref, wd_ref, y_ref, w_s):
    @pl.when(pl.program_id(1) == 0)
    def _():
        w_s[0] = wg_ref[0, 0].astype(BF16)
        w_s[1] = wu_ref[0, 0].astype(BF16)
        w_s[2] = wd_ref[0, 0].astype(BF16)

    x = x_ref[0]
    f = w_s.shape[2]
    acc = None
    for c0 in range(0, f, FFN_COLS):
        a = _dot(x, w_s[0, :, c0:c0 + FFN_COLS])
        u = _dot(x, w_s[1, :, c0:c0 + FFN_COLS])
        hmid = (a * _sigmoid(a) * u).astype(BF16)
        part = _dot(hmid, w_s[2, c0:c0 + FFN_COLS, :])
        acc = part if acc is None else acc + part
    y_ref[0] = (acc * gs_ref[0][:, 0:1]).astype(BF16)


def _ffn_call(xg, gs, w_gate, w_up, w_down, l):
    n_exp, rows, d = xg.shape
    f = w_gate.shape[-1]
    tm = rows // max(1, -(-rows // FFN_ROWS))
    assert rows % tm == 0 and tm % (2 * SUBLANES) == 0, (rows, tm)
    w_map = lambda e, r: (l, e, 0, 0)
    return pl.pallas_call(
        _ffn_kernel,
        grid=(n_exp, rows // tm),
        in_specs=[
            pl.BlockSpec((1, tm, d), lambda e, r: (e, r, 0)),
            pl.BlockSpec((1, tm, LANES), lambda e, r: (e, r, 0)),
            pl.BlockSpec((1, 1, d, f), w_map),
            pl.BlockSpec((1, 1, d, f), w_map),
            pl.BlockSpec((1, 1, f, d), w_map),
        ],
        out_specs=pl.BlockSpec((1, tm, d), lambda e, r: (e, r, 0)),
        out_shape=jax.ShapeDtypeStruct((n_exp, rows, d), BF16),
        scratch_shapes=[pltpu.VMEM((3, d, f), BF16)],
        compiler_params=_cparams(2),
        name="ffn",
    )(xg, gs, w_gate, w_up, w_down)


def _combine_kernel(y_ref, slot_ref, x_ref, g_ref, fg_ref, o_ref, *, final_norm):
    n_exp, _, cap, _ = y_ref.shape
    tk = x_ref.shape[1]
    slots = slot_ref[0]
    lane = lax.broadcasted_iota(jnp.int32, (tk, cap), 1).astype(F32)
    acc = jnp.zeros(x_ref.shape[1:], F32)
    for e in range(n_exp):
        onehot = jnp.where(lane == slots[:, e:e + 1], 1.0, 0.0).astype(BF16)
        acc = acc + _dot(onehot, y_ref[e, 0])
    out = x_ref[0] + g_ref[0] * acc
    if final_norm:
        ms = jnp.mean(out * out, axis=-1, keepdims=True)
        out = out * lax.rsqrt(ms + NORM_EPS) * fg_ref[...]
    o_ref[0] = out


def _combine_call(y, slot, x, gate, final_g, cap, row0, final_norm):
    bsz, t, d = x.shape
    n_exp = y.shape[0]
    tk = min(t, 512)
    assert row0 % cap == 0
    return pl.pallas_call(
        functools.partial(_combine_kernel, final_norm=final_norm),
        grid=(bsz, t // tk),
        in_specs=[
            pl.BlockSpec((n_exp, 1, cap, d), lambda b, i: (0, b, row0 // cap, 0)),
            pl.BlockSpec((1, tk, LANES), lambda b, i: (b, i, 0)),
            pl.BlockSpec((1, tk, d), lambda b, i: (b, i, 0)),
            pl.BlockSpec((1, 1, d), lambda b, i: (b, 0, 0)),
            pl.BlockSpec((1, d), lambda b, i: (0, 0)),
        ],
        out_specs=pl.BlockSpec((1, tk, d), lambda b, i: (b, i, 0)),
        out_shape=jax.ShapeDtypeStruct(x.shape, F32),
        compiler_params=_cparams(2),
        name="combine",
    )(y, slot, x, gate, final_g)


def kernel(x, c, ctx, c_ctx, ada_w, ada_b, norm1_g, norm2_g, lru_w_in, lru_conv_w, lru_conv_b,
           lru_gate_w, lru_gate_b, lru_lambda, lru_w_out, attn_w_qkv, attn_sink, attn_w_o,
           moe_router, moe_w_gate, moe_w_up, moe_w_down, final_g):
    bsz, t, d = x.shape
    tc = ctx.shape[1]
    depth = ada_w.shape[0]
    n_exp = moe_router.shape[-1]
    cap_l = CAPACITY_FACTOR * t // n_exp
    cap_c = CAPACITY_FACTOR * tc // n_exp

    cond_rows = 2 * SUBLANES
    cond = jnp.zeros((cond_rows, d), F32).at[:bsz].set(c).at[bsz].set(c_ctx)
    mod = _ada_call(cond, ada_w, ada_b).reshape(depth, cond_rows, 6, 1, d)
    norm1 = norm1_g.reshape(depth, 1, d)
    norm2 = norm2_g.reshape(depth, 1, d)
    router_p = jnp.pad(moe_router, ((0, 0), (0, 0), (0, LANES - n_exp)))
    final_g2 = final_g.reshape(1, d)
    cos, sin_signed = _rope_tables(t)
    ones_c = jnp.ones((tc, HEAD_DIM), F32)

    for l in range(depth):
        need_ctx = l < depth - 1
        lat = [mod[l, :bsz, i] for i in range(6)]
        con = [jnp.broadcast_to(mod[l, bsz, i], (bsz, 1, d)) for i in range(6)]
        j = l // 2
        if l % 2 == 0:
            s_l, s_c = _lru_call(x, ctx, lat[0], lat[1], con[0], con[1], norm1, l,
                                 lru_w_in[j].astype(BF16), lru_conv_w[j], lru_conv_b[j],
                                 (0.5 * lru_gate_w[j]).astype(BF16), 0.5 * lru_gate_b[j],
                                 lru_lambda[j])
            w_out = lru_w_out[j].astype(BF16)
        else:
            w_qkv = attn_w_qkv[j].astype(BF16)
            q, k, v = _qkv_call(x, lat[0], lat[1], norm1, l, w_qkv, cos, sin_signed, True)
            qc, kc, vc = _qkv_call(ctx, con[0], con[1], norm1, l, w_qkv, ones_c, ones_c, False)
            s_l, s_c = _attn_call(attn_sink[j], q, k, v, qc, kc, vc, need_ctx)
            w_out = attn_w_o[j].astype(BF16)
        x = _resid_call(s_l, w_out, x, lat[2])
        if need_ctx:
            ctx = _resid_call(s_c, w_out, ctx, con[2])

        streams = [(x, lat[3], lat[4], cap_l)]
        if need_ctx:
            streams.append((ctx, con[3], con[4], cap_c))
        xg, gs, slots = _route_call(streams, norm2, l, router_p, n_exp)
        rows = xg.shape[2]
        y = _ffn_call(xg.reshape(n_exp, bsz * rows, d), gs.reshape(n_exp, bsz * rows, LANES),
                      moe_w_gate, moe_w_up, moe_w_down, l).reshape(xg.shape)
        x = _combine_call(y, slots[0], x, lat[5], final_g2, cap_l, 0, not need_ctx)
        if need_ctx:
            ctx = _combine_call(y, slots[1], ctx, con[5], final_g2, cap_c, cap_l, False)
    return x
```

```python
import functools
import math

import jax
import jax.numpy as jnp
from jax import lax
from jax.experimental import pallas as pl
from jax.experimental.pallas import tpu as pltpu

F32 = jnp.float32
BF16 = jnp.bfloat16

LANES = 128
SUBLANES = 8
VMEM_LIMIT_BYTES = 58 * 1024 * 1024

NORM_EPS = 1e-6
NEG_INF = -1e30
LRU_C = 8.0
LRU_BLOCK_W = 128
CONV_W = 4
CONV_HALO = 16
LRU_ROWS = 512
HEAD_DIM = 128
N_KV_HEADS = 2
QBLOCK = 128
WINDOW = 128
GRID_W = 64
ROPE_BASE = 10000.0
CAPACITY_FACTOR = 2

ROW_CHUNK = 256
SCAN_UNROLL = 4
MOD_TILE = 1536
SEARCH_BITS = 3
ROUTE_OUT_BYTES = 5 * 1024 * 1024
FFN_ROWS = 1152
FFN_COLS = 256
COMBINE_ROWS = 1024


def _cparams(n_axes):
    return pltpu.CompilerParams(
        dimension_semantics=("arbitrary",) * n_axes, vmem_limit_bytes=VMEM_LIMIT_BYTES)


def _dot(a, b):
    return jnp.dot(a, b, preferred_element_type=F32)


def _dot_nt(a, b):
    return lax.dot_general(a, b, (((1,), (1,)), ((), ())), preferred_element_type=F32)


def _sigmoid(x):
    return 0.5 * jnp.tanh(0.5 * x) + 0.5


def _sqrt_nonneg(z):
    return jnp.where(z > 0.0, z * lax.rsqrt(z), 0.0)


def _norm_mod(x, g, shift, scale):
    ms = jnp.mean(x * x, axis=-1, keepdims=True)
    y = x * lax.rsqrt(ms + NORM_EPS) * g
    return y * (1.0 + scale) + shift


def _ada_kernel(c_ref, w_ref, b_ref, o_ref):
    c = c_ref[...]
    s = (c * _sigmoid(c)).astype(BF16)
    o_ref[0] = _dot(s, w_ref[0].astype(BF16)) + b_ref[0]


def _ada_call(cond, ada_w, ada_b):
    n_layers, d, n_out = ada_w.shape
    rows = cond.shape[0]
    return pl.pallas_call(
        _ada_kernel,
        grid=(n_layers, n_out // MOD_TILE),
        in_specs=[
            pl.BlockSpec((rows, d), lambda l, n: (0, 0)),
            pl.BlockSpec((1, d, MOD_TILE), lambda l, n: (l, 0, n)),
            pl.BlockSpec((1, 1, MOD_TILE), lambda l, n: (l, 0, n)),
        ],
        out_specs=pl.BlockSpec((1, rows, MOD_TILE), lambda l, n: (l, 0, n)),
        out_shape=jax.ShapeDtypeStruct((n_layers, rows, n_out), F32),
        compiler_params=_cparams(2),
        name="ada",
    )(cond, ada_w, ada_b.reshape(n_layers, 1, n_out))


def _resid_kernel(a_ref, w_ref, x_ref, g_ref, o_ref):
    o_ref[0] = x_ref[0] + g_ref[0] * _dot(a_ref[0], w_ref[...])


def _resid_call(act, w, x, gate):
    bsz, t, d = x.shape
    k = act.shape[-1]
    tm = min(t, 512)
    return pl.pallas_call(
        _resid_kernel,
        grid=(bsz, t // tm),
        in_specs=[
            pl.BlockSpec((1, tm, k), lambda b, i: (b, i, 0)),
            pl.BlockSpec((k, d), lambda b, i: (0, 0)),
            pl.BlockSpec((1, tm, d), lambda b, i: (b, i, 0)),
            pl.BlockSpec((1, 1, d), lambda b, i: (b, 0, 0)),
        ],
        out_specs=pl.BlockSpec((1, tm, d), lambda b, i: (b, i, 0)),
        out_shape=jax.ShapeDtypeStruct(x.shape, F32),
        compiler_params=_cparams(2),
        name="resid",
    )(act, w, x, gate)


LRU_CB = 2 * LRU_BLOCK_W


def _log_sigmoid(x):
    return jnp.minimum(x, 0.0) - jnp.log1p(jnp.exp(-jnp.abs(x)))


def _conv_time(u, cw, cb):
    n = u.shape[0]
    left = (CONV_W - 1) // 2
    acc = cb + u * cw[left:left + 1]
    for k in range(CONV_W):
        off = k - left
        if off != 0:
            acc = acc + pltpu.roll(u, (-off) % n, 0) * cw[k:k + 1]
    return acc[CONV_HALO:n - CONV_HALO]


N_SEG = SUBLANES
SEG_SLACK = N_SEG * 2 * SUBLANES


def _segments(n):
    seg = n // N_SEG
    assert n % (N_SEG * SUBLANES) == 0, n
    pad = SUBLANES if (seg // SUBLANES) % 2 == 0 else 2 * SUBLANES
    return seg, seg + pad


def _lru_kernel(xl_ref, xc_ref, shl_ref, scl_ref, shc_ref, scc_ref, ng_ref, wy_ref, wx_ref,
                cw_ref, cb_ref, gw_ref, gb_ref, lam_ref, ol_ref, oc_ref,
                hl_s, hc_s, a_s, b_s, hs_s, ps_s, y_s, o_s):
    j = pl.program_id(1)
    t_lat = xl_ref.shape[1]
    t_ctx = xc_ref.shape[1]
    gbk = LRU_CB // LRU_BLOCK_W

    @pl.when(j == 0)
    def _():
        g = ng_ref[0]
        for x_ref, h_s, sh_ref, sc_ref in ((xl_ref, hl_s, shl_ref, scl_ref),
                                           (xc_ref, hc_s, shc_ref, scc_ref)):
            n = x_ref.shape[1]
            step = min(n, ROW_CHUNK)
            halo = jnp.zeros((CONV_HALO, h_s.shape[1]), BF16)
            h_s[0:CONV_HALO] = halo
            for r0 in range(0, n, step):
                h_s[CONV_HALO + r0:CONV_HALO + r0 + step] = _norm_mod(
                    x_ref[0, r0:r0 + step], g, sh_ref[0], sc_ref[0]).astype(BF16)
            h_s[CONV_HALO + n:2 * CONV_HALO + n] = halo

    half_rate = (0.5 * LRU_C / math.log(2.0)) * _log_sigmoid(lam_ref[...])

    def branches(h_s, n):
        seg, pitch = _segments(n)
        rows = min(n, LRU_ROWS)
        assert n % rows == 0 and rows % seg == 0, (n, rows, seg)
        for r0 in range(0, n, rows):
            branch_rows(h_s, r0, rows, seg, pitch)

    def branch_rows(h_s, r0, rows, seg, pitch):
        def put(store, val):
            for s in range(r0 // seg, (r0 + rows) // seg):
                store(slice(s * pitch, s * pitch + seg), val[s * seg - r0:(s + 1) * seg - r0])

        y = jax.nn.gelu(_dot(h_s[CONV_HALO + r0:CONV_HALO + r0 + rows], wy_ref[...]))
        xb = _conv_time(_dot(h_s[r0:r0 + rows + 2 * CONV_HALO], wx_ref[...]),
                        cw_ref[...], cb_ref[...])
        for kb in range(gbk):
            lanes = slice(kb * LRU_BLOCK_W, (kb + 1) * LRU_BLOCK_W)

            def put_y(rows, v, kb=kb):
                y_s[kb, rows] = v

            put(put_y, y[:, lanes])
            xk = xb[:, lanes]
            xk16 = xk.astype(BF16)
            half_x = 0.5 * xk
            for d in range(2):
                th = jnp.tanh(_dot(xk16, gw_ref[d, kb]) + gb_ref[d, kb])
                th_r = th[:, :LRU_BLOCK_W]
                th_i = th[:, LRU_BLOCK_W:]
                a = jnp.exp2(th_r * half_rate[d:d + 1, lanes] + half_rate[d:d + 1, lanes])

                def put_a(rows, v, d=d, kb=kb):
                    a_s[d, kb, rows] = v

                def put_b(rows, v, d=d, kb=kb):
                    b_s[d, kb, rows] = v

                put(put_a, a)
                put(put_b, _sqrt_nonneg(1.0 - a * a) * (th_i * half_x + half_x))

    def scan(n, carry_in, o_ref):
        seg, pitch = _segments(n)
        chains = [(d, kb) for d in range(2) for kb in range(gbk)]

        def advance(jstep, state):
            out = []
            for c, (d, kb) in enumerate(chains):
                jj = (seg - 1 - jstep) if d == 1 else jstep
                h, p = state[2 * c], state[2 * c + 1]
                a = a_s[d, kb, pl.ds(jj, N_SEG, stride=pitch), :]
                b = b_s[d, kb, pl.ds(jj, N_SEG, stride=pitch), :]
                h = a * h + b
                p = p * a
                rows = pl.ds(pl.multiple_of(jj * N_SEG, N_SEG), N_SEG)
                hs_s[d, kb, rows] = h
                ps_s[d, kb, rows] = p
                out += [h, p]
            return tuple(out)

        init = (jnp.zeros((N_SEG, LRU_BLOCK_W), F32), jnp.ones((N_SEG, LRU_BLOCK_W), F32))
        state = lax.fori_loop(0, seg, advance, init * len(chains), unroll=SCAN_UNROLL)

        row = lax.broadcasted_iota(jnp.int32, (N_SEG, LRU_BLOCK_W), 0)
        enter, carry_out = [], []
        for c, (d, kb) in enumerate(chains):
            h_end, p_end = state[2 * c], state[2 * c + 1]
            cur = carry_in[c]
            vec = jnp.zeros((N_SEG, LRU_BLOCK_W), F32)
            for s in (range(N_SEG - 1, -1, -1) if d == 1 else range(N_SEG)):
                vec = jnp.where(row == s, cur, vec)
                cur = h_end[s:s + 1] + p_end[s:s + 1] * cur
            enter.append(vec)
            carry_out.append(cur)

        def finish(jstep, carry):
            rows = pl.ds(pl.multiple_of(jstep * N_SEG, N_SEG), N_SEG)
            strided = pl.ds(jstep, N_SEG, stride=pitch)
            for kb in range(gbk):
                tot = None
                for c, (d, kb2) in enumerate(chains):
                    if kb2 == kb:
                        part = hs_s[d, kb, rows] + ps_s[d, kb, rows] * enter[c]
                        tot = part if tot is None else tot + part
                o_s[kb, strided, :] = tot * y_s[kb, strided, :]
            return carry

        lax.fori_loop(0, seg, finish, 0, unroll=SCAN_UNROLL)
        for kb in range(gbk):
            for s in range(N_SEG):
                o_ref[0, s * seg:(s + 1) * seg, kb * LRU_BLOCK_W:(kb + 1) * LRU_BLOCK_W] = (
                    o_s[kb, s * pitch:s * pitch + seg].astype(BF16))
        return carry_out

    branches(hc_s, t_ctx)
    carries = scan(t_ctx, [jnp.zeros((1, LRU_BLOCK_W), F32)] * (2 * gbk), oc_ref)
    branches(hl_s, t_lat)
    scan(t_lat, carries, ol_ref)


def _lru_call(x, ctx, sh_l, sc_l, sh_c, sc_c, norm_g, l, w_in, conv_w, conv_b, gate_w, gate_b, lam):
    bsz, t, d = x.shape
    tc = ctx.shape[1]
    width = w_in.shape[1] // 2
    nblk = width // LRU_CB
    gbk = LRU_CB // LRU_BLOCK_W
    vec = lambda b, c: (b, 0, 0)
    return pl.pallas_call(
        _lru_kernel,
        grid=(bsz, nblk),
        in_specs=[
            pl.BlockSpec((1, t, d), vec),
            pl.BlockSpec((1, tc, d), vec),
            pl.BlockSpec((1, 1, d), vec),
            pl.BlockSpec((1, 1, d), vec),
            pl.BlockSpec((1, 1, d), vec),
            pl.BlockSpec((1, 1, d), vec),
            pl.BlockSpec((1, 1, d), lambda b, c: (l, 0, 0)),
            pl.BlockSpec((d, LRU_CB), lambda b, c: (0, c)),
            pl.BlockSpec((d, LRU_CB), lambda b, c: (0, nblk + c)),
            pl.BlockSpec((CONV_W, LRU_CB), lambda b, c: (0, c)),
            pl.BlockSpec((1, LRU_CB), lambda b, c: (0, c)),
            pl.BlockSpec((2, gbk, LRU_BLOCK_W, 2 * LRU_BLOCK_W), lambda b, c: (0, c, 0, 0)),
            pl.BlockSpec((2, gbk, 1, 2 * LRU_BLOCK_W), lambda b, c: (0, c, 0, 0)),
            pl.BlockSpec((2, LRU_CB), lambda b, c: (0, c)),
        ],
        out_specs=[
            pl.BlockSpec((1, t, LRU_CB), lambda b, c: (b, 0, c)),
            pl.BlockSpec((1, tc, LRU_CB), lambda b, c: (b, 0, c)),
        ],
        out_shape=[
            jax.ShapeDtypeStruct((bsz, t, width), BF16),
            jax.ShapeDtypeStruct((bsz, tc, width), BF16),
        ],
        scratch_shapes=[
            pltpu.VMEM((t + 2 * CONV_HALO, d), BF16),
            pltpu.VMEM((tc + 2 * CONV_HALO, d), BF16),
            pltpu.VMEM((2, gbk, t + SEG_SLACK, LRU_BLOCK_W), F32),
            pltpu.VMEM((2, gbk, t + SEG_SLACK, LRU_BLOCK_W), F32),
            pltpu.VMEM((2, gbk, t, LRU_BLOCK_W), F32),
            pltpu.VMEM((2, gbk, t, LRU_BLOCK_W), F32),
            pltpu.VMEM((gbk, t + SEG_SLACK, LRU_BLOCK_W), F32),
            pltpu.VMEM((gbk, t + SEG_SLACK, LRU_BLOCK_W), F32),
        ],
        compiler_params=_cparams(2),
        name="lru",
    )(x, ctx, sh_l, sc_l, sh_c, sc_c, norm_g, w_in, w_in, conv_w, conv_b.reshape(1, width),
      gate_w, gate_b.reshape(2, -1, 1, 2 * LRU_BLOCK_W), lam)


def _rope(x, cos, sin_signed):
    lane = lax.broadcasted_iota(jnp.int32, x.shape, 1)
    quarter = HEAD_DIM // 4
    partner = jnp.where((lane & quarter) == 0,
                        pltpu.roll(x, HEAD_DIM - quarter, 1), pltpu.roll(x, quarter, 1))
    return x * cos + partner * sin_signed


def _qkv_kernel(x_ref, sh_ref, sc_ref, ng_ref, w_ref, cos_ref, sin_ref, q_ref, k_ref, vt_ref,
                *, n_heads, rotary):
    h = _norm_mod(x_ref[0], ng_ref[0], sh_ref[0], sc_ref[0]).astype(BF16)
    u = _dot(h, w_ref[...])
    scale = HEAD_DIM ** -0.5
    nq = n_heads * HEAD_DIM
    nk = N_KV_HEADS * HEAD_DIM
    for hd in range(n_heads + N_KV_HEADS):
        c = u[:, hd * HEAD_DIM:(hd + 1) * HEAD_DIM]
        if rotary:
            c = _rope(c, cos_ref[...], sin_ref[...])
        if hd < n_heads:
            q_ref[0, :, hd * HEAD_DIM:(hd + 1) * HEAD_DIM] = (c * scale).astype(BF16)
        else:
            kk = hd - n_heads
            k_ref[0, :, kk * HEAD_DIM:(kk + 1) * HEAD_DIM] = c.astype(BF16)
    vt_ref[0] = u[:, nq + nk:].T.astype(BF16)


def _qkv_call(x, sh, sc, norm_g, l, w_qkv, cos, sin_signed, rotary):
    bsz, t, d = x.shape
    nk = N_KV_HEADS * HEAD_DIM
    nq = w_qkv.shape[1] - 2 * nk
    tm = min(t, 512)
    vec = lambda b, i: (b, 0, 0)
    return pl.pallas_call(
        functools.partial(_qkv_kernel, n_heads=nq // HEAD_DIM, rotary=rotary),
        grid=(bsz, t // tm),
        in_specs=[
            pl.BlockSpec((1, tm, d), lambda b, i: (b, i, 0)),
            pl.BlockSpec((1, 1, d), vec),
            pl.BlockSpec((1, 1, d), vec),
            pl.BlockSpec((1, 1, d), lambda b, i: (l, 0, 0)),
            pl.BlockSpec(w_qkv.shape, lambda b, i: (0, 0)),
            pl.BlockSpec((tm, HEAD_DIM), lambda b, i: (i, 0)),
            pl.BlockSpec((tm, HEAD_DIM), lambda b, i: (i, 0)),
        ],
        out_specs=[
            pl.BlockSpec((1, tm, nq), lambda b, i: (b, i, 0)),
            pl.BlockSpec((1, tm, nk), lambda b, i: (b, i, 0)),
            pl.BlockSpec((1, nk, tm), lambda b, i: (b, 0, i)),
        ],
        out_shape=[
            jax.ShapeDtypeStruct((bsz, t, nq), BF16),
            jax.ShapeDtypeStruct((bsz, t, nk), BF16),
            jax.ShapeDtypeStruct((bsz, nk, t), BF16),
        ],
        compiler_params=_cparams(2),
        name="qkv",
    )(x, sh, sc, norm_g, w_qkv, cos, sin_signed)


def _stack_heads(x, group):
    return jnp.concatenate([x[:, g * HEAD_DIM:(g + 1) * HEAD_DIM] for g in range(group)], axis=0)


def _attn_kernel(sink_ref, q_ref, k_ref, vt_ref, kc_ref, vct_ref, *rest, group, need_ctx):
    if need_ctx:
        qc_ref, o_ref, oc_ref = rest
    else:
        (o_ref,) = rest
    kh = pl.program_id(1)
    t = k_ref.shape[1]
    nblk = t // QBLOCK

    def sink_row(cols_per_head):
        col = lax.broadcasted_iota(jnp.int32, (1, group * cols_per_head), 1)
        row = jnp.zeros((1, group * cols_per_head), F32)
        for g in range(group):
            row = jnp.where(col >= g * cols_per_head, sink_ref[kh * group + g], row)
        return row

    kc = kc_ref[0]
    vct = vct_ref[0]
    sink_q = sink_row(QBLOCK)

    def softmax_vp(scores_values, sink):
        m = sink
        for sc, _ in scores_values:
            m = jnp.maximum(m, jnp.max(sc, axis=0, keepdims=True))
        den = jnp.exp(sink - m)
        acc = None
        for sc, val in scores_values:
            p = jnp.exp(sc - m)
            den = den + jnp.sum(p, axis=0, keepdims=True)
            vp = _dot(val, p.astype(BF16))
            acc = vp if acc is None else acc + vp
        return (acc / den).T

    def attend(r0, has_prev, has_next):
        lo = r0 - QBLOCK if has_prev else r0
        if not isinstance(lo, int):
            lo = pl.multiple_of(lo, QBLOCK)
        width = QBLOCK * (1 + has_prev + has_next)
        qs = _stack_heads(q_ref[0, pl.ds(r0, QBLOCK), :], group)
        s = _dot_nt(k_ref[0, pl.ds(lo, width), :], qs)
        kj = lax.broadcasted_iota(jnp.int32, (QBLOCK, group * QBLOCK), 0)
        qi = lax.broadcasted_iota(jnp.int32, (QBLOCK, group * QBLOCK), 1) & (QBLOCK - 1)
        pieces = []
        if has_prev:
            pieces.append(jnp.where(QBLOCK + qi - kj <= WINDOW, s[:QBLOCK], NEG_INF))
        c0 = QBLOCK * has_prev
        pieces.append(s[c0:c0 + QBLOCK])
        if has_next:
            pieces.append(jnp.where(QBLOCK + kj - qi <= WINDOW, s[c0 + QBLOCK:], NEG_INF))
        s = jnp.concatenate(pieces, axis=0)
        o = softmax_vp([(s, vt_ref[0, :, pl.ds(lo, width)]), (_dot_nt(kc, qs), vct)], sink_q)
        for g in range(group):
            o_ref[0, pl.ds(r0, QBLOCK), g * HEAD_DIM:(g + 1) * HEAD_DIM] = (
                o[g * QBLOCK:(g + 1) * QBLOCK].astype(BF16))

    attend(0, False, nblk > 1)
    if nblk > 1:
        attend((nblk - 1) * QBLOCK, True, False)

    def body(n, carry):
        attend(pl.multiple_of(n * QBLOCK, QBLOCK), True, True)
        return carry

    lax.fori_loop(1, nblk - 1, body, 0, unroll=7)

    if need_ctx:
        tc = qc_ref.shape[1]
        o = softmax_vp([(_dot_nt(kc, _stack_heads(qc_ref[0], group)), vct)], sink_row(tc))
        for g in range(group):
            oc_ref[0, :, g * HEAD_DIM:(g + 1) * HEAD_DIM] = o[g * tc:(g + 1) * tc].astype(BF16)


def _attn_call(sink, q, k, vt, qc, kc, vct, need_ctx):
    bsz, t, nq = q.shape
    tc = kc.shape[1]
    group = nq // HEAD_DIM // N_KV_HEADS
    gw = group * HEAD_DIM
    in_specs = [
        pl.BlockSpec(memory_space=pltpu.SMEM),
        pl.BlockSpec((1, t, gw), lambda b, h: (b, 0, h)),
        pl.BlockSpec((1, t, HEAD_DIM), lambda b, h: (b, 0, h)),
        pl.BlockSpec((1, HEAD_DIM, t), lambda b, h: (b, h, 0)),
        pl.BlockSpec((1, tc, HEAD_DIM), lambda b, h: (b, 0, h)),
        pl.BlockSpec((1, HEAD_DIM, tc), lambda b, h: (b, h, 0)),
    ]
    out_specs = [pl.BlockSpec((1, t, gw), lambda b, h: (b, 0, h))]
    out_shape = [jax.ShapeDtypeStruct((bsz, t, nq), BF16)]
    args = [sink, q, k, vt, kc, vct]
    if need_ctx:
        in_specs.append(pl.BlockSpec((1, tc, gw), lambda b, h: (b, 0, h)))
        out_specs.append(pl.BlockSpec((1, tc, gw), lambda b, h: (b, 0, h)))
        out_shape.append(jax.ShapeDtypeStruct((bsz, tc, nq), BF16))
        args.append(qc)
    outs = pl.pallas_call(
        functools.partial(_attn_kernel, group=group, need_ctx=need_ctx),
        grid=(bsz, N_KV_HEADS),
        in_specs=in_specs,
        out_specs=out_specs,
        out_shape=out_shape,
        compiler_params=_cparams(2),
        name="attn",
    )(*args)
    return (outs[0], outs[1]) if need_ctx else (outs[0], None)


def _rope_tables(t):
    half = HEAD_DIM // 2
    freqs = ROPE_BASE ** (-jnp.arange(0, half, 2, dtype=F32) / half)
    pos = jnp.arange(t)
    rows = (pos // GRID_W).astype(F32)[:, None] * freqs
    cols = (pos % GRID_W).astype(F32)[:, None] * freqs
    cos = jnp.concatenate([jnp.cos(rows), jnp.cos(rows), jnp.cos(cols), jnp.cos(cols)], axis=1)
    sin = jnp.concatenate([-jnp.sin(rows), jnp.sin(rows), -jnp.sin(cols), jnp.sin(cols)], axis=1)
    return cos, sin


def _count(mask):
    return jnp.sum(jnp.where(mask, 1.0, 0.0), axis=1, keepdims=True)


def _route_plan(x_ref, sh_ref, sc_ref, ng_ref, rw_ref, slot_ref, h_s, aff_s, slotp_s, slot_s, gate_s,
                cap):
    t = x_ref.shape[1]
    n_exp = slot_s.shape[0]
    chunk = min(t, ROW_CHUNK)
    rw = rw_ref[0]
    rw_hi = rw.astype(BF16)
    rw_lo = (rw - rw_hi.astype(F32)).astype(BF16)
    lane = lax.broadcasted_iota(jnp.int32, (chunk, LANES), 1)
    for r0 in range(0, t, chunk):
        h = _norm_mod(x_ref[0, r0:r0 + chunk], ng_ref[0], sh_ref[0], sc_ref[0])
        h_hi = h.astype(BF16)
        h_lo = (h - h_hi.astype(F32)).astype(BF16)
        h_s[r0:r0 + chunk] = h_hi
        logits = _dot(h_hi, rw_hi) + (_dot(h_lo, rw_hi) + _dot(h_hi, rw_lo))
        logits = jnp.where(lane < n_exp, logits, NEG_INF)
        ex = jnp.exp(logits - jnp.max(logits, axis=1, keepdims=True))
        aff = ex / jnp.sum(ex, axis=1, keepdims=True)
        aff_s[:, r0:r0 + chunk] = aff.T
    aff_t = aff_s[0:n_exp, :]

    def as_f32(word):
        return lax.bitcast_convert_type(word, F32)

    def search(n_digits, accept):
        def body(i, w):
            shift = SEARCH_BITS * (n_digits - 1 - i)
            digit = jnp.zeros_like(w)
            for c in range(1, 1 << SEARCH_BITS):
                ok = accept(w | jnp.left_shift(jnp.int32(c), shift))
                digit = digit + jnp.where(ok, 1, 0)
            return w | jnp.left_shift(digit, shift)

        return lax.fori_loop(0, n_digits, body, jnp.zeros((n_exp, 1), jnp.int32))

    kth = search(30 // SEARCH_BITS, lambda w: _count(aff_t >= as_f32(w)) >= cap)
    above = aff_t >= as_f32(kth + 1)
    tie = (aff_t >= as_f32(kth)) & jnp.logical_not(above)
    need = cap - _count(above)
    idx = lax.broadcasted_iota(jnp.int32, aff_t.shape, 1)
    idx_digits = -(-max(1, (t - 1).bit_length()) // SEARCH_BITS)
    last = search(idx_digits, lambda w: _count(tie & (idx < w)) < need)
    sel = jnp.where(above | (tie & (idx <= last)), 1.0, 0.0)

    upper = jnp.where(lax.broadcasted_iota(jnp.int32, (chunk, chunk), 0)
                      < lax.broadcasted_iota(jnp.int32, (chunk, chunk), 1), 1.0, 0.0).astype(BF16)
    running = jnp.zeros((n_exp, 1), F32)
    slotp_s[...] = jnp.full(slotp_s.shape, -1.0, F32)
    for r0 in range(0, t, chunk):
        sel_c = sel[:, r0:r0 + chunk]
        pos = _dot(sel_c.astype(BF16), upper) + running
        running = running + jnp.sum(sel_c, axis=1, keepdims=True)
        slotp_s[0:n_exp, r0:r0 + chunk] = jnp.where(sel_c > 0.5, pos, -1.0)
    gate_t = sel * aff_t
    for ee in range(n_exp):
        slot_s[ee] = slotp_s[ee:ee + 1, :]
        gate_s[ee] = gate_t[ee:ee + 1, :]
    for r0 in range(0, t, chunk):
        slot_ref[0, r0:r0 + chunk] = slotp_s[:, r0:r0 + chunk].T


def _route_gather(e0, xg_ref, gs_ref, row0, h_s, slot_s, gate_s, cap):
    t = h_s.shape[0]
    slot_id = lax.broadcasted_iota(jnp.int32, (cap, t), 0).astype(F32)
    for i in range(xg_ref.shape[0]):
        onehot = slot_id == slot_s[e0 + i]
        xg_ref[i, 0, row0:row0 + cap] = _dot(
            jnp.where(onehot, 1.0, 0.0).astype(BF16), h_s[...]).astype(BF16)
        g = jnp.sum(jnp.where(onehot, gate_s[e0 + i], 0.0), axis=1, keepdims=True)
        gs_ref[i, 0, row0:row0 + cap] = jnp.broadcast_to(g, (cap, LANES))


def _route_kernel(*refs, caps):
    ns = len(caps)
    ins = [refs[3 * s:3 * s + 3] for s in range(ns)]
    ng_ref, rw_ref, xg_ref, gs_ref = refs[3 * ns:3 * ns + 4]
    slot_refs = refs[3 * ns + 4:4 * ns + 4]
    scratch = [refs[4 * ns + 4 + 5 * s:4 * ns + 9 + 5 * s] for s in range(ns)]
    e = pl.program_id(1)

    @pl.when(e == 0)
    def _():
        for s in range(ns):
            _route_plan(*ins[s], ng_ref, rw_ref, slot_refs[s], *scratch[s], caps[s])

    row0 = 0
    for s in range(ns):
        h_s, _, _, slot_s, gate_s = scratch[s]
        _route_gather(e * xg_ref.shape[0], xg_ref, gs_ref, row0, h_s, slot_s, gate_s, caps[s])
        row0 += caps[s]


def _route_call(streams, norm_g, l, router_p, n_exp):
    bsz, _, d = streams[0][0].shape
    caps = tuple(st[3] for st in streams)
    rows = sum(caps)
    fit = max(1, ROUTE_OUT_BYTES // (rows * d * 2))
    eps = max(k for k in range(1, n_exp + 1) if n_exp % k == 0 and k <= fit)
    vec = lambda b, e: (b, 0, 0)
    in_specs, args, slot_specs, slot_shapes, scratch = [], [], [], [], []
    for x, sh, sc, _ in streams:
        t = x.shape[1]
        in_specs += [pl.BlockSpec((1, t, d), vec), pl.BlockSpec((1, 1, d), vec),
                     pl.BlockSpec((1, 1, d), vec)]
        args += [x, sh, sc]
        slot_specs.append(pl.BlockSpec((1, t, LANES), vec))
        slot_shapes.append(jax.ShapeDtypeStruct((bsz, t, LANES), F32))
        scratch += [
            pltpu.VMEM((t, d), BF16),
            pltpu.VMEM((LANES, t), F32),
            pltpu.VMEM((LANES, t), F32),
            pltpu.VMEM((n_exp, 1, t), F32),
            pltpu.VMEM((n_exp, 1, t), F32),
        ]
    in_specs += [pl.BlockSpec((1, 1, d), lambda b, e: (l, 0, 0)),
                 pl.BlockSpec((1, d, LANES), lambda b, e: (l, 0, 0))]
    outs = pl.pallas_call(
        functools.partial(_route_kernel, caps=caps),
        grid=(bsz, n_exp // eps),
        in_specs=in_specs,
        out_specs=[
            pl.BlockSpec((eps, 1, rows, d), lambda b, e: (e, b, 0, 0)),
            pl.BlockSpec((eps, 1, rows, LANES), lambda b, e: (e, b, 0, 0)),
        ] + slot_specs,
        out_shape=[
            jax.ShapeDtypeStruct((n_exp, bsz, rows, d), BF16),
            jax.ShapeDtypeStruct((n_exp, bsz, rows, LANES), F32),
        ] + slot_shapes,
        scratch_shapes=scratch,
        compiler_params=_cparams(2),
        name="route",
    )(*args, norm_g, router_p)
    return outs[0], outs[1], outs[2:]


def _ffn_kernel(x_ref, gs_ref, wg_ref, wu_ref, wd_ref, y_ref, w_s):
    @pl.when(pl.program_id(1) == 0)
    def _():
        w_s[0] = wg_ref[0, 0].astype(BF16)
        w_s[1] = wu_ref[0, 0].astype(BF16)
        w_s[2] = wd_ref[0, 0].astype(BF16)

    x = x_ref[0]
    f = w_s.shape[2]
    acc = None
    for c0 in range(0, f, FFN_COLS):
        a = _dot(x, w_s[0, :, c0:c0 + FFN_COLS])
        u = _dot(x, w_s[1, :, c0:c0 + FFN_COLS])
        hmid = (a * _sigmoid(a) * u).astype(BF16)
        part = _dot(hmid, w_s[2, c0:c0 + FFN_COLS, :])
        acc = part if acc is None else acc + part
    y_ref[0] = (acc * gs_ref[0][:, 0:1]).astype(BF16)


def _ffn_call(xg, gs, w_gate, w_up, w_down, l):
    n_exp, rows, d = xg.shape
    f = w_gate.shape[-1]
    tm = rows // max(1, -(-rows // FFN_ROWS))
    assert rows % tm == 0 and tm % (2 * SUBLANES) == 0, (rows, tm)
    w_map = lambda e, r: (l, e, 0, 0)
    return pl.pallas_call(
        _ffn_kernel,
        grid=(n_exp, rows // tm),
        in_specs=[
            pl.BlockSpec((1, tm, d), lambda e, r: (e, r, 0)),
            pl.BlockSpec((1, tm, LANES), lambda e, r: (e, r, 0)),
            pl.BlockSpec((1, 1, d, f), w_map),
            pl.BlockSpec((1, 1, d, f), w_map),
            pl.BlockSpec((1, 1, f, d), w_map),
        ],
        out_specs=pl.BlockSpec((1, tm, d), lambda e, r: (e, r, 0)),
        out_shape=jax.ShapeDtypeStruct((n_exp, rows, d), BF16),
        scratch_shapes=[pltpu.VMEM((3, d, f), BF16)],
        compiler_params=_cparams(2),
        name="ffn",
    )(xg, gs, w_gate, w_up, w_down)


def _combine_kernel(y_ref, slot_ref, x_ref, g_ref, fg_ref, o_ref, *, final_norm):
    n_exp, _, cap, _ = y_ref.shape
    tk = x_ref.shape[1]
    slots = slot_ref[0]
    lane = lax.broadcasted_iota(jnp.int32, (tk, cap), 1).astype(F32)
    acc = jnp.zeros(x_ref.shape[1:], F32)
    for e in range(n_exp):
        onehot = jnp.where(lane == slots[:, e:e + 1], 1.0, 0.0).astype(BF16)
        acc = acc + _dot(onehot, y_ref[e, 0])
    out = x_ref[0] + g_ref[0] * acc
    if final_norm:
        ms = jnp.mean(out * out, axis=-1, keepdims=True)
        out = out * lax.rsqrt(ms + NORM_EPS) * fg_ref[...]
    o_ref[0] = out


def _combine_call(y, slot, x, gate, final_g, cap, row0, final_norm):
    bsz, t, d = x.shape
    n_exp = y.shape[0]
    tk = min(t, COMBINE_ROWS)
    assert row0 % cap == 0
    return pl.pallas_call(
        functools.partial(_combine_kernel, final_norm=final_norm),
        grid=(bsz, t // tk),
        in_specs=[
            pl.BlockSpec((n_exp, 1, cap, d), lambda b, i: (0, b, row0 // cap, 0)),
            pl.BlockSpec((1, tk, LANES), lambda b, i: (b, i, 0)),
            pl.BlockSpec((1, tk, d), lambda b, i: (b, i, 0)),
            pl.BlockSpec((1, 1, d), lambda b, i: (b, 0, 0)),
            pl.BlockSpec((1, d), lambda b, i: (0, 0)),
        ],
        out_specs=pl.BlockSpec((1, tk, d), lambda b, i: (b, i, 0)),
        out_shape=jax.ShapeDtypeStruct(x.shape, F32),
        compiler_params=_cparams(2),
        name="combine",
    )(y, slot, x, gate, final_g)


def kernel(x, c, ctx, c_ctx, ada_w, ada_b, norm1_g, norm2_g, lru_w_in, lru_conv_w, lru_conv_b,
           lru_gate_w, lru_gate_b, lru_lambda, lru_w_out, attn_w_qkv, attn_sink, attn_w_o,
           moe_router, moe_w_gate, moe_w_up, moe_w_down, final_g):
    bsz, t, d = x.shape
    tc = ctx.shape[1]
    depth = ada_w.shape[0]
    n_exp = moe_router.shape[-1]
    cap_l = CAPACITY_FACTOR * t // n_exp
    cap_c = CAPACITY_FACTOR * tc // n_exp

    cond_rows = 2 * SUBLANES
    cond = jnp.zeros((cond_rows, d), F32).at[:bsz].set(c).at[bsz].set(c_ctx)
    mod = _ada_call(cond, ada_w, ada_b).reshape(depth, cond_rows, 6, 1, d)
    norm1 = norm1_g.reshape(depth, 1, d)
    norm2 = norm2_g.reshape(depth, 1, d)
    router_p = jnp.pad(moe_router, ((0, 0), (0, 0), (0, LANES - n_exp)))
    final_g2 = final_g.reshape(1, d)
    cos, sin_signed = _rope_tables(t)
    ones_c = jnp.ones((tc, HEAD_DIM), F32)

    for l in range(depth):
        need_ctx = l < depth - 1
        lat = [mod[l, :bsz, i] for i in range(6)]
        con = [jnp.broadcast_to(mod[l, bsz, i], (bsz, 1, d)) for i in range(6)]
        j = l // 2
        if l % 2 == 0:
            s_l, s_c = _lru_call(x, ctx, lat[0], lat[1], con[0], con[1], norm1, l,
                                 lru_w_in[j].astype(BF16), lru_conv_w[j], lru_conv_b[j],
                                 (0.5 * lru_gate_w[j]).astype(BF16), 0.5 * lru_gate_b[j],
                                 lru_lambda[j])
            w_out = lru_w_out[j].astype(BF16)
        else:
            w_qkv = attn_w_qkv[j].astype(BF16)
            q, k, v = _qkv_call(x, lat[0], lat[1], norm1, l, w_qkv, cos, sin_signed, True)
            qc, kc, vc = _qkv_call(ctx, con[0], con[1], norm1, l, w_qkv, ones_c, ones_c, False)
            s_l, s_c = _attn_call(attn_sink[j], q, k, v, qc, kc, vc, need_ctx)
            w_out = attn_w_o[j].astype(BF16)
        x = _resid_call(s_l, w_out, x, lat[2])
        if need_ctx:
            ctx = _resid_call(s_c, w_out, ctx, con[2])

        streams = [(x, lat[3], lat[4], cap_l)]
        if need_ctx:
            streams.append((ctx, con[3], con[4], cap_c))
        xg, gs, slots = _route_call(streams, norm2, l, router_p, n_exp)
        rows = xg.shape[2]
        y = _ffn_call(xg.reshape(n_exp, bsz * rows, d), gs.reshape(n_exp, bsz * rows, LANES),
                      moe_w_gate, moe_w_up, moe_w_down, l).reshape(xg.shape)
        x = _combine_call(y, slots[0], x, lat[5], final_g2, cap_l, 0, not need_ctx)
        if need_ctx:
            ctx = _combine_call(y, slots[1], ctx, con[5], final_g2, cap_c, cap_l, False)
    return x
```

```python
import functools
import math

import jax
import jax.numpy as jnp
from jax import lax
from jax.experimental import pallas as pl
from jax.experimental.pallas import tpu as pltpu

F32 = jnp.float32
BF16 = jnp.bfloat16

LANES = 128
SUBLANES = 8
VMEM_LIMIT_BYTES = 58 * 1024 * 1024

NORM_EPS = 1e-6
NEG_INF = -1e30
LRU_C = 8.0
LRU_BLOCK_W = 128
CONV_W = 4
CONV_HALO = 16
LRU_ROWS = 512
HEAD_DIM = 128
N_KV_HEADS = 2
QBLOCK = 128
WINDOW = 128
GRID_W = 64
ROPE_BASE = 10000.0
CAPACITY_FACTOR = 2

ROW_CHUNK = 256
PROJ_ROWS = 1024
ATTN_HEADS = 4
SCAN_UNROLL = 4
MOD_TILE = 1536
SEARCH_BITS = 3
ROUTE_OUT_BYTES = 5 * 1024 * 1024
FFN_ROWS = 1152
FFN_COLS = 256
COMBINE_ROWS = 1024
SLOT_WINDOW = LANES // 2


def _cparams(n_axes):
    return pltpu.CompilerParams(
        dimension_semantics=("arbitrary",) * n_axes, vmem_limit_bytes=VMEM_LIMIT_BYTES)


def _dot(a, b):
    return jnp.dot(a, b, preferred_element_type=F32)


def _dot_nt(a, b):
    return lax.dot_general(a, b, (((1,), (1,)), ((), ())), preferred_element_type=F32)


def _sigmoid(x):
    return 0.5 * jnp.tanh(0.5 * x) + 0.5


def _sqrt_nonneg(z):
    return jnp.where(z > 0.0, z * lax.rsqrt(z), 0.0)


def _norm_mod(x, g, shift, scale):
    ms = jnp.mean(x * x, axis=-1, keepdims=True)
    y = x * lax.rsqrt(ms + NORM_EPS) * g
    return y * (1.0 + scale) + shift


def _ada_kernel(c_ref, w_ref, b_ref, o_ref):
    c = c_ref[...]
    s = (c * _sigmoid(c)).astype(BF16)
    o_ref[0] = _dot(s, w_ref[0].astype(BF16)) + b_ref[0]


def _ada_call(cond, ada_w, ada_b):
    n_layers, d, n_out = ada_w.shape
    rows = cond.shape[0]
    return pl.pallas_call(
        _ada_kernel,
        grid=(n_layers, n_out // MOD_TILE),
        in_specs=[
            pl.BlockSpec((rows, d), lambda l, n: (0, 0)),
            pl.BlockSpec((1, d, MOD_TILE), lambda l, n: (l, 0, n)),
            pl.BlockSpec((1, 1, MOD_TILE), lambda l, n: (l, 0, n)),
        ],
        out_specs=pl.BlockSpec((1, rows, MOD_TILE), lambda l, n: (l, 0, n)),
        out_shape=jax.ShapeDtypeStruct((n_layers, rows, n_out), F32),
        compiler_params=_cparams(2),
        name="ada",
    )(cond, ada_w, ada_b.reshape(n_layers, 1, n_out))


def _resid_kernel(a_ref, w_ref, x_ref, g_ref, o_ref):
    o_ref[0] = x_ref[0] + g_ref[0] * _dot(a_ref[0], w_ref[...])


def _resid_call(act, w, x, gate):
    bsz, t, d = x.shape
    k = act.shape[-1]
    tm = min(t, PROJ_ROWS)
    return pl.pallas_call(
        _resid_kernel,
        grid=(bsz, t // tm),
        in_specs=[
            pl.BlockSpec((1, tm, k), lambda b, i: (b, i, 0)),
            pl.BlockSpec((k, d), lambda b, i: (0, 0)),
            pl.BlockSpec((1, tm, d), lambda b, i: (b, i, 0)),
            pl.BlockSpec((1, 1, d), lambda b, i: (b, 0, 0)),
        ],
        out_specs=pl.BlockSpec((1, tm, d), lambda b, i: (b, i, 0)),
        out_shape=jax.ShapeDtypeStruct(x.shape, F32),
        compiler_params=_cparams(2),
        name="resid",
    )(act, w, x, gate)


LRU_CB = 2 * LRU_BLOCK_W


def _log_sigmoid(x):
    return jnp.minimum(x, 0.0) - jnp.log1p(jnp.exp(-jnp.abs(x)))


def _conv_time(u, cw, cb):
    n = u.shape[0]
    left = (CONV_W - 1) // 2
    acc = cb + u * cw[left:left + 1]
    for k in range(CONV_W):
        off = k - left
        if off != 0:
            acc = acc + pltpu.roll(u, (-off) % n, 0) * cw[k:k + 1]
    return acc[CONV_HALO:n - CONV_HALO]


N_SEG = SUBLANES
SEG_SLACK = N_SEG * 2 * SUBLANES


def _segments(n):
    seg = n // N_SEG
    assert n % (N_SEG * SUBLANES) == 0, n
    pad = SUBLANES if (seg // SUBLANES) % 2 == 0 else 2 * SUBLANES
    return seg, seg + pad


def _lru_kernel(xl_ref, xc_ref, shl_ref, scl_ref, shc_ref, scc_ref, ng_ref, wy_ref, wx_ref,
                cw_ref, cb_ref, gw_ref, gb_ref, lam_ref, ol_ref, oc_ref,
                hl_s, hc_s, a_s, b_s, hs_s, ps_s, y_s, o_s):
    j = pl.program_id(1)
    t_lat = xl_ref.shape[1]
    t_ctx = xc_ref.shape[1]
    gbk = LRU_CB // LRU_BLOCK_W

    @pl.when(j == 0)
    def _():
        g = ng_ref[0]
        for x_ref, h_s, sh_ref, sc_ref in ((xl_ref, hl_s, shl_ref, scl_ref),
                                           (xc_ref, hc_s, shc_ref, scc_ref)):
            n = x_ref.shape[1]
            step = min(n, ROW_CHUNK)
            halo = jnp.zeros((CONV_HALO, h_s.shape[1]), BF16)
            h_s[0:CONV_HALO] = halo
            for r0 in range(0, n, step):
                h_s[CONV_HALO + r0:CONV_HALO + r0 + step] = _norm_mod(
                    x_ref[0, r0:r0 + step], g, sh_ref[0], sc_ref[0]).astype(BF16)
            h_s[CONV_HALO + n:2 * CONV_HALO + n] = halo

    half_rate = (0.5 * LRU_C / math.log(2.0)) * _log_sigmoid(lam_ref[...])

    def branches(h_s, n):
        seg, pitch = _segments(n)
        rows = min(n, LRU_ROWS)
        assert n % rows == 0 and rows % seg == 0, (n, rows, seg)
        for r0 in range(0, n, rows):
            branch_rows(h_s, r0, rows, seg, pitch)

    def branch_rows(h_s, r0, rows, seg, pitch):
        def put(store, val):
            for s in range(r0 // seg, (r0 + rows) // seg):
                store(slice(s * pitch, s * pitch + seg), val[s * seg - r0:(s + 1) * seg - r0])

        y = jax.nn.gelu(_dot(h_s[CONV_HALO + r0:CONV_HALO + r0 + rows], wy_ref[...]))
        xb = _conv_time(_dot(h_s[r0:r0 + rows + 2 * CONV_HALO], wx_ref[...]),
                        cw_ref[...], cb_ref[...])
        for kb in range(gbk):
            lanes = slice(kb * LRU_BLOCK_W, (kb + 1) * LRU_BLOCK_W)

            def put_y(rows, v, kb=kb):
                y_s[kb, rows] = v

            put(put_y, y[:, lanes])
            xk = xb[:, lanes]
            xk16 = xk.astype(BF16)
            half_x = 0.5 * xk
            for d in range(2):
                th = jnp.tanh(_dot(xk16, gw_ref[d, kb]) + gb_ref[d, kb])
                th_r = th[:, :LRU_BLOCK_W]
                th_i = th[:, LRU_BLOCK_W:]
                a = jnp.exp2(th_r * half_rate[d:d + 1, lanes] + half_rate[d:d + 1, lanes])

                def put_a(rows, v, d=d, kb=kb):
                    a_s[d, kb, rows] = v

                def put_b(rows, v, d=d, kb=kb):
                    b_s[d, kb, rows] = v

                put(put_a, a)
                put(put_b, _sqrt_nonneg(1.0 - a * a) * (th_i * half_x + half_x))

    def scan(n, carry_in, o_ref):
        seg, pitch = _segments(n)
        chains = [(d, kb) for d in range(2) for kb in range(gbk)]

        def advance(jstep, state):
            out = []
            for c, (d, kb) in enumerate(chains):
                jj = (seg - 1 - jstep) if d == 1 else jstep
                h, p = state[2 * c], state[2 * c + 1]
                a = a_s[d, kb, pl.ds(jj, N_SEG, stride=pitch), :]
                b = b_s[d, kb, pl.ds(jj, N_SEG, stride=pitch), :]
                h = a * h + b
                p = p * a
                rows = pl.ds(pl.multiple_of(jj * N_SEG, N_SEG), N_SEG)
                hs_s[d, kb, rows] = h
                ps_s[d, kb, rows] = p
                out += [h, p]
            return tuple(out)

        init = (jnp.zeros((N_SEG, LRU_BLOCK_W), F32), jnp.ones((N_SEG, LRU_BLOCK_W), F32))
        state = lax.fori_loop(0, seg, advance, init * len(chains), unroll=SCAN_UNROLL)

        row = lax.broadcasted_iota(jnp.int32, (N_SEG, LRU_BLOCK_W), 0)
        enter, carry_out = [], []
        for c, (d, kb) in enumerate(chains):
            h_end, p_end = state[2 * c], state[2 * c + 1]
            cur = carry_in[c]
            vec = jnp.zeros((N_SEG, LRU_BLOCK_W), F32)
            for s in (range(N_SEG - 1, -1, -1) if d == 1 else range(N_SEG)):
                vec = jnp.where(row == s, cur, vec)
                cur = h_end[s:s + 1] + p_end[s:s + 1] * cur
            enter.append(vec)
            carry_out.append(cur)

        def finish(jstep, carry):
            rows = pl.ds(pl.multiple_of(jstep * N_SEG, N_SEG), N_SEG)
            strided = pl.ds(jstep, N_SEG, stride=pitch)
            for kb in range(gbk):
                tot = None
                for c, (d, kb2) in enumerate(chains):
                    if kb2 == kb:
                        part = hs_s[d, kb, rows] + ps_s[d, kb, rows] * enter[c]
                        tot = part if tot is None else tot + part
                o_s[kb, strided, :] = tot * y_s[kb, strided, :]
            return carry

        lax.fori_loop(0, seg, finish, 0, unroll=SCAN_UNROLL)
        for kb in range(gbk):
            for s in range(N_SEG):
                o_ref[0, s * seg:(s + 1) * seg, kb * LRU_BLOCK_W:(kb + 1) * LRU_BLOCK_W] = (
                    o_s[kb, s * pitch:s * pitch + seg].astype(BF16))
        return carry_out

    branches(hc_s, t_ctx)
    carries = scan(t_ctx, [jnp.zeros((1, LRU_BLOCK_W), F32)] * (2 * gbk), oc_ref)
    branches(hl_s, t_lat)
    scan(t_lat, carries, ol_ref)


def _lru_call(x, ctx, sh_l, sc_l, sh_c, sc_c, norm_g, l, w_in, conv_w, conv_b, gate_w, gate_b, lam):
    bsz, t, d = x.shape
    tc = ctx.shape[1]
    width = w_in.shape[1] // 2
    nblk = width // LRU_CB
    gbk = LRU_CB // LRU_BLOCK_W
    vec = lambda b, c: (b, 0, 0)
    return pl.pallas_call(
        _lru_kernel,
        grid=(bsz, nblk),
        in_specs=[
            pl.BlockSpec((1, t, d), vec),
            pl.BlockSpec((1, tc, d), vec),
            pl.BlockSpec((1, 1, d), vec),
            pl.BlockSpec((1, 1, d), vec),
            pl.BlockSpec((1, 1, d), vec),
            pl.BlockSpec((1, 1, d), vec),
            pl.BlockSpec((1, 1, d), lambda b, c: (l, 0, 0)),
            pl.BlockSpec((d, LRU_CB), lambda b, c: (0, c)),
            pl.BlockSpec((d, LRU_CB), lambda b, c: (0, nblk + c)),
            pl.BlockSpec((CONV_W, LRU_CB), lambda b, c: (0, c)),
            pl.BlockSpec((1, LRU_CB), lambda b, c: (0, c)),
            pl.BlockSpec((2, gbk, LRU_BLOCK_W, 2 * LRU_BLOCK_W), lambda b, c: (0, c, 0, 0)),
            pl.BlockSpec((2, gbk, 1, 2 * LRU_BLOCK_W), lambda b, c: (0, c, 0, 0)),
            pl.BlockSpec((2, LRU_CB), lambda b, c: (0, c)),
        ],
        out_specs=[
            pl.BlockSpec((1, t, LRU_CB), lambda b, c: (b, 0, c)),
            pl.BlockSpec((1, tc, LRU_CB), lambda b, c: (b, 0, c)),
        ],
        out_shape=[
            jax.ShapeDtypeStruct((bsz, t, width), BF16),
            jax.ShapeDtypeStruct((bsz, tc, width), BF16),
        ],
        scratch_shapes=[
            pltpu.VMEM((t + 2 * CONV_HALO, d), BF16),
            pltpu.VMEM((tc + 2 * CONV_HALO, d), BF16),
            pltpu.VMEM((2, gbk, t + SEG_SLACK, LRU_BLOCK_W), F32),
            pltpu.VMEM((2, gbk, t + SEG_SLACK, LRU_BLOCK_W), F32),
            pltpu.VMEM((2, gbk, t, LRU_BLOCK_W), F32),
            pltpu.VMEM((2, gbk, t, LRU_BLOCK_W), F32),
            pltpu.VMEM((gbk, t + SEG_SLACK, LRU_BLOCK_W), F32),
            pltpu.VMEM((gbk, t + SEG_SLACK, LRU_BLOCK_W), F32),
        ],
        compiler_params=_cparams(2),
        name="lru",
    )(x, ctx, sh_l, sc_l, sh_c, sc_c, norm_g, w_in, w_in, conv_w, conv_b.reshape(1, width),
      gate_w, gate_b.reshape(2, -1, 1, 2 * LRU_BLOCK_W), lam)


def _rope(x, cos, sin_signed):
    lane = lax.broadcasted_iota(jnp.int32, x.shape, 1)
    quarter = HEAD_DIM // 4
    partner = jnp.where((lane & quarter) == 0,
                        pltpu.roll(x, HEAD_DIM - quarter, 1), pltpu.roll(x, quarter, 1))
    return x * cos + partner * sin_signed


def _qkv_kernel(x_ref, sh_ref, sc_ref, ng_ref, w_ref, cos_ref, sin_ref, q_ref, k_ref, vt_ref,
                *, n_heads, rotary):
    h = _norm_mod(x_ref[0], ng_ref[0], sh_ref[0], sc_ref[0]).astype(BF16)
    u = _dot(h, w_ref[...])
    scale = HEAD_DIM ** -0.5
    nq = n_heads * HEAD_DIM
    nk = N_KV_HEADS * HEAD_DIM
    for hd in range(n_heads + N_KV_HEADS):
        c = u[:, hd * HEAD_DIM:(hd + 1) * HEAD_DIM]
        if rotary:
            c = _rope(c, cos_ref[...], sin_ref[...])
        if hd < n_heads:
            q_ref[0, :, hd * HEAD_DIM:(hd + 1) * HEAD_DIM] = (c * scale).astype(BF16)
        else:
            kk = hd - n_heads
            k_ref[0, :, kk * HEAD_DIM:(kk + 1) * HEAD_DIM] = c.astype(BF16)
    vt_ref[0] = u[:, nq + nk:].T.astype(BF16)


def _qkv_call(x, sh, sc, norm_g, l, w_qkv, cos, sin_signed, rotary):
    bsz, t, d = x.shape
    nk = N_KV_HEADS * HEAD_DIM
    nq = w_qkv.shape[1] - 2 * nk
    tm = min(t, PROJ_ROWS)
    vec = lambda b, i: (b, 0, 0)
    return pl.pallas_call(
        functools.partial(_qkv_kernel, n_heads=nq // HEAD_DIM, rotary=rotary),
        grid=(bsz, t // tm),
        in_specs=[
            pl.BlockSpec((1, tm, d), lambda b, i: (b, i, 0)),
            pl.BlockSpec((1, 1, d), vec),
            pl.BlockSpec((1, 1, d), vec),
            pl.BlockSpec((1, 1, d), lambda b, i: (l, 0, 0)),
            pl.BlockSpec(w_qkv.shape, lambda b, i: (0, 0)),
            pl.BlockSpec((tm, HEAD_DIM), lambda b, i: (i, 0)),
            pl.BlockSpec((tm, HEAD_DIM), lambda b, i: (i, 0)),
        ],
        out_specs=[
            pl.BlockSpec((1, tm, nq), lambda b, i: (b, i, 0)),
            pl.BlockSpec((1, tm, nk), lambda b, i: (b, i, 0)),
            pl.BlockSpec((1, nk, tm), lambda b, i: (b, 0, i)),
        ],
        out_shape=[
            jax.ShapeDtypeStruct((bsz, t, nq), BF16),
            jax.ShapeDtypeStruct((bsz, t, nk), BF16),
            jax.ShapeDtypeStruct((bsz, nk, t), BF16),
        ],
        compiler_params=_cparams(2),
        name="qkv",
    )(x, sh, sc, norm_g, w_qkv, cos, sin_signed)


def _stack_heads(x, group):
    return jnp.concatenate([x[:, g * HEAD_DIM:(g + 1) * HEAD_DIM] for g in range(group)], axis=0)


def _attn_kernel(sink_ref, q_ref, k_ref, vt_ref, kc_ref, vct_ref, *rest, group, need_ctx):
    if need_ctx:
        qc_ref, o_ref, oc_ref = rest
    else:
        (o_ref,) = rest
    kh = pl.program_id(1)
    t = k_ref.shape[1]
    nblk = t // QBLOCK

    def sink_row(cols_per_head):
        col = lax.broadcasted_iota(jnp.int32, (1, group * cols_per_head), 1)
        row = jnp.zeros((1, group * cols_per_head), F32)
        for g in range(group):
            row = jnp.where(col >= g * cols_per_head, sink_ref[kh * group + g], row)
        return row

    kc = kc_ref[0]
    vct = vct_ref[0]
    sink_q = sink_row(QBLOCK)

    def softmax_vp(scores_values, sink):
        m = sink
        for sc, _ in scores_values:
            m = jnp.maximum(m, jnp.max(sc, axis=0, keepdims=True))
        den = jnp.exp(sink - m)
        acc = None
        for sc, val in scores_values:
            p = jnp.exp(sc - m)
            den = den + jnp.sum(p, axis=0, keepdims=True)
            vp = _dot(val, p.astype(BF16))
            acc = vp if acc is None else acc + vp
        return (acc / den).T

    def attend(r0, has_prev, has_next):
        lo = r0 - QBLOCK if has_prev else r0
        if not isinstance(lo, int):
            lo = pl.multiple_of(lo, QBLOCK)
        width = QBLOCK * (1 + has_prev + has_next)
        q_blk = q_ref[0, pl.ds(r0, QBLOCK), :]
        k_band = k_ref[0, pl.ds(lo, width), :]
        vt_band = vt_ref[0, :, pl.ds(lo, width)]
        sub = min(group, ATTN_HEADS)
        kj = lax.broadcasted_iota(jnp.int32, (QBLOCK, sub * QBLOCK), 0)
        qi = lax.broadcasted_iota(jnp.int32, (QBLOCK, sub * QBLOCK), 1) & (QBLOCK - 1)
        for h0 in range(0, group, sub):
            cols = slice(h0 * HEAD_DIM, (h0 + sub) * HEAD_DIM)
            qs = _stack_heads(q_blk[:, cols], sub)
            s = _dot_nt(k_band, qs)
            pieces = []
            if has_prev:
                pieces.append(jnp.where(QBLOCK + qi - kj <= WINDOW, s[:QBLOCK], NEG_INF))
            c0 = QBLOCK * has_prev
            pieces.append(s[c0:c0 + QBLOCK])
            if has_next:
                pieces.append(jnp.where(QBLOCK + kj - qi <= WINDOW, s[c0 + QBLOCK:], NEG_INF))
            s = jnp.concatenate(pieces, axis=0)
            o = softmax_vp([(s, vt_band), (_dot_nt(kc, qs), vct)], sink_q[:, cols])
            for g in range(sub):
                o_ref[0, pl.ds(r0, QBLOCK), (h0 + g) * HEAD_DIM:(h0 + g + 1) * HEAD_DIM] = (
                    o[g * QBLOCK:(g + 1) * QBLOCK].astype(BF16))

    attend(0, False, nblk > 1)
    if nblk > 1:
        attend((nblk - 1) * QBLOCK, True, False)

    def body(n, carry):
        attend(pl.multiple_of(n * QBLOCK, QBLOCK), True, True)
        return carry

    lax.fori_loop(1, nblk - 1, body, 0, unroll=7)

    if need_ctx:
        tc = qc_ref.shape[1]
        o = softmax_vp([(_dot_nt(kc, _stack_heads(qc_ref[0], group)), vct)], sink_row(tc))
        for g in range(group):
            oc_ref[0, :, g * HEAD_DIM:(g + 1) * HEAD_DIM] = o[g * tc:(g + 1) * tc].astype(BF16)


def _attn_call(sink, q, k, vt, qc, kc, vct, need_ctx):
    bsz, t, nq = q.shape
    tc = kc.shape[1]
    group = nq // HEAD_DIM // N_KV_HEADS
    gw = group * HEAD_DIM
    in_specs = [
        pl.BlockSpec(memory_space=pltpu.SMEM),
        pl.BlockSpec((1, t, gw), lambda b, h: (b, 0, h)),
        pl.BlockSpec((1, t, HEAD_DIM), lambda b, h: (b, 0, h)),
        pl.BlockSpec((1, HEAD_DIM, t), lambda b, h: (b, h, 0)),
        pl.BlockSpec((1, tc, HEAD_DIM), lambda b, h: (b, 0, h)),
        pl.BlockSpec((1, HEAD_DIM, tc), lambda b, h: (b, h, 0)),
    ]
    out_specs = [pl.BlockSpec((1, t, gw), lambda b, h: (b, 0, h))]
    out_shape = [jax.ShapeDtypeStruct((bsz, t, nq), BF16)]
    args = [sink, q, k, vt, kc, vct]
    if need_ctx:
        in_specs.append(pl.BlockSpec((1, tc, gw), lambda b, h: (b, 0, h)))
        out_specs.append(pl.BlockSpec((1, tc, gw), lambda b, h: (b, 0, h)))
        out_shape.append(jax.ShapeDtypeStruct((bsz, tc, nq), BF16))
        args.append(qc)
    outs = pl.pallas_call(
        functools.partial(_attn_kernel, group=group, need_ctx=need_ctx),
        grid=(bsz, N_KV_HEADS),
        in_specs=in_specs,
        out_specs=out_specs,
        out_shape=out_shape,
        compiler_params=_cparams(2),
        name="attn",
    )(*args)
    return (outs[0], outs[1]) if need_ctx else (outs[0], None)


def _rope_tables(t):
    half = HEAD_DIM // 2
    freqs = ROPE_BASE ** (-jnp.arange(0, half, 2, dtype=F32) / half)
    pos = jnp.arange(t)
    rows = (pos // GRID_W).astype(F32)[:, None] * freqs
    cols = (pos % GRID_W).astype(F32)[:, None] * freqs
    cos = jnp.concatenate([jnp.cos(rows), jnp.cos(rows), jnp.cos(cols), jnp.cos(cols)], axis=1)
    sin = jnp.concatenate([-jnp.sin(rows), jnp.sin(rows), -jnp.sin(cols), jnp.sin(cols)], axis=1)
    return cos, sin


def _count(mask):
    return jnp.sum(jnp.where(mask, 1.0, 0.0), axis=1, keepdims=True)


def _route_plan(x_ref, sh_ref, sc_ref, ng_ref, rw_ref, slot_ref, starts_ref,
                h_s, aff_s, slotp_s, slot_s, gate_s, cap):
    t = x_ref.shape[1]
    n_exp = slot_s.shape[0]
    chunk = min(t, ROW_CHUNK)
    rw = rw_ref[0]
    rw_hi = rw.astype(BF16)
    rw_lo = (rw - rw_hi.astype(F32)).astype(BF16)
    lane = lax.broadcasted_iota(jnp.int32, (chunk, LANES), 1)
    for r0 in range(0, t, chunk):
        h = _norm_mod(x_ref[0, r0:r0 + chunk], ng_ref[0], sh_ref[0], sc_ref[0])
        h_hi = h.astype(BF16)
        h_lo = (h - h_hi.astype(F32)).astype(BF16)
        h_s[r0:r0 + chunk] = h_hi
        logits = _dot(h_hi, rw_hi) + (_dot(h_lo, rw_hi) + _dot(h_hi, rw_lo))
        logits = jnp.where(lane < n_exp, logits, NEG_INF)
        ex = jnp.exp(logits - jnp.max(logits, axis=1, keepdims=True))
        aff = ex / jnp.sum(ex, axis=1, keepdims=True)
        aff_s[:, r0:r0 + chunk] = aff.T
    aff_t = aff_s[0:n_exp, :]

    def as_f32(word):
        return lax.bitcast_convert_type(word, F32)

    def search(n_digits, accept):
        def body(i, w):
            shift = SEARCH_BITS * (n_digits - 1 - i)
            digit = jnp.zeros_like(w)
            for c in range(1, 1 << SEARCH_BITS):
                ok = accept(w | jnp.left_shift(jnp.int32(c), shift))
                digit = digit + jnp.where(ok, 1, 0)
            return w | jnp.left_shift(digit, shift)

        return lax.fori_loop(0, n_digits, body, jnp.zeros((n_exp, 1), jnp.int32))

    kth = search(30 // SEARCH_BITS, lambda w: _count(aff_t >= as_f32(w)) >= cap)
    above = aff_t >= as_f32(kth + 1)
    tie = (aff_t >= as_f32(kth)) & jnp.logical_not(above)
    need = cap - _count(above)
    idx = lax.broadcasted_iota(jnp.int32, aff_t.shape, 1)
    idx_digits = -(-max(1, (t - 1).bit_length()) // SEARCH_BITS)
    last = search(idx_digits, lambda w: _count(tie & (idx < w)) < need)
    sel = jnp.where(above | (tie & (idx <= last)), 1.0, 0.0)

    upper = jnp.where(lax.broadcasted_iota(jnp.int32, (chunk, chunk), 0)
                      < lax.broadcasted_iota(jnp.int32, (chunk, chunk), 1), 1.0, 0.0).astype(BF16)
    running = jnp.zeros((n_exp, 1), F32)
    starts = jnp.zeros((n_exp, LANES), F32)
    chunk_id = lax.broadcasted_iota(jnp.int32, (n_exp, LANES), 1)
    slotp_s[...] = jnp.full(slotp_s.shape, -1.0, F32)
    for r0 in range(0, t, chunk):
        starts = jnp.where(chunk_id == r0 // chunk, running, starts)
        sel_c = sel[:, r0:r0 + chunk]
        pos = _dot(sel_c.astype(BF16), upper) + running
        running = running + jnp.sum(sel_c, axis=1, keepdims=True)
        slotp_s[0:n_exp, r0:r0 + chunk] = jnp.where(sel_c > 0.5, pos, -1.0)
    starts_ref[0] = starts
    gate_t = sel * aff_t
    for ee in range(n_exp):
        slot_s[ee] = slotp_s[ee:ee + 1, :]
        gate_s[ee] = gate_t[ee:ee + 1, :]
    for r0 in range(0, t, chunk):
        slot_ref[0, r0:r0 + chunk] = slotp_s[:, r0:r0 + chunk].T


def _route_gather(e0, xg_ref, gs_ref, row0, h_s, slot_s, gate_s, cap):
    t = h_s.shape[0]
    slot_id = lax.broadcasted_iota(jnp.int32, (cap, t), 0).astype(F32)
    for i in range(xg_ref.shape[0]):
        onehot = slot_id == slot_s[e0 + i]
        xg_ref[i, 0, row0:row0 + cap] = _dot(
            jnp.where(onehot, 1.0, 0.0).astype(BF16), h_s[...]).astype(BF16)
        g = jnp.sum(jnp.where(onehot, gate_s[e0 + i], 0.0), axis=1, keepdims=True)
        gs_ref[i, 0, row0:row0 + cap] = jnp.broadcast_to(g, (cap, LANES))


def _route_kernel(*refs, caps):
    ns = len(caps)
    ins = [refs[3 * s:3 * s + 3] for s in range(ns)]
    ng_ref, rw_ref, xg_ref, gs_ref = refs[3 * ns:3 * ns + 4]
    plan_outs = [refs[3 * ns + 4 + 2 * s:3 * ns + 6 + 2 * s] for s in range(ns)]
    scratch = [refs[5 * ns + 4 + 5 * s:5 * ns + 9 + 5 * s] for s in range(ns)]
    e = pl.program_id(1)

    @pl.when(e == 0)
    def _():
        for s in range(ns):
            _route_plan(*ins[s], ng_ref, rw_ref, *plan_outs[s], *scratch[s], caps[s])

    row0 = 0
    for s in range(ns):
        h_s, _, _, slot_s, gate_s = scratch[s]
        _route_gather(e * xg_ref.shape[0], xg_ref, gs_ref, row0, h_s, slot_s, gate_s, caps[s])
        row0 += caps[s]


def _route_call(streams, norm_g, l, router_p, n_exp):
    bsz, _, d = streams[0][0].shape
    caps = tuple(st[3] for st in streams)
    rows = sum(caps)
    fit = max(1, ROUTE_OUT_BYTES // (rows * d * 2))
    eps = max(k for k in range(1, n_exp + 1) if n_exp % k == 0 and k <= fit)
    vec = lambda b, e: (b, 0, 0)
    in_specs, args, slot_specs, slot_shapes, scratch = [], [], [], [], []
    for x, sh, sc, _ in streams:
        t = x.shape[1]
        in_specs += [pl.BlockSpec((1, t, d), vec), pl.BlockSpec((1, 1, d), vec),
                     pl.BlockSpec((1, 1, d), vec)]
        args += [x, sh, sc]
        slot_specs += [pl.BlockSpec((1, t, LANES), vec), pl.BlockSpec((1, n_exp, LANES), vec)]
        slot_shapes += [jax.ShapeDtypeStruct((bsz, t, LANES), F32),
                        jax.ShapeDtypeStruct((bsz, n_exp, LANES), F32)]
        scratch += [
            pltpu.VMEM((t, d), BF16),
            pltpu.VMEM((LANES, t), F32),
            pltpu.VMEM((LANES, t), F32),
            pltpu.VMEM((n_exp, 1, t), F32),
            pltpu.VMEM((n_exp, 1, t), F32),
        ]
    in_specs += [pl.BlockSpec((1, 1, d), lambda b, e: (l, 0, 0)),
                 pl.BlockSpec((1, d, LANES), lambda b, e: (l, 0, 0))]
    outs = pl.pallas_call(
        functools.partial(_route_kernel, caps=caps),
        grid=(bsz, n_exp // eps),
        in_specs=in_specs,
        out_specs=[
            pl.BlockSpec((eps, 1, rows, d), lambda b, e: (e, b, 0, 0)),
            pl.BlockSpec((eps, 1, rows, LANES), lambda b, e: (e, b, 0, 0)),
        ] + slot_specs,
        out_shape=[
            jax.ShapeDtypeStruct((n_exp, bsz, rows, d), BF16),
            jax.ShapeDtypeStruct((n_exp, bsz, rows, LANES), F32),
        ] + slot_shapes,
        scratch_shapes=scratch,
        compiler_params=_cparams(2),
        name="route",
    )(*args, norm_g, router_p)
    plans = []
    for s, (x, _, _, _) in enumerate(streams):
        n_chunks = -(-x.shape[1] // ROW_CHUNK)
        starts = outs[3 + 2 * s][:, :, :n_chunks].astype(jnp.int32)
        plans.append((outs[2 + 2 * s], jnp.swapaxes(starts, 1, 2).reshape(bsz, n_chunks * n_exp)))
    return outs[0], outs[1], plans


def _ffn_kernel(x_ref, gs_ref, wg_ref, wu_ref, wd_ref, y_ref, w_s):
    @pl.when(pl.program_id(1) == 0)
    def _():
        w_s[0] = wg_ref[0, 0].astype(BF16)
        w_s[1] = wu_ref[0, 0].astype(BF16)
        w_s[2] = wd_ref[0, 0].astype(BF16)

    x = x_ref[0]
    f = w_s.shape[2]
    acc = None
    for c0 in range(0, f, FFN_COLS):
        a = _dot(x, w_s[0, :, c0:c0 + FFN_COLS])
        u = _dot(x, w_s[1, :, c0:c0 + FFN_COLS])
        hmid = (a * _sigmoid(a) * u).astype(BF16)
        part = _dot(hmid, w_s[2, c0:c0 + FFN_COLS, :])
        acc = part if acc is None else acc + part
    y_ref[0] = (acc * gs_ref[0][:, 0:1]).astype(BF16)


def _ffn_call(xg, gs, w_gate, w_up, w_down, l):
    n_exp, rows, d = xg.shape
    f = w_gate.shape[-1]
    tm = rows // max(1, -(-rows // FFN_ROWS))
    assert rows % tm == 0 and tm % (2 * SUBLANES) == 0, (rows, tm)
    w_map = lambda e, r: (l, e, 0, 0)
    return pl.pallas_call(
        _ffn_kernel,
        grid=(n_exp, rows // tm),
        in_specs=[
            pl.BlockSpec((1, tm, d), lambda e, r: (e, r, 0)),
            pl.BlockSpec((1, tm, LANES), lambda e, r: (e, r, 0)),
            pl.BlockSpec((1, 1, d, f), w_map),
            pl.BlockSpec((1, 1, d, f), w_map),
            pl.BlockSpec((1, 1, f, d), w_map),
        ],
        out_specs=pl.BlockSpec((1, tm, d), lambda e, r: (e, r, 0)),
        out_shape=jax.ShapeDtypeStruct((n_exp, rows, d), BF16),
        scratch_shapes=[pltpu.VMEM((3, d, f), BF16)],
        compiler_params=_cparams(2),
        name="ffn",
    )(xg, gs, w_gate, w_up, w_down)


def _combine_kernel(starts_ref, y_ref, slot_ref, x_ref, g_ref, fg_ref, o_ref, *, final_norm, windowed):
    n_exp, _, cap, d = y_ref.shape
    tk = x_ref.shape[1]
    tile = min(tk, ROW_CHUNK)
    b = pl.program_id(0)
    tile0 = pl.program_id(1) * (tk // tile)
    n_tiles = pl.num_programs(1) * (tk // tile)

    for sub in range(tk // tile):
        rows = slice(sub * tile, (sub + 1) * tile)
        slots = slot_ref[0, rows]

        def finish(acc, rows=rows):
            out = x_ref[0, rows] + g_ref[0] * acc
            if final_norm:
                ms = jnp.mean(out * out, axis=-1, keepdims=True)
                out = out * lax.rsqrt(ms + NORM_EPS) * fg_ref[...]
            o_ref[0, rows] = out

        def dense(slots=slots, finish=finish):
            lane = lax.broadcasted_iota(jnp.int32, (tile, cap), 1).astype(F32)
            acc = jnp.zeros((tile, d), F32)
            for e in range(n_exp):
                onehot = jnp.where(lane == slots[:, e:e + 1], 1.0, 0.0).astype(BF16)
                acc = acc + _dot(onehot, y_ref[e, 0])
            finish(acc)

        if not windowed:
            dense()
            continue

        t_idx = tile0 + sub
        nxt = jnp.minimum(t_idx + 1, n_tiles - 1)
        win, misfit = [], jnp.int32(0)
        for e in range(n_exp):
            first = starts_ref[b, t_idx * n_exp + e]
            end = jnp.where(t_idx + 1 < n_tiles, starts_ref[b, nxt * n_exp + e], cap)
            w0 = jnp.minimum(first & -(2 * SUBLANES), cap - SLOT_WINDOW)
            misfit = misfit | jnp.where(end - w0 > SLOT_WINDOW, 1, 0)
            win.append(w0)

        def sparse(slots=slots, finish=finish, win=win):
            lane = lax.broadcasted_iota(jnp.int32, (tile, LANES), 1)
            low = lane < SLOT_WINDOW
            pos = (lane & (SLOT_WINDOW - 1)).astype(F32)
            per_dot = 2 * LANES // SLOT_WINDOW
            acc = jnp.zeros((tile, d), F32)
            for e0 in range(0, n_exp, per_dot):
                halves = []
                for e in range(e0, e0 + per_dot, 2):
                    rel_a = slots[:, e:e + 1] - win[e].astype(F32)
                    rel_b = slots[:, e + 1:e + 2] - win[e + 1].astype(F32)
                    hit = jnp.where(low, rel_a, rel_b) == pos
                    halves.append(jnp.where(hit, 1.0, 0.0).astype(BF16))
                yk = jnp.concatenate(
                    [y_ref[e, 0, pl.ds(pl.multiple_of(win[e], 2 * SUBLANES), SLOT_WINDOW), :]
                     for e in range(e0, e0 + per_dot)], axis=0)
                acc = acc + _dot(jnp.concatenate(halves, axis=1), yk)
            finish(acc)

        pl.when(misfit == 0)(sparse)
        pl.when(misfit != 0)(dense)


def _combine_call(y, slot, starts, x, gate, final_g, cap, row0, final_norm):
    bsz, t, d = x.shape
    n_exp = y.shape[0]
    tk = min(t, COMBINE_ROWS)
    assert row0 % cap == 0
    windowed = (cap >= SLOT_WINDOW and n_exp % (2 * LANES // SLOT_WINDOW) == 0
                and t % ROW_CHUNK == 0)
    grid_spec = pltpu.PrefetchScalarGridSpec(
        num_scalar_prefetch=1,
        grid=(bsz, t // tk),
        in_specs=[
            pl.BlockSpec((n_exp, 1, cap, d), lambda b, i, st: (0, b, row0 // cap, 0)),
            pl.BlockSpec((1, tk, LANES), lambda b, i, st: (b, i, 0)),
            pl.BlockSpec((1, tk, d), lambda b, i, st: (b, i, 0)),
            pl.BlockSpec((1, 1, d), lambda b, i, st: (b, 0, 0)),
            pl.BlockSpec((1, d), lambda b, i, st: (0, 0)),
        ],
        out_specs=pl.BlockSpec((1, tk, d), lambda b, i, st: (b, i, 0)),
    )
    return pl.pallas_call(
        functools.partial(_combine_kernel, final_norm=final_norm, windowed=windowed),
        grid_spec=grid_spec,
        out_shape=jax.ShapeDtypeStruct(x.shape, F32),
        compiler_params=_cparams(2),
        name="combine",
    )(starts, y, slot, x, gate, final_g)


def kernel(x, c, ctx, c_ctx, ada_w, ada_b, norm1_g, norm2_g, lru_w_in, lru_conv_w, lru_conv_b,
           lru_gate_w, lru_gate_b, lru_lambda, lru_w_out, attn_w_qkv, attn_sink, attn_w_o,
           moe_router, moe_w_gate, moe_w_up, moe_w_down, final_g):
    bsz, t, d = x.shape
    tc = ctx.shape[1]
    depth = ada_w.shape[0]
    n_exp = moe_router.shape[-1]
    cap_l = CAPACITY_FACTOR * t // n_exp
    cap_c = CAPACITY_FACTOR * tc // n_exp

    cond_rows = 2 * SUBLANES
    cond = jnp.zeros((cond_rows, d), F32).at[:bsz].set(c).at[bsz].set(c_ctx)
    mod = _ada_call(cond, ada_w, ada_b).reshape(depth, cond_rows, 6, 1, d)
    norm1 = norm1_g.reshape(depth, 1, d)
    norm2 = norm2_g.reshape(depth, 1, d)
    router_p = jnp.pad(moe_router, ((0, 0), (0, 0), (0, LANES - n_exp)))
    final_g2 = final_g.reshape(1, d)
    cos, sin_signed = _rope_tables(t)
    ones_c = jnp.ones((tc, HEAD_DIM), F32)

    for l in range(depth):
        need_ctx = l < depth - 1
        lat = [mod[l, :bsz, i] for i in range(6)]
        con = [jnp.broadcast_to(mod[l, bsz, i], (bsz, 1, d)) for i in range(6)]
        j = l // 2
        if l % 2 == 0:
            s_l, s_c = _lru_call(x, ctx, lat[0], lat[1], con[0], con[1], norm1, l,
                                 lru_w_in[j].astype(BF16), lru_conv_w[j], lru_conv_b[j],
                                 (0.5 * lru_gate_w[j]).astype(BF16), 0.5 * lru_gate_b[j],
                                 lru_lambda[j])
            w_out = lru_w_out[j].astype(BF16)
        else:
            w_qkv = attn_w_qkv[j].astype(BF16)
            q, k, v = _qkv_call(x, lat[0], lat[1], norm1, l, w_qkv, cos, sin_signed, True)
            qc, kc, vc = _qkv_call(ctx, con[0], con[1], norm1, l, w_qkv, ones_c, ones_c, False)
            s_l, s_c = _attn_call(attn_sink[j], q, k, v, qc, kc, vc, need_ctx)
            w_out = attn_w_o[j].astype(BF16)
        x = _resid_call(s_l, w_out, x, lat[2])
        if need_ctx:
            ctx = _resid_call(s_c, w_out, ctx, con[2])

        streams = [(x, lat[3], lat[4], cap_l)]
        if need_ctx:
            streams.append((ctx, con[3], con[4], cap_c))
        xg, gs, plans = _route_call(streams, norm2, l, router_p, n_exp)
        rows = xg.shape[2]
        y = _ffn_call(xg.reshape(n_exp, bsz * rows, d), gs.reshape(n_exp, bsz * rows, LANES),
                      moe_w_gate, moe_w_up, moe_w_down, l).reshape(xg.shape)
        x = _combine_call(y, *plans[0], x, lat[5], final_g2, cap_l, 0, not need_ctx)
        if need_ctx:
            ctx = _combine_call(y, *plans[1], ctx, con[5], final_g2, cap_c, cap_l, False)
    return x
```

```python
import functools
import math

import jax
import jax.numpy as jnp
from jax import lax
from jax.experimental import pallas as pl
from jax.experimental.pallas import tpu as pltpu

F32 = jnp.float32
BF16 = jnp.bfloat16

LANES = 128
SUBLANES = 8
VMEM_LIMIT_BYTES = 58 * 1024 * 1024

NORM_EPS = 1e-6
NEG_INF = -1e30
LRU_C = 8.0
LRU_BLOCK_W = 128
CONV_W = 4
CONV_HALO = 16
LRU_ROWS = 512
HEAD_DIM = 128
N_KV_HEADS = 2
QBLOCK = 128
WINDOW = 128
GRID_W = 64
ROPE_BASE = 10000.0
CAPACITY_FACTOR = 2

ROW_CHUNK = 256
PROJ_ROWS = 1024
ATTN_HEADS = 4
SCAN_UNROLL = 4
MOD_TILE = 1536
SEARCH_BITS = 3
FFN_ROWS = 1152
FFN_COLS = 256
COMBINE_ROWS = 1024
SLOT_WINDOW = LANES // 2


def _cparams(n_axes):
    return pltpu.CompilerParams(
        dimension_semantics=("arbitrary",) * n_axes, vmem_limit_bytes=VMEM_LIMIT_BYTES)


def _dot(a, b):
    return jnp.dot(a, b, preferred_element_type=F32)


def _dot_nt(a, b):
    return lax.dot_general(a, b, (((1,), (1,)), ((), ())), preferred_element_type=F32)


def _sigmoid(x):
    return 0.5 * jnp.tanh(0.5 * x) + 0.5


def _sqrt_nonneg(z):
    return jnp.where(z > 0.0, z * lax.rsqrt(z), 0.0)


def _norm_mod(x, g, shift, scale):
    ms = jnp.mean(x * x, axis=-1, keepdims=True)
    y = x * lax.rsqrt(ms + NORM_EPS) * g
    return y * (1.0 + scale) + shift


def _ada_kernel(c_ref, w_ref, b_ref, o_ref):
    c = c_ref[...]
    s = (c * _sigmoid(c)).astype(BF16)
    o_ref[0] = _dot(s, w_ref[0].astype(BF16)) + b_ref[0]


def _ada_call(cond, ada_w, ada_b):
    n_layers, d, n_out = ada_w.shape
    rows = cond.shape[0]
    return pl.pallas_call(
        _ada_kernel,
        grid=(n_layers, n_out // MOD_TILE),
        in_specs=[
            pl.BlockSpec((rows, d), lambda l, n: (0, 0)),
            pl.BlockSpec((1, d, MOD_TILE), lambda l, n: (l, 0, n)),
            pl.BlockSpec((1, 1, MOD_TILE), lambda l, n: (l, 0, n)),
        ],
        out_specs=pl.BlockSpec((1, rows, MOD_TILE), lambda l, n: (l, 0, n)),
        out_shape=jax.ShapeDtypeStruct((n_layers, rows, n_out), F32),
        compiler_params=_cparams(2),
        name="ada",
    )(cond, ada_w, ada_b.reshape(n_layers, 1, n_out))


def _resid_kernel(a_ref, w_ref, x_ref, g_ref, o_ref):
    o_ref[0] = x_ref[0] + g_ref[0] * _dot(a_ref[0], w_ref[...])


def _resid_call(act, w, x, gate):
    bsz, t, d = x.shape
    k = act.shape[-1]
    tm = min(t, PROJ_ROWS)
    return pl.pallas_call(
        _resid_kernel,
        grid=(bsz, t // tm),
        in_specs=[
            pl.BlockSpec((1, tm, k), lambda b, i: (b, i, 0)),
            pl.BlockSpec((k, d), lambda b, i: (0, 0)),
            pl.BlockSpec((1, tm, d), lambda b, i: (b, i, 0)),
            pl.BlockSpec((1, 1, d), lambda b, i: (b, 0, 0)),
        ],
        out_specs=pl.BlockSpec((1, tm, d), lambda b, i: (b, i, 0)),
        out_shape=jax.ShapeDtypeStruct(x.shape, F32),
        compiler_params=_cparams(2),
        name="resid",
    )(act, w, x, gate)


LRU_CB = 2 * LRU_BLOCK_W


def _log_sigmoid(x):
    return jnp.minimum(x, 0.0) - jnp.log1p(jnp.exp(-jnp.abs(x)))


def _conv_time(u, cw, cb):
    n = u.shape[0]
    left = (CONV_W - 1) // 2
    acc = cb + u * cw[left:left + 1]
    for k in range(CONV_W):
        off = k - left
        if off != 0:
            acc = acc + pltpu.roll(u, (-off) % n, 0) * cw[k:k + 1]
    return acc[CONV_HALO:n - CONV_HALO]


N_SEG = SUBLANES
SEG_SLACK = N_SEG * 2 * SUBLANES


def _segments(n):
    seg = n // N_SEG
    assert n % (N_SEG * SUBLANES) == 0, n
    pad = SUBLANES if (seg // SUBLANES) % 2 == 0 else 2 * SUBLANES
    return seg, seg + pad


def _lru_kernel(xl_ref, xc_ref, shl_ref, scl_ref, shc_ref, scc_ref, ng_ref, wy_ref, wx_ref,
                cw_ref, cb_ref, gw_ref, gb_ref, lam_ref, ol_ref, oc_ref,
                hl_s, hc_s, a_s, b_s, hs_s, ps_s, y_s, o_s):
    j = pl.program_id(1)
    t_lat = xl_ref.shape[1]
    t_ctx = xc_ref.shape[1]
    gbk = LRU_CB // LRU_BLOCK_W

    @pl.when(j == 0)
    def _():
        g = ng_ref[0]
        for x_ref, h_s, sh_ref, sc_ref in ((xl_ref, hl_s, shl_ref, scl_ref),
                                           (xc_ref, hc_s, shc_ref, scc_ref)):
            n = x_ref.shape[1]
            step = min(n, ROW_CHUNK)
            halo = jnp.zeros((CONV_HALO, h_s.shape[1]), BF16)
            h_s[0:CONV_HALO] = halo
            for r0 in range(0, n, step):
                h_s[CONV_HALO + r0:CONV_HALO + r0 + step] = _norm_mod(
                    x_ref[0, r0:r0 + step], g, sh_ref[0], sc_ref[0]).astype(BF16)
            h_s[CONV_HALO + n:2 * CONV_HALO + n] = halo

    half_rate = (0.5 * LRU_C / math.log(2.0)) * _log_sigmoid(lam_ref[...])

    def branches(h_s, n):
        seg, pitch = _segments(n)
        rows = min(n, LRU_ROWS)
        assert n % rows == 0 and rows % seg == 0, (n, rows, seg)
        for r0 in range(0, n, rows):
            branch_rows(h_s, r0, rows, seg, pitch)

    def branch_rows(h_s, r0, rows, seg, pitch):
        def put(store, val):
            for s in range(r0 // seg, (r0 + rows) // seg):
                store(slice(s * pitch, s * pitch + seg), val[s * seg - r0:(s + 1) * seg - r0])

        y = jax.nn.gelu(_dot(h_s[CONV_HALO + r0:CONV_HALO + r0 + rows], wy_ref[...]))
        xb = _conv_time(_dot(h_s[r0:r0 + rows + 2 * CONV_HALO], wx_ref[...]),
                        cw_ref[...], cb_ref[...])
        for kb in range(gbk):
            lanes = slice(kb * LRU_BLOCK_W, (kb + 1) * LRU_BLOCK_W)

            def put_y(rows, v, kb=kb):
                y_s[kb, rows] = v

            put(put_y, y[:, lanes])
            xk = xb[:, lanes]
            xk16 = xk.astype(BF16)
            half_x = 0.5 * xk
            for d in range(2):
                th = jnp.tanh(_dot(xk16, gw_ref[d, kb]) + gb_ref[d, kb])
                th_r = th[:, :LRU_BLOCK_W]
                th_i = th[:, LRU_BLOCK_W:]
                a = jnp.exp2(th_r * half_rate[d:d + 1, lanes] + half_rate[d:d + 1, lanes])

                def put_a(rows, v, d=d, kb=kb):
                    a_s[d, kb, rows] = v

                def put_b(rows, v, d=d, kb=kb):
                    b_s[d, kb, rows] = v

                put(put_a, a)
                put(put_b, _sqrt_nonneg(1.0 - a * a) * (th_i * half_x + half_x))

    def scan(n, carry_in, o_ref):
        seg, pitch = _segments(n)
        chains = [(d, kb) for d in range(2) for kb in range(gbk)]

        def advance(jstep, state):
            out = []
            for c, (d, kb) in enumerate(chains):
                jj = (seg - 1 - jstep) if d == 1 else jstep
                h, p = state[2 * c], state[2 * c + 1]
                a = a_s[d, kb, pl.ds(jj, N_SEG, stride=pitch), :]
                b = b_s[d, kb, pl.ds(jj, N_SEG, stride=pitch), :]
                h = a * h + b
                p = p * a
                rows = pl.ds(pl.multiple_of(jj * N_SEG, N_SEG), N_SEG)
                hs_s[d, kb, rows] = h
                ps_s[d, kb, rows] = p
                out += [h, p]
            return tuple(out)

        init = (jnp.zeros((N_SEG, LRU_BLOCK_W), F32), jnp.ones((N_SEG, LRU_BLOCK_W), F32))
        state = lax.fori_loop(0, seg, advance, init * len(chains), unroll=SCAN_UNROLL)

        row = lax.broadcasted_iota(jnp.int32, (N_SEG, LRU_BLOCK_W), 0)
        enter, carry_out = [], []
        for c, (d, kb) in enumerate(chains):
            h_end, p_end = state[2 * c], state[2 * c + 1]
            cur = carry_in[c]
            vec = jnp.zeros((N_SEG, LRU_BLOCK_W), F32)
            for s in (range(N_SEG - 1, -1, -1) if d == 1 else range(N_SEG)):
                vec = jnp.where(row == s, cur, vec)
                cur = h_end[s:s + 1] + p_end[s:s + 1] * cur
            enter.append(vec)
            carry_out.append(cur)

        def finish(jstep, carry):
            rows = pl.ds(pl.multiple_of(jstep * N_SEG, N_SEG), N_SEG)
            strided = pl.ds(jstep, N_SEG, stride=pitch)
            for kb in range(gbk):
                tot = None
                for c, (d, kb2) in enumerate(chains):
                    if kb2 == kb:
                        part = hs_s[d, kb, rows] + ps_s[d, kb, rows] * enter[c]
                        tot = part if tot is None else tot + part
                o_s[kb, strided, :] = tot * y_s[kb, strided, :]
            return carry

        lax.fori_loop(0, seg, finish, 0, unroll=SCAN_UNROLL)
        for kb in range(gbk):
            for s in range(N_SEG):
                o_ref[0, s * seg:(s + 1) * seg, kb * LRU_BLOCK_W:(kb + 1) * LRU_BLOCK_W] = (
                    o_s[kb, s * pitch:s * pitch + seg].astype(BF16))
        return carry_out

    branches(hc_s, t_ctx)
    carries = scan(t_ctx, [jnp.zeros((1, LRU_BLOCK_W), F32)] * (2 * gbk), oc_ref)
    branches(hl_s, t_lat)
    scan(t_lat, carries, ol_ref)


def _lru_call(x, ctx, sh_l, sc_l, sh_c, sc_c, norm_g, l, w_in, conv_w, conv_b, gate_w, gate_b, lam):
    bsz, t, d = x.shape
    tc = ctx.shape[1]
    width = w_in.shape[1] // 2
    nblk = width // LRU_CB
    gbk = LRU_CB // LRU_BLOCK_W
    vec = lambda b, c: (b, 0, 0)
    return pl.pallas_call(
        _lru_kernel,
        grid=(bsz, nblk),
        in_specs=[
            pl.BlockSpec((1, t, d), vec),
            pl.BlockSpec((1, tc, d), vec),
            pl.BlockSpec((1, 1, d), vec),
            pl.BlockSpec((1, 1, d), vec),
            pl.BlockSpec((1, 1, d), vec),
            pl.BlockSpec((1, 1, d), vec),
            pl.BlockSpec((1, 1, d), lambda b, c: (l, 0, 0)),
            pl.BlockSpec((d, LRU_CB), lambda b, c: (0, c)),
            pl.BlockSpec((d, LRU_CB), lambda b, c: (0, nblk + c)),
            pl.BlockSpec((CONV_W, LRU_CB), lambda b, c: (0, c)),
            pl.BlockSpec((1, LRU_CB), lambda b, c: (0, c)),
            pl.BlockSpec((2, gbk, LRU_BLOCK_W, 2 * LRU_BLOCK_W), lambda b, c: (0, c, 0, 0)),
            pl.BlockSpec((2, gbk, 1, 2 * LRU_BLOCK_W), lambda b, c: (0, c, 0, 0)),
            pl.BlockSpec((2, LRU_CB), lambda b, c: (0, c)),
        ],
        out_specs=[
            pl.BlockSpec((1, t, LRU_CB), lambda b, c: (b, 0, c)),
            pl.BlockSpec((1, tc, LRU_CB), lambda b, c: (b, 0, c)),
        ],
        out_shape=[
            jax.ShapeDtypeStruct((bsz, t, width), BF16),
            jax.ShapeDtypeStruct((bsz, tc, width), BF16),
        ],
        scratch_shapes=[
            pltpu.VMEM((t + 2 * CONV_HALO, d), BF16),
            pltpu.VMEM((tc + 2 * CONV_HALO, d), BF16),
            pltpu.VMEM((2, gbk, t + SEG_SLACK, LRU_BLOCK_W), F32),
            pltpu.VMEM((2, gbk, t + SEG_SLACK, LRU_BLOCK_W), F32),
            pltpu.VMEM((2, gbk, t, LRU_BLOCK_W), F32),
            pltpu.VMEM((2, gbk, t, LRU_BLOCK_W), F32),
            pltpu.VMEM((gbk, t + SEG_SLACK, LRU_BLOCK_W), F32),
            pltpu.VMEM((gbk, t + SEG_SLACK, LRU_BLOCK_W), F32),
        ],
        compiler_params=_cparams(2),
        name="lru",
    )(x, ctx, sh_l, sc_l, sh_c, sc_c, norm_g, w_in, w_in, conv_w, conv_b.reshape(1, width),
      gate_w, gate_b.reshape(2, -1, 1, 2 * LRU_BLOCK_W), lam)


def _rope(x, cos, sin_signed):
    lane = lax.broadcasted_iota(jnp.int32, x.shape, 1)
    quarter = HEAD_DIM // 4
    partner = jnp.where((lane & quarter) == 0,
                        pltpu.roll(x, HEAD_DIM - quarter, 1), pltpu.roll(x, quarter, 1))
    return x * cos + partner * sin_signed


def _qkv_kernel(x_ref, sh_ref, sc_ref, ng_ref, w_ref, cos_ref, sin_ref, q_ref, k_ref, vt_ref,
                *, n_heads, rotary):
    h = _norm_mod(x_ref[0], ng_ref[0], sh_ref[0], sc_ref[0]).astype(BF16)
    u = _dot(h, w_ref[...])
    scale = HEAD_DIM ** -0.5
    nq = n_heads * HEAD_DIM
    nk = N_KV_HEADS * HEAD_DIM
    for hd in range(n_heads + N_KV_HEADS):
        c = u[:, hd * HEAD_DIM:(hd + 1) * HEAD_DIM]
        if rotary:
            c = _rope(c, cos_ref[...], sin_ref[...])
        if hd < n_heads:
            q_ref[0, :, hd * HEAD_DIM:(hd + 1) * HEAD_DIM] = (c * scale).astype(BF16)
        else:
            kk = hd - n_heads
            k_ref[0, :, kk * HEAD_DIM:(kk + 1) * HEAD_DIM] = c.astype(BF16)
    vt_ref[0] = u[:, nq + nk:].T.astype(BF16)


def _qkv_call(x, sh, sc, norm_g, l, w_qkv, cos, sin_signed, rotary):
    bsz, t, d = x.shape
    nk = N_KV_HEADS * HEAD_DIM
    nq = w_qkv.shape[1] - 2 * nk
    tm = min(t, PROJ_ROWS)
    vec = lambda b, i: (b, 0, 0)
    return pl.pallas_call(
        functools.partial(_qkv_kernel, n_heads=nq // HEAD_DIM, rotary=rotary),
        grid=(bsz, t // tm),
        in_specs=[
            pl.BlockSpec((1, tm, d), lambda b, i: (b, i, 0)),
            pl.BlockSpec((1, 1, d), vec),
            pl.BlockSpec((1, 1, d), vec),
            pl.BlockSpec((1, 1, d), lambda b, i: (l, 0, 0)),
            pl.BlockSpec(w_qkv.shape, lambda b, i: (0, 0)),
            pl.BlockSpec((tm, HEAD_DIM), lambda b, i: (i, 0)),
            pl.BlockSpec((tm, HEAD_DIM), lambda b, i: (i, 0)),
        ],
        out_specs=[
            pl.BlockSpec((1, tm, nq), lambda b, i: (b, i, 0)),
            pl.BlockSpec((1, tm, nk), lambda b, i: (b, i, 0)),
            pl.BlockSpec((1, nk, tm), lambda b, i: (b, 0, i)),
        ],
        out_shape=[
            jax.ShapeDtypeStruct((bsz, t, nq), BF16),
            jax.ShapeDtypeStruct((bsz, t, nk), BF16),
            jax.ShapeDtypeStruct((bsz, nk, t), BF16),
        ],
        compiler_params=_cparams(2),
        name="qkv",
    )(x, sh, sc, norm_g, w_qkv, cos, sin_signed)


def _stack_heads(x, group):
    return jnp.concatenate([x[:, g * HEAD_DIM:(g + 1) * HEAD_DIM] for g in range(group)], axis=0)


def _attn_kernel(sink_ref, q_ref, k_ref, vt_ref, kc_ref, vct_ref, *rest, group, need_ctx):
    if need_ctx:
        qc_ref, o_ref, oc_ref = rest
    else:
        (o_ref,) = rest
    kh = pl.program_id(1)
    t = k_ref.shape[1]
    nblk = t // QBLOCK

    def sink_row(cols_per_head):
        col = lax.broadcasted_iota(jnp.int32, (1, group * cols_per_head), 1)
        row = jnp.zeros((1, group * cols_per_head), F32)
        for g in range(group):
            row = jnp.where(col >= g * cols_per_head, sink_ref[kh * group + g], row)
        return row

    kc = kc_ref[0]
    vct = vct_ref[0]
    sink_q = sink_row(QBLOCK)

    def softmax_vp(scores_values, sink):
        m = sink
        for sc, _ in scores_values:
            m = jnp.maximum(m, jnp.max(sc, axis=0, keepdims=True))
        den = jnp.exp(sink - m)
        acc = None
        for sc, val in scores_values:
            p = jnp.exp(sc - m)
            den = den + jnp.sum(p, axis=0, keepdims=True)
            vp = _dot(val, p.astype(BF16))
            acc = vp if acc is None else acc + vp
        return (acc / den).T

    def attend(r0, has_prev, has_next):
        lo = r0 - QBLOCK if has_prev else r0
        if not isinstance(lo, int):
            lo = pl.multiple_of(lo, QBLOCK)
        width = QBLOCK * (1 + has_prev + has_next)
        q_blk = q_ref[0, pl.ds(r0, QBLOCK), :]
        k_band = k_ref[0, pl.ds(lo, width), :]
        vt_band = vt_ref[0, :, pl.ds(lo, width)]
        sub = min(group, ATTN_HEADS)
        kj = lax.broadcasted_iota(jnp.int32, (QBLOCK, sub * QBLOCK), 0)
        qi = lax.broadcasted_iota(jnp.int32, (QBLOCK, sub * QBLOCK), 1) & (QBLOCK - 1)
        for h0 in range(0, group, sub):
            cols = slice(h0 * HEAD_DIM, (h0 + sub) * HEAD_DIM)
            qs = _stack_heads(q_blk[:, cols], sub)
            s = _dot_nt(k_band, qs)
            pieces = []
            if has_prev:
                pieces.append(jnp.where(QBLOCK + qi - kj <= WINDOW, s[:QBLOCK], NEG_INF))
            c0 = QBLOCK * has_prev
            pieces.append(s[c0:c0 + QBLOCK])
            if has_next:
                pieces.append(jnp.where(QBLOCK + kj - qi <= WINDOW, s[c0 + QBLOCK:], NEG_INF))
            s = jnp.concatenate(pieces, axis=0)
            o = softmax_vp([(s, vt_band), (_dot_nt(kc, qs), vct)], sink_q[:, cols])
            for g in range(sub):
                o_ref[0, pl.ds(r0, QBLOCK), (h0 + g) * HEAD_DIM:(h0 + g + 1) * HEAD_DIM] = (
                    o[g * QBLOCK:(g + 1) * QBLOCK].astype(BF16))

    attend(0, False, nblk > 1)
    if nblk > 1:
        attend((nblk - 1) * QBLOCK, True, False)

    def body(n, carry):
        attend(pl.multiple_of(n * QBLOCK, QBLOCK), True, True)
        return carry

    lax.fori_loop(1, nblk - 1, body, 0, unroll=7)

    if need_ctx:
        tc = qc_ref.shape[1]
        o = softmax_vp([(_dot_nt(kc, _stack_heads(qc_ref[0], group)), vct)], sink_row(tc))
        for g in range(group):
            oc_ref[0, :, g * HEAD_DIM:(g + 1) * HEAD_DIM] = o[g * tc:(g + 1) * tc].astype(BF16)


def _attn_call(sink, q, k, vt, qc, kc, vct, need_ctx):
    bsz, t, nq = q.shape
    tc = kc.shape[1]
    group = nq // HEAD_DIM // N_KV_HEADS
    gw = group * HEAD_DIM
    in_specs = [
        pl.BlockSpec(memory_space=pltpu.SMEM),
        pl.BlockSpec((1, t, gw), lambda b, h: (b, 0, h)),
        pl.BlockSpec((1, t, HEAD_DIM), lambda b, h: (b, 0, h)),
        pl.BlockSpec((1, HEAD_DIM, t), lambda b, h: (b, h, 0)),
        pl.BlockSpec((1, tc, HEAD_DIM), lambda b, h: (b, 0, h)),
        pl.BlockSpec((1, HEAD_DIM, tc), lambda b, h: (b, h, 0)),
    ]
    out_specs = [pl.BlockSpec((1, t, gw), lambda b, h: (b, 0, h))]
    out_shape = [jax.ShapeDtypeStruct((bsz, t, nq), BF16)]
    args = [sink, q, k, vt, kc, vct]
    if need_ctx:
        in_specs.append(pl.BlockSpec((1, tc, gw), lambda b, h: (b, 0, h)))
        out_specs.append(pl.BlockSpec((1, tc, gw), lambda b, h: (b, 0, h)))
        out_shape.append(jax.ShapeDtypeStruct((bsz, tc, nq), BF16))
        args.append(qc)
    outs = pl.pallas_call(
        functools.partial(_attn_kernel, group=group, need_ctx=need_ctx),
        grid=(bsz, N_KV_HEADS),
        in_specs=in_specs,
        out_specs=out_specs,
        out_shape=out_shape,
        compiler_params=_cparams(2),
        name="attn",
    )(*args)
    return (outs[0], outs[1]) if need_ctx else (outs[0], None)


def _rope_tables(t):
    half = HEAD_DIM // 2
    freqs = ROPE_BASE ** (-jnp.arange(0, half, 2, dtype=F32) / half)
    pos = jnp.arange(t)
    rows = (pos // GRID_W).astype(F32)[:, None] * freqs
    cols = (pos % GRID_W).astype(F32)[:, None] * freqs
    cos = jnp.concatenate([jnp.cos(rows), jnp.cos(rows), jnp.cos(cols), jnp.cos(cols)], axis=1)
    sin = jnp.concatenate([-jnp.sin(rows), jnp.sin(rows), -jnp.sin(cols), jnp.sin(cols)], axis=1)
    return cos, sin


def _count(mask):
    return jnp.sum(jnp.where(mask, 1.0, 0.0), axis=1, keepdims=True)


def _route_plan(x_ref, sh_ref, sc_ref, ng_ref, rw_ref, h_ref, slot_ref, starts_ref, slot_e_ref,
                gate_e_ref, aff_s, slotp_s, cap):
    t = x_ref.shape[1]
    n_exp = slot_e_ref.shape[1]
    chunk = min(t, ROW_CHUNK)
    rw = rw_ref[0]
    rw_hi = rw.astype(BF16)
    rw_lo = (rw - rw_hi.astype(F32)).astype(BF16)
    lane = lax.broadcasted_iota(jnp.int32, (chunk, LANES), 1)
    for r0 in range(0, t, chunk):
        h = _norm_mod(x_ref[0, r0:r0 + chunk], ng_ref[0], sh_ref[0], sc_ref[0])
        h_hi = h.astype(BF16)
        h_lo = (h - h_hi.astype(F32)).astype(BF16)
        h_ref[0, r0:r0 + chunk] = h_hi
        logits = _dot(h_hi, rw_hi) + (_dot(h_lo, rw_hi) + _dot(h_hi, rw_lo))
        logits = jnp.where(lane < n_exp, logits, NEG_INF)
        ex = jnp.exp(logits - jnp.max(logits, axis=1, keepdims=True))
        aff = ex / jnp.sum(ex, axis=1, keepdims=True)
        aff_s[:, r0:r0 + chunk] = aff.T
    aff_t = aff_s[0:n_exp, :]

    def as_f32(word):
        return lax.bitcast_convert_type(word, F32)

    def search(n_digits, accept):
        def body(i, w):
            shift = SEARCH_BITS * (n_digits - 1 - i)
            digit = jnp.zeros_like(w)
            for c in range(1, 1 << SEARCH_BITS):
                ok = accept(w | jnp.left_shift(jnp.int32(c), shift))
                digit = digit + jnp.where(ok, 1, 0)
            return w | jnp.left_shift(digit, shift)

        return lax.fori_loop(0, n_digits, body, jnp.zeros((n_exp, 1), jnp.int32))

    kth = search(30 // SEARCH_BITS, lambda w: _count(aff_t >= as_f32(w)) >= cap)
    above = aff_t >= as_f32(kth + 1)
    tie = (aff_t >= as_f32(kth)) & jnp.logical_not(above)
    need = cap - _count(above)
    idx = lax.broadcasted_iota(jnp.int32, aff_t.shape, 1)
    idx_digits = -(-max(1, (t - 1).bit_length()) // SEARCH_BITS)
    last = search(idx_digits, lambda w: _count(tie & (idx < w)) < need)
    sel = jnp.where(above | (tie & (idx <= last)), 1.0, 0.0)

    upper = jnp.where(lax.broadcasted_iota(jnp.int32, (chunk, chunk), 0)
                      < lax.broadcasted_iota(jnp.int32, (chunk, chunk), 1), 1.0, 0.0).astype(BF16)
    running = jnp.zeros((n_exp, 1), F32)
    starts = jnp.zeros((n_exp, LANES), F32)
    chunk_id = lax.broadcasted_iota(jnp.int32, (n_exp, LANES), 1)
    slotp_s[...] = jnp.full(slotp_s.shape, -1.0, F32)
    for r0 in range(0, t, chunk):
        starts = jnp.where(chunk_id == r0 // chunk, running, starts)
        sel_c = sel[:, r0:r0 + chunk]
        pos = _dot(sel_c.astype(BF16), upper) + running
        running = running + jnp.sum(sel_c, axis=1, keepdims=True)
        slotp_s[0:n_exp, r0:r0 + chunk] = jnp.where(sel_c > 0.5, pos, -1.0)
    starts_ref[0] = starts
    slot_e_ref[0] = slotp_s[0:n_exp, :]
    gate_e_ref[0] = sel * aff_t
    for r0 in range(0, t, chunk):
        slot_ref[0, r0:r0 + chunk] = slotp_s[:, r0:r0 + chunk].T


N_PLAN_OUTS = 5


def _plan_kernel(*refs, caps):
    ns = len(caps)
    ng_ref, rw_ref = refs[3 * ns:3 * ns + 2]
    outs0 = 3 * ns + 2
    scr0 = outs0 + N_PLAN_OUTS * ns
    for s in range(ns):
        _route_plan(*refs[3 * s:3 * s + 3], ng_ref, rw_ref,
                    *refs[outs0 + N_PLAN_OUTS * s:outs0 + N_PLAN_OUTS * (s + 1)],
                    *refs[scr0 + 2 * s:scr0 + 2 * s + 2], caps[s])


def _plan_call(streams, norm_g, l, router_p, n_exp):
    bsz, _, d = streams[0][0].shape
    vec = lambda b: (b, 0, 0)
    in_specs, args, out_specs, out_shapes, scratch = [], [], [], [], []
    for x, sh, sc, _ in streams:
        t = x.shape[1]
        in_specs += [pl.BlockSpec((1, t, d), vec), pl.BlockSpec((1, 1, d), vec),
                     pl.BlockSpec((1, 1, d), vec)]
        args += [x, sh, sc]
        for shape, dtype in (((t, d), BF16), ((t, LANES), F32), ((n_exp, LANES), F32),
                             ((n_exp, t), F32), ((n_exp, t), F32)):
            out_specs.append(pl.BlockSpec((1,) + shape, vec))
            out_shapes.append(jax.ShapeDtypeStruct((bsz,) + shape, dtype))
        scratch += [pltpu.VMEM((LANES, t), F32), pltpu.VMEM((LANES, t), F32)]
    in_specs += [pl.BlockSpec((1, 1, d), lambda b: (l, 0, 0)),
                 pl.BlockSpec((1, d, LANES), lambda b: (l, 0, 0))]
    outs = pl.pallas_call(
        functools.partial(_plan_kernel, caps=tuple(st[3] for st in streams)),
        grid=(bsz,),
        in_specs=in_specs,
        out_specs=out_specs,
        out_shape=out_shapes,
        scratch_shapes=scratch,
        compiler_params=_cparams(1),
        name="plan",
    )(*args, norm_g, router_p)
    plans = []
    for s, (x, _, _, _) in enumerate(streams):
        h, slot, starts, slot_e, gate_e = outs[N_PLAN_OUTS * s:N_PLAN_OUTS * (s + 1)]
        n_chunks = -(-x.shape[1] // ROW_CHUNK)
        starts = jnp.swapaxes(starts[:, :, :n_chunks].astype(jnp.int32), 1, 2)
        plans.append((h, slot, starts.reshape(bsz, n_chunks * n_exp), slot_e, gate_e))
    return plans


def _gather_stream(b, starts_ref, h_ref, slot_e_ref, gate_e_ref, xg_ref, gs_ref, row0, cap):
    n_exp, t = slot_e_ref.shape[1:]
    d = h_ref.shape[2]
    chunk = min(t, ROW_CHUNK)
    n_chunks = t // chunk
    win = min(cap, SLOT_WINDOW)
    rows_all = slice(row0, row0 + cap)
    for e in range(n_exp):
        xg_ref[e, 0, rows_all] = jnp.zeros((cap, d), BF16)
        gs_ref[e, 0, rows_all] = jnp.zeros((cap, LANES), F32)

    def scatter_rows(e, rows, n_rows, tokens, gates):
        cur = xg_ref[e, 0, rows].astype(F32)
        xg_ref[e, 0, rows] = (cur + tokens).astype(BF16)
        gs_ref[e, 0, rows] = gs_ref[e, 0, rows] + jnp.broadcast_to(gates, (n_rows, LANES))

    for c in range(n_chunks):
        toks = slice(c * chunk, (c + 1) * chunk)
        h_c = h_ref[0, toks]
        if cap > win:
            nxt = min(c + 1, n_chunks - 1)
            starts, misfit = [], jnp.int32(0)
            for e in range(n_exp):
                first = starts_ref[b, c * n_exp + e]
                end = starts_ref[b, nxt * n_exp + e] if c + 1 < n_chunks else cap
                w0 = jnp.minimum(first & -(2 * SUBLANES), cap - win)
                misfit = misfit | jnp.where(end - w0 > win, 1, 0)
                starts.append(w0)
        else:
            starts, misfit = [0] * n_exp, None

        def windowed(toks=toks, h_c=h_c, starts=starts):
            slot_id = lax.broadcasted_iota(jnp.int32, (win, chunk), 0).astype(F32)
            hits = [slot_e_ref[0, e:e + 1, toks] - jnp.float32(starts[e]) == slot_id
                    for e in range(n_exp)]
            onehot = jnp.concatenate([jnp.where(hit, 1.0, 0.0).astype(BF16) for hit in hits], axis=0)
            picked = _dot(onehot, h_c)
            for e in range(n_exp):
                w0 = starts[e] if isinstance(starts[e], int) else pl.multiple_of(starts[e], 2 * SUBLANES)
                gates = jnp.sum(jnp.where(hits[e], gate_e_ref[0, e:e + 1, toks], 0.0),
                                axis=1, keepdims=True)
                scatter_rows(e, pl.ds(row0 + w0, win), win, picked[e * win:(e + 1) * win], gates)

        def dense(toks=toks, h_c=h_c):
            slot_id = lax.broadcasted_iota(jnp.int32, (cap, chunk), 0).astype(F32)
            for e in range(n_exp):
                hit = slot_e_ref[0, e:e + 1, toks] == slot_id
                gates = jnp.sum(jnp.where(hit, gate_e_ref[0, e:e + 1, toks], 0.0),
                                axis=1, keepdims=True)
                scatter_rows(e, rows_all, cap, _dot(jnp.where(hit, 1.0, 0.0).astype(BF16), h_c), gates)

        if misfit is None:
            windowed()
        else:
            pl.when(misfit == 0)(windowed)
            pl.when(misfit != 0)(dense)


def _gather_kernel(*refs, caps):
    ns = len(caps)
    xg_ref, gs_ref = refs[4 * ns:4 * ns + 2]
    b = pl.program_id(0)
    row0 = 0
    for s in range(ns):
        _gather_stream(b, refs[s], *refs[ns + 3 * s:ns + 3 * s + 3], xg_ref, gs_ref, row0, caps[s])
        row0 += caps[s]


def _gather_call(plans, caps, n_exp):
    bsz, _, d = plans[0][0].shape
    rows = sum(caps)
    vec = lambda b, *_: (b, 0, 0)
    in_specs, args = [], []
    for h, _, _, slot_e, gate_e in plans:
        t = h.shape[1]
        in_specs += [pl.BlockSpec((1, t, d), vec), pl.BlockSpec((1, n_exp, t), vec),
                     pl.BlockSpec((1, n_exp, t), vec)]
        args += [h, slot_e, gate_e]
    grid_spec = pltpu.PrefetchScalarGridSpec(
        num_scalar_prefetch=len(plans),
        grid=(bsz,),
        in_specs=in_specs,
        out_specs=[
            pl.BlockSpec((n_exp, 1, rows, d), lambda b, *_: (0, b, 0, 0)),
            pl.BlockSpec((n_exp, 1, rows, LANES), lambda b, *_: (0, b, 0, 0)),
        ],
    )
    return pl.pallas_call(
        functools.partial(_gather_kernel, caps=caps),
        grid_spec=grid_spec,
        out_shape=[
            jax.ShapeDtypeStruct((n_exp, bsz, rows, d), BF16),
            jax.ShapeDtypeStruct((n_exp, bsz, rows, LANES), F32),
        ],
        compiler_params=_cparams(1),
        name="gather",
    )(*[p[2] for p in plans], *args)


def _ffn_kernel(x_ref, gs_ref, wg_ref, wu_ref, wd_ref, y_ref, w_s):
    @pl.when(pl.program_id(1) == 0)
    def _():
        w_s[0] = wg_ref[0, 0].astype(BF16)
        w_s[1] = wu_ref[0, 0].astype(BF16)
        w_s[2] = wd_ref[0, 0].astype(BF16)

    x = x_ref[0]
    f = w_s.shape[2]
    acc = None
    for c0 in range(0, f, FFN_COLS):
        a = _dot(x, w_s[0, :, c0:c0 + FFN_COLS])
        u = _dot(x, w_s[1, :, c0:c0 + FFN_COLS])
        hmid = (a * _sigmoid(a) * u).astype(BF16)
        part = _dot(hmid, w_s[2, c0:c0 + FFN_COLS, :])
        acc = part if acc is None else acc + part
    y_ref[0] = (acc * gs_ref[0][:, 0:1]).astype(BF16)


def _ffn_call(xg, gs, w_gate, w_up, w_down, l):
    n_exp, rows, d = xg.shape
    f = w_gate.shape[-1]
    tm = rows // max(1, -(-rows // FFN_ROWS))
    assert rows % tm == 0 and tm % (2 * SUBLANES) == 0, (rows, tm)
    w_map = lambda e, r: (l, e, 0, 0)
    return pl.pallas_call(
        _ffn_kernel,
        grid=(n_exp, rows // tm),
        in_specs=[
            pl.BlockSpec((1, tm, d), lambda e, r: (e, r, 0)),
            pl.BlockSpec((1, tm, LANES), lambda e, r: (e, r, 0)),
            pl.BlockSpec((1, 1, d, f), w_map),
            pl.BlockSpec((1, 1, d, f), w_map),
            pl.BlockSpec((1, 1, f, d), w_map),
        ],
        out_specs=pl.BlockSpec((1, tm, d), lambda e, r: (e, r, 0)),
        out_shape=jax.ShapeDtypeStruct((n_exp, rows, d), BF16),
        scratch_shapes=[pltpu.VMEM((3, d, f), BF16)],
        compiler_params=_cparams(2),
        name="ffn",
    )(xg, gs, w_gate, w_up, w_down)


def _combine_kernel(starts_ref, y_ref, slot_ref, x_ref, g_ref, fg_ref, o_ref, *, final_norm, windowed):
    n_exp, _, cap, d = y_ref.shape
    tk = x_ref.shape[1]
    tile = min(tk, ROW_CHUNK)
    b = pl.program_id(0)
    tile0 = pl.program_id(1) * (tk // tile)
    n_tiles = pl.num_programs(1) * (tk // tile)

    for sub in range(tk // tile):
        rows = slice(sub * tile, (sub + 1) * tile)
        slots = slot_ref[0, rows]

        def finish(acc, rows=rows):
            out = x_ref[0, rows] + g_ref[0] * acc
            if final_norm:
                ms = jnp.mean(out * out, axis=-1, keepdims=True)
                out = out * lax.rsqrt(ms + NORM_EPS) * fg_ref[...]
            o_ref[0, rows] = out

        def dense(slots=slots, finish=finish):
            lane = lax.broadcasted_iota(jnp.int32, (tile, cap), 1).astype(F32)
            acc = jnp.zeros((tile, d), F32)
            for e in range(n_exp):
                onehot = jnp.where(lane == slots[:, e:e + 1], 1.0, 0.0).astype(BF16)
                acc = acc + _dot(onehot, y_ref[e, 0])
            finish(acc)

        if not windowed:
            dense()
            continue

        t_idx = tile0 + sub
        nxt = jnp.minimum(t_idx + 1, n_tiles - 1)
        win, misfit = [], jnp.int32(0)
        for e in range(n_exp):
            first = starts_ref[b, t_idx * n_exp + e]
            end = jnp.where(t_idx + 1 < n_tiles, starts_ref[b, nxt * n_exp + e], cap)
            w0 = jnp.minimum(first & -(2 * SUBLANES), cap - SLOT_WINDOW)
            misfit = misfit | jnp.where(end - w0 > SLOT_WINDOW, 1, 0)
            win.append(w0)

        def sparse(slots=slots, finish=finish, win=win):
            lane = lax.broadcasted_iota(jnp.int32, (tile, LANES), 1)
            low = lane < SLOT_WINDOW
            pos = (lane & (SLOT_WINDOW - 1)).astype(F32)
            per_dot = 2 * LANES // SLOT_WINDOW
            acc = jnp.zeros((tile, d), F32)
            for e0 in range(0, n_exp, per_dot):
                halves = []
                for e in range(e0, e0 + per_dot, 2):
                    rel_a = slots[:, e:e + 1] - win[e].astype(F32)
                    rel_b = slots[:, e + 1:e + 2] - win[e + 1].astype(F32)
                    hit = jnp.where(low, rel_a, rel_b) == pos
                    halves.append(jnp.where(hit, 1.0, 0.0).astype(BF16))
                yk = jnp.concatenate(
                    [y_ref[e, 0, pl.ds(pl.multiple_of(win[e], 2 * SUBLANES), SLOT_WINDOW), :]
                     for e in range(e0, e0 + per_dot)], axis=0)
                acc = acc + _dot(jnp.concatenate(halves, axis=1), yk)
            finish(acc)

        pl.when(misfit == 0)(sparse)
        pl.when(misfit != 0)(dense)


def _combine_call(y, slot, starts, x, gate, final_g, cap, row0, final_norm):
    bsz, t, d = x.shape
    n_exp = y.shape[0]
    tk = min(t, COMBINE_ROWS)
    assert row0 % cap == 0
    windowed = (cap >= SLOT_WINDOW and n_exp % (2 * LANES // SLOT_WINDOW) == 0
                and t % ROW_CHUNK == 0)
    grid_spec = pltpu.PrefetchScalarGridSpec(
        num_scalar_prefetch=1,
        grid=(bsz, t // tk),
        in_specs=[
            pl.BlockSpec((n_exp, 1, cap, d), lambda b, i, st: (0, b, row0 // cap, 0)),
            pl.BlockSpec((1, tk, LANES), lambda b, i, st: (b, i, 0)),
            pl.BlockSpec((1, tk, d), lambda b, i, st: (b, i, 0)),
            pl.BlockSpec((1, 1, d), lambda b, i, st: (b, 0, 0)),
            pl.BlockSpec((1, d), lambda b, i, st: (0, 0)),
        ],
        out_specs=pl.BlockSpec((1, tk, d), lambda b, i, st: (b, i, 0)),
    )
    return pl.pallas_call(
        functools.partial(_combine_kernel, final_norm=final_norm, windowed=windowed),
        grid_spec=grid_spec,
        out_shape=jax.ShapeDtypeStruct(x.shape, F32),
        compiler_params=_cparams(2),
        name="combine",
    )(starts, y, slot, x, gate, final_g)


def kernel(x, c, ctx, c_ctx, ada_w, ada_b, norm1_g, norm2_g, lru_w_in, lru_conv_w, lru_conv_b,
           lru_gate_w, lru_gate_b, lru_lambda, lru_w_out, attn_w_qkv, attn_sink, attn_w_o,
           moe_router, moe_w_gate, moe_w_up, moe_w_down, final_g):
    bsz, t, d = x.shape
    tc = ctx.shape[1]
    depth = ada_w.shape[0]
    n_exp = moe_router.shape[-1]
    cap_l = CAPACITY_FACTOR * t // n_exp
    cap_c = CAPACITY_FACTOR * tc // n_exp

    cond_rows = 2 * SUBLANES
    cond = jnp.zeros((cond_rows, d), F32).at[:bsz].set(c).at[bsz].set(c_ctx)
    mod = _ada_call(cond, ada_w, ada_b).reshape(depth, cond_rows, 6, 1, d)
    norm1 = norm1_g.reshape(depth, 1, d)
    norm2 = norm2_g.reshape(depth, 1, d)
    router_p = jnp.pad(moe_router, ((0, 0), (0, 0), (0, LANES - n_exp)))
    final_g2 = final_g.reshape(1, d)
    cos, sin_signed = _rope_tables(t)
    ones_c = jnp.ones((tc, HEAD_DIM), F32)

    for l in range(depth):
        need_ctx = l < depth - 1
        lat = [mod[l, :bsz, i] for i in range(6)]
        con = [jnp.broadcast_to(mod[l, bsz, i], (bsz, 1, d)) for i in range(6)]
        j = l // 2
        if l % 2 == 0:
            s_l, s_c = _lru_call(x, ctx, lat[0], lat[1], con[0], con[1], norm1, l,
                                 lru_w_in[j].astype(BF16), lru_conv_w[j], lru_conv_b[j],
                                 (0.5 * lru_gate_w[j]).astype(BF16), 0.5 * lru_gate_b[j],
                                 lru_lambda[j])
            w_out = lru_w_out[j].astype(BF16)
        else:
            w_qkv = attn_w_qkv[j].astype(BF16)
            q, k, v = _qkv_call(x, lat[0], lat[1], norm1, l, w_qkv, cos, sin_signed, True)
            qc, kc, vc = _qkv_call(ctx, con[0], con[1], norm1, l, w_qkv, ones_c, ones_c, False)
            s_l, s_c = _attn_call(attn_sink[j], q, k, v, qc, kc, vc, need_ctx)
            w_out = attn_w_o[j].astype(BF16)
        x = _resid_call(s_l, w_out, x, lat[2])
        if need_ctx:
            ctx = _resid_call(s_c, w_out, ctx, con[2])

        streams = [(x, lat[3], lat[4], cap_l)]
        if need_ctx:
            streams.append((ctx, con[3], con[4], cap_c))
        plans = _plan_call(streams, norm2, l, router_p, n_exp)
        xg, gs = _gather_call(plans, tuple(st[3] for st in streams), n_exp)
        rows = xg.shape[2]
        y = _ffn_call(xg.reshape(n_exp, bsz * rows, d), gs.reshape(n_exp, bsz * rows, LANES),
                      moe_w_gate, moe_w_up, moe_w_down, l).reshape(xg.shape)
        x = _combine_call(y, plans[0][1], plans[0][2], x, lat[5], final_g2, cap_l, 0, not need_ctx)
        if need_ctx:
            ctx = _combine_call(y, plans[1][1], plans[1][2], ctx, con[5], final_g2, cap_c, cap_l,
                                False)
    return x
```

```python
import functools
import math

import jax
import jax.numpy as jnp
from jax import lax
from jax.experimental import pallas as pl
from jax.experimental.pallas import tpu as pltpu

F32 = jnp.float32
BF16 = jnp.bfloat16

LANES = 128
SUBLANES = 8
VMEM_LIMIT_BYTES = 58 * 1024 * 1024

NORM_EPS = 1e-6
NEG_INF = -1e30
LRU_C = 8.0
LRU_BLOCK_W = 128
CONV_W = 4
CONV_HALO = 16
LRU_ROWS = 256
HEAD_DIM = 128
N_KV_HEADS = 2
QBLOCK = 128
WINDOW = 128
GRID_W = 64
ROPE_BASE = 10000.0
CAPACITY_FACTOR = 2

ROW_CHUNK = 256
PROJ_ROWS = 1024
ATTN_HEADS = 4
SCAN_UNROLL = 8
MOD_TILE = 1536
SEARCH_BITS = 3
FFN_ROWS = 1152
FFN_COLS = 256
COMBINE_ROWS = 1024
SLOT_WINDOW = LANES // 2


def _cparams(n_axes):
    return pltpu.CompilerParams(
        dimension_semantics=("arbitrary",) * n_axes, vmem_limit_bytes=VMEM_LIMIT_BYTES)


def _dot(a, b):
    return jnp.dot(a, b, preferred_element_type=F32)


def _dot_nt(a, b):
    return lax.dot_general(a, b, (((1,), (1,)), ((), ())), preferred_element_type=F32)


def _sigmoid(x):
    return 0.5 * jnp.tanh(0.5 * x) + 0.5


def _sqrt_nonneg(z):
    return jnp.where(z > 0.0, z * lax.rsqrt(z), 0.0)


def _norm_mod(x, g, shift, scale):
    ms = jnp.mean(x * x, axis=-1, keepdims=True)
    y = x * lax.rsqrt(ms + NORM_EPS) * g
    return y * (1.0 + scale) + shift


def _ada_kernel(c_ref, w_ref, b_ref, o_ref):
    c = c_ref[...]
    s = (c * _sigmoid(c)).astype(BF16)
    o_ref[0] = _dot(s, w_ref[0].astype(BF16)) + b_ref[0]


def _ada_call(cond, ada_w, ada_b):
    n_layers, d, n_out = ada_w.shape
    rows = cond.shape[0]
    return pl.pallas_call(
        _ada_kernel,
        grid=(n_layers, n_out // MOD_TILE),
        in_specs=[
            pl.BlockSpec((rows, d), lambda l, n: (0, 0)),
            pl.BlockSpec((1, d, MOD_TILE), lambda l, n: (l, 0, n)),
            pl.BlockSpec((1, 1, MOD_TILE), lambda l, n: (l, 0, n)),
        ],
        out_specs=pl.BlockSpec((1, rows, MOD_TILE), lambda l, n: (l, 0, n)),
        out_shape=jax.ShapeDtypeStruct((n_layers, rows, n_out), F32),
        compiler_params=_cparams(2),
        name="ada",
    )(cond, ada_w, ada_b.reshape(n_layers, 1, n_out))


def _resid_kernel(a_ref, w_ref, x_ref, g_ref, o_ref):
    o_ref[0] = x_ref[0] + g_ref[0] * _dot(a_ref[0], w_ref[...])


def _resid_call(act, w, x, gate):
    bsz, t, d = x.shape
    k = act.shape[-1]
    tm = min(t, PROJ_ROWS)
    return pl.pallas_call(
        _resid_kernel,
        grid=(bsz, t // tm),
        in_specs=[
            pl.BlockSpec((1, tm, k), lambda b, i: (b, i, 0)),
            pl.BlockSpec((k, d), lambda b, i: (0, 0)),
            pl.BlockSpec((1, tm, d), lambda b, i: (b, i, 0)),
            pl.BlockSpec((1, 1, d), lambda b, i: (b, 0, 0)),
        ],
        out_specs=pl.BlockSpec((1, tm, d), lambda b, i: (b, i, 0)),
        out_shape=jax.ShapeDtypeStruct(x.shape, F32),
        compiler_params=_cparams(2),
        name="resid",
    )(act, w, x, gate)


LRU_CB = 2 * LRU_BLOCK_W


def _log_sigmoid(x):
    return jnp.minimum(x, 0.0) - jnp.log1p(jnp.exp(-jnp.abs(x)))


def _conv_time(u, cw, cb):
    n = u.shape[0]
    left = (CONV_W - 1) // 2
    acc = cb + u * cw[left:left + 1]
    for k in range(CONV_W):
        off = k - left
        if off != 0:
            acc = acc + pltpu.roll(u, (-off) % n, 0) * cw[k:k + 1]
    return acc[CONV_HALO:n - CONV_HALO]


N_SEG = SUBLANES
SEG_SLACK = N_SEG * 2 * SUBLANES


def _segments(n):
    seg = n // N_SEG
    assert n % (N_SEG * SUBLANES) == 0, n
    pad = SUBLANES if (seg // SUBLANES) % 2 == 0 else 2 * SUBLANES
    return seg, seg + pad


def _lru_kernel(xl_ref, xc_ref, shl_ref, scl_ref, shc_ref, scc_ref, ng_ref, wy_ref, wx_ref,
                cw_ref, cb_ref, gw_ref, gb_ref, lam_ref, ol_ref, oc_ref,
                hl_s, hc_s, a_s, b_s, hs_s, ps_s, y_s, o_s):
    j = pl.program_id(1)
    t_lat = xl_ref.shape[1]
    t_ctx = xc_ref.shape[1]
    gbk = LRU_CB // LRU_BLOCK_W

    @pl.when(j == 0)
    def _():
        g = ng_ref[0]
        for x_ref, h_s, sh_ref, sc_ref in ((xl_ref, hl_s, shl_ref, scl_ref),
                                           (xc_ref, hc_s, shc_ref, scc_ref)):
            n = x_ref.shape[1]
            step = min(n, ROW_CHUNK)
            halo = jnp.zeros((CONV_HALO, h_s.shape[1]), BF16)
            h_s[0:CONV_HALO] = halo
            for r0 in range(0, n, step):
                h_s[CONV_HALO + r0:CONV_HALO + r0 + step] = _norm_mod(
                    x_ref[0, r0:r0 + step], g, sh_ref[0], sc_ref[0]).astype(BF16)
            h_s[CONV_HALO + n:2 * CONV_HALO + n] = halo

    half_rate = (0.5 * LRU_C / math.log(2.0)) * _log_sigmoid(lam_ref[...])

    def branches(h_s, n):
        seg, pitch = _segments(n)
        rows = min(n, LRU_ROWS)
        assert n % rows == 0 and rows % seg == 0, (n, rows, seg)
        for r0 in range(0, n, rows):
            branch_rows(h_s, r0, rows, seg, pitch)

    def branch_rows(h_s, r0, rows, seg, pitch):
        def put(store, val):
            for s in range(r0 // seg, (r0 + rows) // seg):
                store(slice(s * pitch, s * pitch + seg), val[s * seg - r0:(s + 1) * seg - r0])

        y = jax.nn.gelu(_dot(h_s[CONV_HALO + r0:CONV_HALO + r0 + rows], wy_ref[...]))
        xb = _conv_time(_dot(h_s[r0:r0 + rows + 2 * CONV_HALO], wx_ref[...]),
                        cw_ref[...], cb_ref[...])
        for kb in range(gbk):
            lanes = slice(kb * LRU_BLOCK_W, (kb + 1) * LRU_BLOCK_W)

            def put_y(rows, v, kb=kb):
                y_s[kb, rows] = v

            put(put_y, y[:, lanes])
            xk = xb[:, lanes]
            xk16 = xk.astype(BF16)
            half_x = 0.5 * xk
            for d in range(2):
                th = jnp.tanh(_dot(xk16, gw_ref[d, kb]) + gb_ref[d, kb])
                th_r = th[:, :LRU_BLOCK_W]
                th_i = th[:, LRU_BLOCK_W:]
                a = jnp.exp2(th_r * half_rate[d:d + 1, lanes] + half_rate[d:d + 1, lanes])

                def put_a(rows, v, d=d, kb=kb):
                    a_s[d, kb, rows] = v

                def put_b(rows, v, d=d, kb=kb):
                    b_s[d, kb, rows] = v

                put(put_a, a)
                put(put_b, _sqrt_nonneg(1.0 - a * a) * (th_i * half_x + half_x))

    def scan(n, carry_in, o_ref):
        seg, pitch = _segments(n)
        chains = [(d, kb) for d in range(2) for kb in range(gbk)]

        def advance(jstep, state):
            out = []
            for c, (d, kb) in enumerate(chains):
                jj = (seg - 1 - jstep) if d == 1 else jstep
                h, p = state[2 * c], state[2 * c + 1]
                a = a_s[d, kb, pl.ds(jj, N_SEG, stride=pitch), :]
                b = b_s[d, kb, pl.ds(jj, N_SEG, stride=pitch), :]
                h = a * h + b
                p = p * a
                rows = pl.ds(pl.multiple_of(jj * N_SEG, N_SEG), N_SEG)
                hs_s[d, kb, rows] = h
                ps_s[d, kb, rows] = p
                out += [h, p]
            return tuple(out)

        init = (jnp.zeros((N_SEG, LRU_BLOCK_W), F32), jnp.ones((N_SEG, LRU_BLOCK_W), F32))
        state = lax.fori_loop(0, seg, advance, init * len(chains), unroll=SCAN_UNROLL)

        row = lax.broadcasted_iota(jnp.int32, (N_SEG, LRU_BLOCK_W), 0)
        enter, carry_out = [], []
        for c, (d, kb) in enumerate(chains):
            h_end, p_end = state[2 * c], state[2 * c + 1]
            cur = carry_in[c]
            vec = jnp.zeros((N_SEG, LRU_BLOCK_W), F32)
            for s in (range(N_SEG - 1, -1, -1) if d == 1 else range(N_SEG)):
                vec = jnp.where(row == s, cur, vec)
                cur = h_end[s:s + 1] + p_end[s:s + 1] * cur
            enter.append(vec)
            carry_out.append(cur)

        def finish(jstep, carry):
            rows = pl.ds(pl.multiple_of(jstep * N_SEG, N_SEG), N_SEG)
            strided = pl.ds(jstep, N_SEG, stride=pitch)
            for kb in range(gbk):
                tot = None
                for c, (d, kb2) in enumerate(chains):
                    if kb2 == kb:
                        part = hs_s[d, kb, rows] + ps_s[d, kb, rows] * enter[c]
                        tot = part if tot is None else tot + part
                o_s[kb, strided, :] = tot * y_s[kb, strided, :]
            return carry

        lax.fori_loop(0, seg, finish, 0, unroll=SCAN_UNROLL)
        for kb in range(gbk):
            for s in range(N_SEG):
                o_ref[0, s * seg:(s + 1) * seg, kb * LRU_BLOCK_W:(kb + 1) * LRU_BLOCK_W] = (
                    o_s[kb, s * pitch:s * pitch + seg].astype(BF16))
        return carry_out

    branches(hc_s, t_ctx)
    carries = scan(t_ctx, [jnp.zeros((1, LRU_BLOCK_W), F32)] * (2 * gbk), oc_ref)
    branches(hl_s, t_lat)
    scan(t_lat, carries, ol_ref)


def _lru_call(x, ctx, sh_l, sc_l, sh_c, sc_c, norm_g, l, w_in, conv_w, conv_b, gate_w, gate_b, lam):
    bsz, t, d = x.shape
    tc = ctx.shape[1]
    width = w_in.shape[1] // 2
    nblk = width // LRU_CB
    gbk = LRU_CB // LRU_BLOCK_W
    vec = lambda b, c: (b, 0, 0)
    return pl.pallas_call(
        _lru_kernel,
        grid=(bsz, nblk),
        in_specs=[
            pl.BlockSpec((1, t, d), vec),
            pl.BlockSpec((1, tc, d), vec),
            pl.BlockSpec((1, 1, d), vec),
            pl.BlockSpec((1, 1, d), vec),
            pl.BlockSpec((1, 1, d), vec),
            pl.BlockSpec((1, 1, d), vec),
            pl.BlockSpec((1, 1, d), lambda b, c: (l, 0, 0)),
            pl.BlockSpec((d, LRU_CB), lambda b, c: (0, c)),
            pl.BlockSpec((d, LRU_CB), lambda b, c: (0, nblk + c)),
            pl.BlockSpec((CONV_W, LRU_CB), lambda b, c: (0, c)),
            pl.BlockSpec((1, LRU_CB), lambda b, c: (0, c)),
            pl.BlockSpec((2, gbk, LRU_BLOCK_W, 2 * LRU_BLOCK_W), lambda b, c: (0, c, 0, 0)),
            pl.BlockSpec((2, gbk, 1, 2 * LRU_BLOCK_W), lambda b, c: (0, c, 0, 0)),
            pl.BlockSpec((2, LRU_CB), lambda b, c: (0, c)),
        ],
        out_specs=[
            pl.BlockSpec((1, t, LRU_CB), lambda b, c: (b, 0, c)),
            pl.BlockSpec((1, tc, LRU_CB), lambda b, c: (b, 0, c)),
        ],
        out_shape=[
            jax.ShapeDtypeStruct((bsz, t, width), BF16),
            jax.ShapeDtypeStruct((bsz, tc, width), BF16),
        ],
        scratch_shapes=[
            pltpu.VMEM((t + 2 * CONV_HALO, d), BF16),
            pltpu.VMEM((tc + 2 * CONV_HALO, d), BF16),
            pltpu.VMEM((2, gbk, t + SEG_SLACK, LRU_BLOCK_W), F32),
            pltpu.VMEM((2, gbk, t + SEG_SLACK, LRU_BLOCK_W), F32),
            pltpu.VMEM((2, gbk, t, LRU_BLOCK_W), F32),
            pltpu.VMEM((2, gbk, t, LRU_BLOCK_W), F32),
            pltpu.VMEM((gbk, t + SEG_SLACK, LRU_BLOCK_W), F32),
            pltpu.VMEM((gbk, t + SEG_SLACK, LRU_BLOCK_W), F32),
        ],
        compiler_params=_cparams(2),
        name="lru",
    )(x, ctx, sh_l, sc_l, sh_c, sc_c, norm_g, w_in, w_in, conv_w, conv_b.reshape(1, width),
      gate_w, gate_b.reshape(2, -1, 1, 2 * LRU_BLOCK_W), lam)


def _rope(x, cos, sin_signed):
    lane = lax.broadcasted_iota(jnp.int32, x.shape, 1)
    quarter = HEAD_DIM // 4
    partner = jnp.where((lane & quarter) == 0,
                        pltpu.roll(x, HEAD_DIM - quarter, 1), pltpu.roll(x, quarter, 1))
    return x * cos + partner * sin_signed


def _qkv_kernel(x_ref, sh_ref, sc_ref, ng_ref, w_ref, cos_ref, sin_ref, q_ref, k_ref, vt_ref,
                *, n_heads, rotary):
    h = _norm_mod(x_ref[0], ng_ref[0], sh_ref[0], sc_ref[0]).astype(BF16)
    u = _dot(h, w_ref[...])
    scale = HEAD_DIM ** -0.5
    nq = n_heads * HEAD_DIM
    nk = N_KV_HEADS * HEAD_DIM
    for hd in range(n_heads + N_KV_HEADS):
        c = u[:, hd * HEAD_DIM:(hd + 1) * HEAD_DIM]
        if rotary:
            c = _rope(c, cos_ref[...], sin_ref[...])
        if hd < n_heads:
            q_ref[0, :, hd * HEAD_DIM:(hd + 1) * HEAD_DIM] = (c * scale).astype(BF16)
        else:
            kk = hd - n_heads
            k_ref[0, :, kk * HEAD_DIM:(kk + 1) * HEAD_DIM] = c.astype(BF16)
    vt_ref[0] = u[:, nq + nk:].T.astype(BF16)


def _qkv_call(x, sh, sc, norm_g, l, w_qkv, cos, sin_signed, rotary):
    bsz, t, d = x.shape
    nk = N_KV_HEADS * HEAD_DIM
    nq = w_qkv.shape[1] - 2 * nk
    tm = min(t, PROJ_ROWS)
    vec = lambda b, i: (b, 0, 0)
    return pl.pallas_call(
        functools.partial(_qkv_kernel, n_heads=nq // HEAD_DIM, rotary=rotary),
        grid=(bsz, t // tm),
        in_specs=[
            pl.BlockSpec((1, tm, d), lambda b, i: (b, i, 0)),
            pl.BlockSpec((1, 1, d), vec),
            pl.BlockSpec((1, 1, d), vec),
            pl.BlockSpec((1, 1, d), lambda b, i: (l, 0, 0)),
            pl.BlockSpec(w_qkv.shape, lambda b, i: (0, 0)),
            pl.BlockSpec((tm, HEAD_DIM), lambda b, i: (i, 0)),
            pl.BlockSpec((tm, HEAD_DIM), lambda b, i: (i, 0)),
        ],
        out_specs=[
            pl.BlockSpec((1, tm, nq), lambda b, i: (b, i, 0)),
            pl.BlockSpec((1, tm, nk), lambda b, i: (b, i, 0)),
            pl.BlockSpec((1, nk, tm), lambda b, i: (b, 0, i)),
        ],
        out_shape=[
            jax.ShapeDtypeStruct((bsz, t, nq), BF16),
            jax.ShapeDtypeStruct((bsz, t, nk), BF16),
            jax.ShapeDtypeStruct((bsz, nk, t), BF16),
        ],
        compiler_params=_cparams(2),
        name="qkv",
    )(x, sh, sc, norm_g, w_qkv, cos, sin_signed)


def _stack_heads(x, group):
    return jnp.concatenate([x[:, g * HEAD_DIM:(g + 1) * HEAD_DIM] for g in range(group)], axis=0)


def _attn_kernel(sink_ref, q_ref, k_ref, vt_ref, kc_ref, vct_ref, *rest, group, need_ctx):
    if need_ctx:
        qc_ref, o_ref, oc_ref = rest
    else:
        (o_ref,) = rest
    kh = pl.program_id(1)
    t = k_ref.shape[1]
    nblk = t // QBLOCK

    def sink_row(cols_per_head):
        col = lax.broadcasted_iota(jnp.int32, (1, group * cols_per_head), 1)
        row = jnp.zeros((1, group * cols_per_head), F32)
        for g in range(group):
            row = jnp.where(col >= g * cols_per_head, sink_ref[kh * group + g], row)
        return row

    kc = kc_ref[0]
    vct = vct_ref[0]
    sink_q = sink_row(QBLOCK)

    def softmax_vp(scores_values, sink):
        m = sink
        for sc, _ in scores_values:
            m = jnp.maximum(m, jnp.max(sc, axis=0, keepdims=True))
        den = jnp.exp(sink - m)
        acc = None
        for sc, val in scores_values:
            p = jnp.exp(sc - m)
            den = den + jnp.sum(p, axis=0, keepdims=True)
            vp = _dot(val, p.astype(BF16))
            acc = vp if acc is None else acc + vp
        return (acc / den).T

    def attend(r0, has_prev, has_next):
        lo = r0 - QBLOCK if has_prev else r0
        if not isinstance(lo, int):
            lo = pl.multiple_of(lo, QBLOCK)
        width = QBLOCK * (1 + has_prev + has_next)
        q_blk = q_ref[0, pl.ds(r0, QBLOCK), :]
        k_band = k_ref[0, pl.ds(lo, width), :]
        vt_band = vt_ref[0, :, pl.ds(lo, width)]
        sub = min(group, ATTN_HEADS)
        kj = lax.broadcasted_iota(jnp.int32, (QBLOCK, sub * QBLOCK), 0)
        qi = lax.broadcasted_iota(jnp.int32, (QBLOCK, sub * QBLOCK), 1) & (QBLOCK - 1)
        for h0 in range(0, group, sub):
            cols = slice(h0 * HEAD_DIM, (h0 + sub) * HEAD_DIM)
            qs = _stack_heads(q_blk[:, cols], sub)
            s = _dot_nt(k_band, qs)
            pieces = []
            if has_prev:
                pieces.append(jnp.where(QBLOCK + qi - kj <= WINDOW, s[:QBLOCK], NEG_INF))
            c0 = QBLOCK * has_prev
            pieces.append(s[c0:c0 + QBLOCK])
            if has_next:
                pieces.append(jnp.where(QBLOCK + kj - qi <= WINDOW, s[c0 + QBLOCK:], NEG_INF))
            s = jnp.concatenate(pieces, axis=0)
            o = softmax_vp([(s, vt_band), (_dot_nt(kc, qs), vct)], sink_q[:, cols])
            for g in range(sub):
                o_ref[0, pl.ds(r0, QBLOCK), (h0 + g) * HEAD_DIM:(h0 + g + 1) * HEAD_DIM] = (
                    o[g * QBLOCK:(g + 1) * QBLOCK].astype(BF16))

    attend(0, False, nblk > 1)
    if nblk > 1:
        attend((nblk - 1) * QBLOCK, True, False)

    def body(n, carry):
        attend(pl.multiple_of(n * QBLOCK, QBLOCK), True, True)
        return carry

    lax.fori_loop(1, nblk - 1, body, 0, unroll=7)

    if need_ctx:
        tc = qc_ref.shape[1]
        o = softmax_vp([(_dot_nt(kc, _stack_heads(qc_ref[0], group)), vct)], sink_row(tc))
        for g in range(group):
            oc_ref[0, :, g * HEAD_DIM:(g + 1) * HEAD_DIM] = o[g * tc:(g + 1) * tc].astype(BF16)


def _attn_call(sink, q, k, vt, qc, kc, vct, need_ctx):
    bsz, t, nq = q.shape
    tc = kc.shape[1]
    group = nq // HEAD_DIM // N_KV_HEADS
    gw = group * HEAD_DIM
    in_specs = [
        pl.BlockSpec(memory_space=pltpu.SMEM),
        pl.BlockSpec((1, t, gw), lambda b, h: (b, 0, h)),
        pl.BlockSpec((1, t, HEAD_DIM), lambda b, h: (b, 0, h)),
        pl.BlockSpec((1, HEAD_DIM, t), lambda b, h: (b, h, 0)),
        pl.BlockSpec((1, tc, HEAD_DIM), lambda b, h: (b, 0, h)),
        pl.BlockSpec((1, HEAD_DIM, tc), lambda b, h: (b, h, 0)),
    ]
    out_specs = [pl.BlockSpec((1, t, gw), lambda b, h: (b, 0, h))]
    out_shape = [jax.ShapeDtypeStruct((bsz, t, nq), BF16)]
    args = [sink, q, k, vt, kc, vct]
    if need_ctx:
        in_specs.append(pl.BlockSpec((1, tc, gw), lambda b, h: (b, 0, h)))
        out_specs.append(pl.BlockSpec((1, tc, gw), lambda b, h: (b, 0, h)))
        out_shape.append(jax.ShapeDtypeStruct((bsz, tc, nq), BF16))
        args.append(qc)
    outs = pl.pallas_call(
        functools.partial(_attn_kernel, group=group, need_ctx=need_ctx),
        grid=(bsz, N_KV_HEADS),
        in_specs=in_specs,
        out_specs=out_specs,
        out_shape=out_shape,
        compiler_params=_cparams(2),
        name="attn",
    )(*args)
    return (outs[0], outs[1]) if need_ctx else (outs[0], None)


def _rope_tables(t):
    half = HEAD_DIM // 2
    freqs = ROPE_BASE ** (-jnp.arange(0, half, 2, dtype=F32) / half)
    pos = jnp.arange(t)
    rows = (pos // GRID_W).astype(F32)[:, None] * freqs
    cols = (pos % GRID_W).astype(F32)[:, None] * freqs
    cos = jnp.concatenate([jnp.cos(rows), jnp.cos(rows), jnp.cos(cols), jnp.cos(cols)], axis=1)
    sin = jnp.concatenate([-jnp.sin(rows), jnp.sin(rows), -jnp.sin(cols), jnp.sin(cols)], axis=1)
    return cos, sin


def _count(mask):
    return jnp.sum(jnp.where(mask, 1.0, 0.0), axis=1, keepdims=True)


def _route_plan(x_ref, sh_ref, sc_ref, ng_ref, rw_ref, h_ref, slot_ref, starts_ref, slot_e_ref,
                gate_e_ref, aff_s, slotp_s, cap):
    t = x_ref.shape[1]
    n_exp = slot_e_ref.shape[1]
    chunk = min(t, ROW_CHUNK)
    rw = rw_ref[0]
    rw_hi = rw.astype(BF16)
    rw_lo = (rw - rw_hi.astype(F32)).astype(BF16)
    lane = lax.broadcasted_iota(jnp.int32, (chunk, LANES), 1)
    for r0 in range(0, t, chunk):
        h = _norm_mod(x_ref[0, r0:r0 + chunk], ng_ref[0], sh_ref[0], sc_ref[0])
        h_hi = h.astype(BF16)
        h_lo = (h - h_hi.astype(F32)).astype(BF16)
        h_ref[0, r0:r0 + chunk] = h_hi
        logits = _dot(h_hi, rw_hi) + (_dot(h_lo, rw_hi) + _dot(h_hi, rw_lo))
        logits = jnp.where(lane < n_exp, logits, NEG_INF)
        ex = jnp.exp(logits - jnp.max(logits, axis=1, keepdims=True))
        aff = ex / jnp.sum(ex, axis=1, keepdims=True)
        aff_s[:, r0:r0 + chunk] = aff.T
    aff_t = aff_s[0:n_exp, :]

    def as_f32(word):
        return lax.bitcast_convert_type(word, F32)

    def search(n_digits, accept):
        def body(i, w):
            shift = SEARCH_BITS * (n_digits - 1 - i)
            digit = jnp.zeros_like(w)
            for c in range(1, 1 << SEARCH_BITS):
                ok = accept(w | jnp.left_shift(jnp.int32(c), shift))
                digit = digit + jnp.where(ok, 1, 0)
            return w | jnp.left_shift(digit, shift)

        return lax.fori_loop(0, n_digits, body, jnp.zeros((n_exp, 1), jnp.int32))

    kth = search(30 // SEARCH_BITS, lambda w: _count(aff_t >= as_f32(w)) >= cap)
    above = aff_t >= as_f32(kth + 1)
    tie = (aff_t >= as_f32(kth)) & jnp.logical_not(above)
    need = cap - _count(above)
    idx = lax.broadcasted_iota(jnp.int32, aff_t.shape, 1)
    idx_digits = -(-max(1, (t - 1).bit_length()) // SEARCH_BITS)
    last = search(idx_digits, lambda w: _count(tie & (idx < w)) < need)
    sel = jnp.where(above | (tie & (idx <= last)), 1.0, 0.0)

    upper = jnp.where(lax.broadcasted_iota(jnp.int32, (chunk, chunk), 0)
                      < lax.broadcasted_iota(jnp.int32, (chunk, chunk), 1), 1.0, 0.0).astype(BF16)
    running = jnp.zeros((n_exp, 1), F32)
    starts = jnp.zeros((n_exp, LANES), F32)
    chunk_id = lax.broadcasted_iota(jnp.int32, (n_exp, LANES), 1)
    slotp_s[...] = jnp.full(slotp_s.shape, -1.0, F32)
    for r0 in range(0, t, chunk):
        starts = jnp.where(chunk_id == r0 // chunk, running, starts)
        sel_c = sel[:, r0:r0 + chunk]
        pos = _dot(sel_c.astype(BF16), upper) + running
        running = running + jnp.sum(sel_c, axis=1, keepdims=True)
        slotp_s[0:n_exp, r0:r0 + chunk] = jnp.where(sel_c > 0.5, pos, -1.0)
    starts_ref[0] = starts
    slot_e_ref[0] = slotp_s[0:n_exp, :]
    gate_e_ref[0] = sel * aff_t
    for r0 in range(0, t, chunk):
        slot_ref[0, r0:r0 + chunk] = slotp_s[:, r0:r0 + chunk].T


N_PLAN_OUTS = 5


def _plan_kernel(*refs, caps):
    ns = len(caps)
    ng_ref, rw_ref = refs[3 * ns:3 * ns + 2]
    outs0 = 3 * ns + 2
    scr0 = outs0 + N_PLAN_OUTS * ns
    for s in range(ns):
        _route_plan(*refs[3 * s:3 * s + 3], ng_ref, rw_ref,
                    *refs[outs0 + N_PLAN_OUTS * s:outs0 + N_PLAN_OUTS * (s + 1)],
                    *refs[scr0 + 2 * s:scr0 + 2 * s + 2], caps[s])


def _plan_call(streams, norm_g, l, router_p, n_exp):
    bsz, _, d = streams[0][0].shape
    vec = lambda b: (b, 0, 0)
    in_specs, args, out_specs, out_shapes, scratch = [], [], [], [], []
    for x, sh, sc, _ in streams:
        t = x.shape[1]
        in_specs += [pl.BlockSpec((1, t, d), vec), pl.BlockSpec((1, 1, d), vec),
                     pl.BlockSpec((1, 1, d), vec)]
        args += [x, sh, sc]
        for shape, dtype in (((t, d), BF16), ((t, LANES), F32), ((n_exp, LANES), F32),
                             ((n_exp, t), F32), ((n_exp, t), F32)):
            out_specs.append(pl.BlockSpec((1,) + shape, vec))
            out_shapes.append(jax.ShapeDtypeStruct((bsz,) + shape, dtype))
        scratch += [pltpu.VMEM((LANES, t), F32), pltpu.VMEM((LANES, t), F32)]
    in_specs += [pl.BlockSpec((1, 1, d), lambda b: (l, 0, 0)),
                 pl.BlockSpec((1, d, LANES), lambda b: (l, 0, 0))]
    outs = pl.pallas_call(
        functools.partial(_plan_kernel, caps=tuple(st[3] for st in streams)),
        grid=(bsz,),
        in_specs=in_specs,
        out_specs=out_specs,
        out_shape=out_shapes,
        scratch_shapes=scratch,
        compiler_params=_cparams(1),
        name="plan",
    )(*args, norm_g, router_p)
    plans = []
    for s, (x, _, _, _) in enumerate(streams):
        h, slot, starts, slot_e, gate_e = outs[N_PLAN_OUTS * s:N_PLAN_OUTS * (s + 1)]
        n_chunks = -(-x.shape[1] // ROW_CHUNK)
        starts = jnp.swapaxes(starts[:, :, :n_chunks].astype(jnp.int32), 1, 2)
        plans.append((h, slot, starts.reshape(bsz, n_chunks * n_exp), slot_e, gate_e))
    return plans


def _gather_stream(b, starts_ref, h_ref, slot_e_ref, gate_e_ref, xg_ref, gs_ref, row0, cap):
    n_exp, t = slot_e_ref.shape[1:]
    d = h_ref.shape[2]
    chunk = min(t, ROW_CHUNK)
    n_chunks = t // chunk
    win = min(cap, SLOT_WINDOW)
    rows_all = slice(row0, row0 + cap)
    for e in range(n_exp):
        xg_ref[e, 0, rows_all] = jnp.zeros((cap, d), BF16)
        gs_ref[e, 0, rows_all] = jnp.zeros((cap, LANES), F32)

    def scatter_rows(e, rows, n_rows, tokens, gates):
        cur = xg_ref[e, 0, rows].astype(F32)
        xg_ref[e, 0, rows] = (cur + tokens).astype(BF16)
        gs_ref[e, 0, rows] = gs_ref[e, 0, rows] + jnp.broadcast_to(gates, (n_rows, LANES))

    for c in range(n_chunks):
        toks = slice(c * chunk, (c + 1) * chunk)
        h_c = h_ref[0, toks]
        if cap > win:
            nxt = min(c + 1, n_chunks - 1)
            starts, misfit = [], jnp.int32(0)
            for e in range(n_exp):
                first = starts_ref[b, c * n_exp + e]
                end = starts_ref[b, nxt * n_exp + e] if c + 1 < n_chunks else cap
                w0 = jnp.minimum(first & -(2 * SUBLANES), cap - win)
                misfit = misfit | jnp.where(end - w0 > win, 1, 0)
                starts.append(w0)
        else:
            starts, misfit = [0] * n_exp, None

        def windowed(toks=toks, h_c=h_c, starts=starts):
            slot_id = lax.broadcasted_iota(jnp.int32, (win, chunk), 0).astype(F32)
            hits = [slot_e_ref[0, e:e + 1, toks] - jnp.float32(starts[e]) == slot_id
                    for e in range(n_exp)]
            onehot = jnp.concatenate([jnp.where(hit, 1.0, 0.0).astype(BF16) for hit in hits], axis=0)
            picked = _dot(onehot, h_c)
            for e in range(n_exp):
                w0 = starts[e] if isinstance(starts[e], int) else pl.multiple_of(starts[e], 2 * SUBLANES)
                gates = jnp.sum(jnp.where(hits[e], gate_e_ref[0, e:e + 1, toks], 0.0),
                                axis=1, keepdims=True)
                scatter_rows(e, pl.ds(row0 + w0, win), win, picked[e * win:(e + 1) * win], gates)

        def dense(toks=toks, h_c=h_c):
            slot_id = lax.broadcasted_iota(jnp.int32, (cap, chunk), 0).astype(F32)
            for e in range(n_exp):
                hit = slot_e_ref[0, e:e + 1, toks] == slot_id
                gates = jnp.sum(jnp.where(hit, gate_e_ref[0, e:e + 1, toks], 0.0),
                                axis=1, keepdims=True)
                scatter_rows(e, rows_all, cap, _dot(jnp.where(hit, 1.0, 0.0).astype(BF16), h_c), gates)

        if misfit is None:
            windowed()
        else:
            pl.when(misfit == 0)(windowed)
            pl.when(misfit != 0)(dense)


def _gather_kernel(*refs, caps):
    ns = len(caps)
    xg_ref, gs_ref = refs[4 * ns:4 * ns + 2]
    b = pl.program_id(0)
    row0 = 0
    for s in range(ns):
        _gather_stream(b, refs[s], *refs[ns + 3 * s:ns + 3 * s + 3], xg_ref, gs_ref, row0, caps[s])
        row0 += caps[s]


def _gather_call(plans, caps, n_exp):
    bsz, _, d = plans[0][0].shape
    rows = sum(caps)
    vec = lambda b, *_: (b, 0, 0)
    in_specs, args = [], []
    for h, _, _, slot_e, gate_e in plans:
        t = h.shape[1]
        in_specs += [pl.BlockSpec((1, t, d), vec), pl.BlockSpec((1, n_exp, t), vec),
                     pl.BlockSpec((1, n_exp, t), vec)]
        args += [h, slot_e, gate_e]
    grid_spec = pltpu.PrefetchScalarGridSpec(
        num_scalar_prefetch=len(plans),
        grid=(bsz,),
        in_specs=in_specs,
        out_specs=[
            pl.BlockSpec((n_exp, 1, rows, d), lambda b, *_: (0, b, 0, 0)),
            pl.BlockSpec((n_exp, 1, rows, LANES), lambda b, *_: (0, b, 0, 0)),
        ],
    )
    return pl.pallas_call(
        functools.partial(_gather_kernel, caps=caps),
        grid_spec=grid_spec,
        out_shape=[
            jax.ShapeDtypeStruct((n_exp, bsz, rows, d), BF16),
            jax.ShapeDtypeStruct((n_exp, bsz, rows, LANES), F32),
        ],
        compiler_params=_cparams(1),
        name="gather",
    )(*[p[2] for p in plans], *args)


def _ffn_kernel(x_ref, gs_ref, wg_ref, wu_ref, wd_ref, y_ref, w_s):
    @pl.when(pl.program_id(1) == 0)
    def _():
        w_s[0] = wg_ref[0, 0].astype(BF16)
        w_s[1] = wu_ref[0, 0].astype(BF16)
        w_s[2] = wd_ref[0, 0].astype(BF16)

    x = x_ref[0]
    f = w_s.shape[2]
    acc = None
    for c0 in range(0, f, FFN_COLS):
        a = _dot(x, w_s[0, :, c0:c0 + FFN_COLS])
        u = _dot(x, w_s[1, :, c0:c0 + FFN_COLS])
        hmid = (a * _sigmoid(a) * u).astype(BF16)
        part = _dot(hmid, w_s[2, c0:c0 + FFN_COLS, :])
        acc = part if acc is None else acc + part
    y_ref[0] = (acc * gs_ref[0][:, 0:1]).astype(BF16)


def _ffn_call(xg, gs, w_gate, w_up, w_down, l):
    n_exp, rows, d = xg.shape
    f = w_gate.shape[-1]
    tm = rows // max(1, -(-rows // FFN_ROWS))
    assert rows % tm == 0 and tm % (2 * SUBLANES) == 0, (rows, tm)
    w_map = lambda e, r: (l, e, 0, 0)
    return pl.pallas_call(
        _ffn_kernel,
        grid=(n_exp, rows // tm),
        in_specs=[
            pl.BlockSpec((1, tm, d), lambda e, r: (e, r, 0)),
            pl.BlockSpec((1, tm, LANES), lambda e, r: (e, r, 0)),
            pl.BlockSpec((1, 1, d, f), w_map),
            pl.BlockSpec((1, 1, d, f), w_map),
            pl.BlockSpec((1, 1, f, d), w_map),
        ],
        out_specs=pl.BlockSpec((1, tm, d), lambda e, r: (e, r, 0)),
        out_shape=jax.ShapeDtypeStruct((n_exp, rows, d), BF16),
        scratch_shapes=[pltpu.VMEM((3, d, f), BF16)],
        compiler_params=_cparams(2),
        name="ffn",
    )(xg, gs, w_gate, w_up, w_down)


def _combine_kernel(starts_ref, y_ref, slot_ref, x_ref, g_ref, fg_ref, o_ref, *, final_norm, windowed):
    n_exp, _, cap, d = y_ref.shape
    tk = x_ref.shape[1]
    tile = min(tk, ROW_CHUNK)
    b = pl.program_id(0)
    tile0 = pl.program_id(1) * (tk // tile)
    n_tiles = pl.num_programs(1) * (tk // tile)

    for sub in range(tk // tile):
        rows = slice(sub * tile, (sub + 1) * tile)
        slots = slot_ref[0, rows]

        def finish(acc, rows=rows):
            out = x_ref[0, rows] + g_ref[0] * acc
            if final_norm:
                ms = jnp.mean(out * out, axis=-1, keepdims=True)
                out = out * lax.rsqrt(ms + NORM_EPS) * fg_ref[...]
            o_ref[0, rows] = out

        def dense(slots=slots, finish=finish):
            lane = lax.broadcasted_iota(jnp.int32, (tile, cap), 1).astype(F32)
            acc = jnp.zeros((tile, d), F32)
            for e in range(n_exp):
                onehot = jnp.where(lane == slots[:, e:e + 1], 1.0, 0.0).astype(BF16)
                acc = acc + _dot(onehot, y_ref[e, 0])
            finish(acc)

        if not windowed:
            dense()
            continue

        t_idx = tile0 + sub
        nxt = jnp.minimum(t_idx + 1, n_tiles - 1)
        win, misfit = [], jnp.int32(0)
        for e in range(n_exp):
            first = starts_ref[b, t_idx * n_exp + e]
            end = jnp.where(t_idx + 1 < n_tiles, starts_ref[b, nxt * n_exp + e], cap)
            w0 = jnp.minimum(first & -(2 * SUBLANES), cap - SLOT_WINDOW)
            misfit = misfit | jnp.where(end - w0 > SLOT_WINDOW, 1, 0)
            win.append(w0)

        def sparse(slots=slots, finish=finish, win=win):
            lane = lax.broadcasted_iota(jnp.int32, (tile, LANES), 1)
            low = lane < SLOT_WINDOW
            pos = (lane & (SLOT_WINDOW - 1)).astype(F32)
            per_dot = 2 * LANES // SLOT_WINDOW
            acc = jnp.zeros((tile, d), F32)
            for e0 in range(0, n_exp, per_dot):
                halves = []
                for e in range(e0, e0 + per_dot, 2):
                    rel_a = slots[:, e:e + 1] - win[e].astype(F32)
                    rel_b = slots[:, e + 1:e + 2] - win[e + 1].astype(F32)
                    hit = jnp.where(low, rel_a, rel_b) == pos
                    halves.append(jnp.where(hit, 1.0, 0.0).astype(BF16))
                yk = jnp.concatenate(
                    [y_ref[e, 0, pl.ds(pl.multiple_of(win[e], 2 * SUBLANES), SLOT_WINDOW), :]
                     for e in range(e0, e0 + per_dot)], axis=0)
                acc = acc + _dot(jnp.concatenate(halves, axis=1), yk)
            finish(acc)

        pl.when(misfit == 0)(sparse)
        pl.when(misfit != 0)(dense)


def _combine_call(y, slot, starts, x, gate, final_g, cap, row0, final_norm):
    bsz, t, d = x.shape
    n_exp = y.shape[0]
    tk = min(t, COMBINE_ROWS)
    assert row0 % cap == 0
    windowed = (cap >= SLOT_WINDOW and n_exp % (2 * LANES // SLOT_WINDOW) == 0
                and t % ROW_CHUNK == 0)
    grid_spec = pltpu.PrefetchScalarGridSpec(
        num_scalar_prefetch=1,
        grid=(bsz, t // tk),
        in_specs=[
            pl.BlockSpec((n_exp, 1, cap, d), lambda b, i, st: (0, b, row0 // cap, 0)),
            pl.BlockSpec((1, tk, LANES), lambda b, i, st: (b, i, 0)),
            pl.BlockSpec((1, tk, d), lambda b, i, st: (b, i, 0)),
            pl.BlockSpec((1, 1, d), lambda b, i, st: (b, 0, 0)),
            pl.BlockSpec((1, d), lambda b, i, st: (0, 0)),
        ],
        out_specs=pl.BlockSpec((1, tk, d), lambda b, i, st: (b, i, 0)),
    )
    return pl.pallas_call(
        functools.partial(_combine_kernel, final_norm=final_norm, windowed=windowed),
        grid_spec=grid_spec,
        out_shape=jax.ShapeDtypeStruct(x.shape, F32),
        compiler_params=_cparams(2),
        name="combine",
    )(starts, y, slot, x, gate, final_g)


def kernel(x, c, ctx, c_ctx, ada_w, ada_b, norm1_g, norm2_g, lru_w_in, lru_conv_w, lru_conv_b,
           lru_gate_w, lru_gate_b, lru_lambda, lru_w_out, attn_w_qkv, attn_sink, attn_w_o,
           moe_router, moe_w_gate, moe_w_up, moe_w_down, final_g):
    bsz, t, d = x.shape
    tc = ctx.shape[1]
    depth = ada_w.shape[0]
    n_exp = moe_router.shape[-1]
    cap_l = CAPACITY_FACTOR * t // n_exp
    cap_c = CAPACITY_FACTOR * tc // n_exp

    cond_rows = 2 * SUBLANES
    cond = jnp.zeros((cond_rows, d), F32).at[:bsz].set(c).at[bsz].set(c_ctx)
    mod = _ada_call(cond, ada_w, ada_b).reshape(depth, cond_rows, 6, 1, d)
    norm1 = norm1_g.reshape(depth, 1, d)
    norm2 = norm2_g.reshape(depth, 1, d)
    router_p = jnp.pad(moe_router, ((0, 0), (0, 0), (0, LANES - n_exp)))
    final_g2 = final_g.reshape(1, d)
    cos, sin_signed = _rope_tables(t)
    ones_c = jnp.ones((tc, HEAD_DIM), F32)

    for l in range(depth):
        need_ctx = l < depth - 1
        lat = [mod[l, :bsz, i] for i in range(6)]
        con = [jnp.broadcast_to(mod[l, bsz, i], (bsz, 1, d)) for i in range(6)]
        j = l // 2
        if l % 2 == 0:
            s_l, s_c = _lru_call(x, ctx, lat[0], lat[1], con[0], con[1], norm1, l,
                                 lru_w_in[j].astype(BF16), lru_conv_w[j], lru_conv_b[j],
                                 (0.5 * lru_gate_w[j]).astype(BF16), 0.5 * lru_gate_b[j],
                                 lru_lambda[j])
            w_out = lru_w_out[j].astype(BF16)
        else:
            w_qkv = attn_w_qkv[j].astype(BF16)
            q, k, v = _qkv_call(x, lat[0], lat[1], norm1, l, w_qkv, cos, sin_signed, True)
            qc, kc, vc = _qkv_call(ctx, con[0], con[1], norm1, l, w_qkv, ones_c, ones_c, False)
            s_l, s_c = _attn_call(attn_sink[j], q, k, v, qc, kc, vc, need_ctx)
            w_out = attn_w_o[j].astype(BF16)
        x = _resid_call(s_l, w_out, x, lat[2])
        if need_ctx:
            ctx = _resid_call(s_c, w_out, ctx, con[2])

        streams = [(x, lat[3], lat[4], cap_l)]
        if need_ctx:
            streams.append((ctx, con[3], con[4], cap_c))
        plans = _plan_call(streams, norm2, l, router_p, n_exp)
        xg, gs = _gather_call(plans, tuple(st[3] for st in streams), n_exp)
        rows = xg.shape[2]
        y = _ffn_call(xg.reshape(n_exp, bsz * rows, d), gs.reshape(n_exp, bsz * rows, LANES),
                      moe_w_gate, moe_w_up, moe_w_down, l).reshape(xg.shape)
        x = _combine_call(y, plans[0][1], plans[0][2], x, lat[5], final_g2, cap_l, 0, not need_ctx)
        if need_ctx:
            ctx = _combine_call(y, plans[1][1], plans[1][2], ctx, con[5], final_g2, cap_c, cap_l,
                                False)
    return x
```

```python
import functools
import math

import jax
import jax.numpy as jnp
from jax import lax
from jax.experimental import pallas as pl
from jax.experimental.pallas import tpu as pltpu

F32 = jnp.float32
BF16 = jnp.bfloat16

LANES = 128
SUBLANES = 8
VMEM_LIMIT_BYTES = 58 * 1024 * 1024

NORM_EPS = 1e-6
NEG_INF = -1e30
LRU_C = 8.0
LRU_BLOCK_W = 128
CONV_W = 4
CONV_HALO = 16
LRU_ROWS = 256
HEAD_DIM = 128
N_KV_HEADS = 2
QBLOCK = 128
WINDOW = 128
GRID_W = 64
ROPE_BASE = 10000.0
CAPACITY_FACTOR = 2

ROW_CHUNK = 256
PROJ_ROWS = 1024
RESID_ROWS = 2048
ATTN_HEADS = 4
SCAN_UNROLL = 8
MOD_TILE = 1536
SEARCH_BITS = 3
FFN_ROWS = 1152
FFN_COLS = 256
COMBINE_ROWS = 1024
SLOT_WINDOW = LANES // 2


def _cparams(n_axes):
    return pltpu.CompilerParams(
        dimension_semantics=("arbitrary",) * n_axes, vmem_limit_bytes=VMEM_LIMIT_BYTES)


def _dot(a, b):
    return jnp.dot(a, b, preferred_element_type=F32)


def _dot_nt(a, b):
    return lax.dot_general(a, b, (((1,), (1,)), ((), ())), preferred_element_type=F32)


def _sigmoid(x):
    return 0.5 * jnp.tanh(0.5 * x) + 0.5


def _sqrt_nonneg(z):
    return jnp.where(z > 0.0, z * lax.rsqrt(z), 0.0)


def _norm_mod(x, g, shift, scale):
    ms = jnp.mean(x * x, axis=-1, keepdims=True)
    y = x * lax.rsqrt(ms + NORM_EPS) * g
    return y * (1.0 + scale) + shift


def _ada_kernel(c_ref, w_ref, b_ref, o_ref):
    c = c_ref[...]
    s = (c * _sigmoid(c)).astype(BF16)
    o_ref[0] = _dot(s, w_ref[0].astype(BF16)) + b_ref[0]


def _ada_call(cond, ada_w, ada_b):
    n_layers, d, n_out = ada_w.shape
    rows = cond.shape[0]
    return pl.pallas_call(
        _ada_kernel,
        grid=(n_layers, n_out // MOD_TILE),
        in_specs=[
            pl.BlockSpec((rows, d), lambda l, n: (0, 0)),
            pl.BlockSpec((1, d, MOD_TILE), lambda l, n: (l, 0, n)),
            pl.BlockSpec((1, 1, MOD_TILE), lambda l, n: (l, 0, n)),
        ],
        out_specs=pl.BlockSpec((1, rows, MOD_TILE), lambda l, n: (l, 0, n)),
        out_shape=jax.ShapeDtypeStruct((n_layers, rows, n_out), F32),
        compiler_params=_cparams(2),
        name="ada",
    )(cond, ada_w, ada_b.reshape(n_layers, 1, n_out))


def _resid_kernel(a_ref, w_ref, x_ref, g_ref, o_ref):
    o_ref[0] = x_ref[0] + g_ref[0] * _dot(a_ref[0], w_ref[...])


def _resid_call(act, w, x, gate):
    bsz, t, d = x.shape
    k = act.shape[-1]
    tm = min(t, RESID_ROWS)
    return pl.pallas_call(
        _resid_kernel,
        grid=(bsz, t // tm),
        in_specs=[
            pl.BlockSpec((1, tm, k), lambda b, i: (b, i, 0)),
            pl.BlockSpec((k, d), lambda b, i: (0, 0)),
            pl.BlockSpec((1, tm, d), lambda b, i: (b, i, 0)),
            pl.BlockSpec((1, 1, d), lambda b, i: (b, 0, 0)),
        ],
        out_specs=pl.BlockSpec((1, tm, d), lambda b, i: (b, i, 0)),
        out_shape=jax.ShapeDtypeStruct(x.shape, F32),
        compiler_params=_cparams(2),
        name="resid",
    )(act, w, x, gate)


LRU_CB = 2 * LRU_BLOCK_W


def _log_sigmoid(x):
    return jnp.minimum(x, 0.0) - jnp.log1p(jnp.exp(-jnp.abs(x)))


def _conv_time(u, cw, cb):
    n = u.shape[0]
    left = (CONV_W - 1) // 2
    acc = cb + u * cw[left:left + 1]
    for k in range(CONV_W):
        off = k - left
        if off != 0:
            acc = acc + pltpu.roll(u, (-off) % n, 0) * cw[k:k + 1]
    return acc[CONV_HALO:n - CONV_HALO]


N_SEG = SUBLANES
SEG_SLACK = N_SEG * 2 * SUBLANES


def _segments(n):
    seg = n // N_SEG
    assert n % (N_SEG * SUBLANES) == 0, n
    pad = SUBLANES if (seg // SUBLANES) % 2 == 0 else 2 * SUBLANES
    return seg, seg + pad


def _lru_kernel(xl_ref, xc_ref, shl_ref, scl_ref, shc_ref, scc_ref, ng_ref, wy_ref, wx_ref,
                cw_ref, cb_ref, gw_ref, gb_ref, lam_ref, ol_ref, oc_ref,
                hl_s, hc_s, a_s, b_s, hs_s, ps_s, y_s, o_s):
    j = pl.program_id(1)
    t_lat = xl_ref.shape[1]
    t_ctx = xc_ref.shape[1]
    gbk = LRU_CB // LRU_BLOCK_W

    @pl.when(j == 0)
    def _():
        g = ng_ref[0]
        for x_ref, h_s, sh_ref, sc_ref in ((xl_ref, hl_s, shl_ref, scl_ref),
                                           (xc_ref, hc_s, shc_ref, scc_ref)):
            n = x_ref.shape[1]
            step = min(n, ROW_CHUNK)
            halo = jnp.zeros((CONV_HALO, h_s.shape[1]), BF16)
            h_s[0:CONV_HALO] = halo
            for r0 in range(0, n, step):
                h_s[CONV_HALO + r0:CONV_HALO + r0 + step] = _norm_mod(
                    x_ref[0, r0:r0 + step], g, sh_ref[0], sc_ref[0]).astype(BF16)
            h_s[CONV_HALO + n:2 * CONV_HALO + n] = halo

    half_rate = (0.5 * LRU_C / math.log(2.0)) * _log_sigmoid(lam_ref[...])

    def branches(h_s, n):
        seg, pitch = _segments(n)
        rows = min(n, LRU_ROWS)
        assert n % rows == 0 and rows % seg == 0, (n, rows, seg)
        for r0 in range(0, n, rows):
            branch_rows(h_s, r0, rows, seg, pitch)

    def branch_rows(h_s, r0, rows, seg, pitch):
        def put(store, val):
            for s in range(r0 // seg, (r0 + rows) // seg):
                store(slice(s * pitch, s * pitch + seg), val[s * seg - r0:(s + 1) * seg - r0])

        y = jax.nn.gelu(_dot(h_s[CONV_HALO + r0:CONV_HALO + r0 + rows], wy_ref[...]))
        xb = _conv_time(_dot(h_s[r0:r0 + rows + 2 * CONV_HALO], wx_ref[...]),
                        cw_ref[...], cb_ref[...])
        for kb in range(gbk):
            lanes = slice(kb * LRU_BLOCK_W, (kb + 1) * LRU_BLOCK_W)

            def put_y(rows, v, kb=kb):
                y_s[kb, rows] = v

            put(put_y, y[:, lanes])
            xk = xb[:, lanes]
            xk16 = xk.astype(BF16)
            half_x = 0.5 * xk
            for d in range(2):
                th = jnp.tanh(_dot(xk16, gw_ref[d, kb]) + gb_ref[d, kb])
                th_r = th[:, :LRU_BLOCK_W]
                th_i = th[:, LRU_BLOCK_W:]
                a = jnp.exp2(th_r * half_rate[d:d + 1, lanes] + half_rate[d:d + 1, lanes])

                def put_a(rows, v, d=d, kb=kb):
                    a_s[d, kb, rows] = v

                def put_b(rows, v, d=d, kb=kb):
                    b_s[d, kb, rows] = v

                put(put_a, a)
                put(put_b, _sqrt_nonneg(1.0 - a * a) * (th_i * half_x + half_x))

    def scan(n, carry_in, o_ref):
        seg, pitch = _segments(n)
        chains = [(d, kb) for d in range(2) for kb in range(gbk)]

        def advance(jstep, state):
            out = []
            for c, (d, kb) in enumerate(chains):
                jj = (seg - 1 - jstep) if d == 1 else jstep
                h, p = state[2 * c], state[2 * c + 1]
                a = a_s[d, kb, pl.ds(jj, N_SEG, stride=pitch), :]
                b = b_s[d, kb, pl.ds(jj, N_SEG, stride=pitch), :]
                h = a * h + b
                p = p * a
                rows = pl.ds(pl.multiple_of(jj * N_SEG, N_SEG), N_SEG)
                hs_s[d, kb, rows] = h
                ps_s[d, kb, rows] = p
                out += [h, p]
            return tuple(out)

        init = (jnp.zeros((N_SEG, LRU_BLOCK_W), F32), jnp.ones((N_SEG, LRU_BLOCK_W), F32))
        state = lax.fori_loop(0, seg, advance, init * len(chains), unroll=SCAN_UNROLL)

        row = lax.broadcasted_iota(jnp.int32, (N_SEG, LRU_BLOCK_W), 0)
        enter, carry_out = [], []
        for c, (d, kb) in enumerate(chains):
            h_end, p_end = state[2 * c], state[2 * c + 1]
            cur = carry_in[c]
            vec = jnp.zeros((N_SEG, LRU_BLOCK_W), F32)
            for s in (range(N_SEG - 1, -1, -1) if d == 1 else range(N_SEG)):
                vec = jnp.where(row == s, cur, vec)
                cur = h_end[s:s + 1] + p_end[s:s + 1] * cur
            enter.append(vec)
            carry_out.append(cur)

        def finish(jstep, carry):
            rows = pl.ds(pl.multiple_of(jstep * N_SEG, N_SEG), N_SEG)
            strided = pl.ds(jstep, N_SEG, stride=pitch)
            for kb in range(gbk):
                tot = None
                for c, (d, kb2) in enumerate(chains):
                    if kb2 == kb:
                        part = hs_s[d, kb, rows] + ps_s[d, kb, rows] * enter[c]
                        tot = part if tot is None else tot + part
                o_s[kb, strided, :] = tot * y_s[kb, strided, :]
            return carry

        lax.fori_loop(0, seg, finish, 0, unroll=SCAN_UNROLL)
        for kb in range(gbk):
            for s in range(N_SEG):
                o_ref[0, s * seg:(s + 1) * seg, kb * LRU_BLOCK_W:(kb + 1) * LRU_BLOCK_W] = (
                    o_s[kb, s * pitch:s * pitch + seg].astype(BF16))
        return carry_out

    branches(hc_s, t_ctx)
    carries = scan(t_ctx, [jnp.zeros((1, LRU_BLOCK_W), F32)] * (2 * gbk), oc_ref)
    branches(hl_s, t_lat)
    scan(t_lat, carries, ol_ref)


def _lru_call(x, ctx, sh_l, sc_l, sh_c, sc_c, norm_g, l, w_in, conv_w, conv_b, gate_w, gate_b, lam):
    bsz, t, d = x.shape
    tc = ctx.shape[1]
    width = w_in.shape[1] // 2
    nblk = width // LRU_CB
    gbk = LRU_CB // LRU_BLOCK_W
    vec = lambda b, c: (b, 0, 0)
    return pl.pallas_call(
        _lru_kernel,
        grid=(bsz, nblk),
        in_specs=[
            pl.BlockSpec((1, t, d), vec),
            pl.BlockSpec((1, tc, d), vec),
            pl.BlockSpec((1, 1, d), vec),
            pl.BlockSpec((1, 1, d), vec),
            pl.BlockSpec((1, 1, d), vec),
            pl.BlockSpec((1, 1, d), vec),
            pl.BlockSpec((1, 1, d), lambda b, c: (l, 0, 0)),
            pl.BlockSpec((d, LRU_CB), lambda b, c: (0, c)),
            pl.BlockSpec((d, LRU_CB), lambda b, c: (0, nblk + c)),
            pl.BlockSpec((CONV_W, LRU_CB), lambda b, c: (0, c)),
            pl.BlockSpec((1, LRU_CB), lambda b, c: (0, c)),
            pl.BlockSpec((2, gbk, LRU_BLOCK_W, 2 * LRU_BLOCK_W), lambda b, c: (0, c, 0, 0)),
            pl.BlockSpec((2, gbk, 1, 2 * LRU_BLOCK_W), lambda b, c: (0, c, 0, 0)),
            pl.BlockSpec((2, LRU_CB), lambda b, c: (0, c)),
        ],
        out_specs=[
            pl.BlockSpec((1, t, LRU_CB), lambda b, c: (b, 0, c)),
            pl.BlockSpec((1, tc, LRU_CB), lambda b, c: (b, 0, c)),
        ],
        out_shape=[
            jax.ShapeDtypeStruct((bsz, t, width), BF16),
            jax.ShapeDtypeStruct((bsz, tc, width), BF16),
        ],
        scratch_shapes=[
            pltpu.VMEM((t + 2 * CONV_HALO, d), BF16),
            pltpu.VMEM((tc + 2 * CONV_HALO, d), BF16),
            pltpu.VMEM((2, gbk, t + SEG_SLACK, LRU_BLOCK_W), F32),
            pltpu.VMEM((2, gbk, t + SEG_SLACK, LRU_BLOCK_W), F32),
            pltpu.VMEM((2, gbk, t, LRU_BLOCK_W), F32),
            pltpu.VMEM((2, gbk, t, LRU_BLOCK_W), F32),
            pltpu.VMEM((gbk, t + SEG_SLACK, LRU_BLOCK_W), F32),
            pltpu.VMEM((gbk, t + SEG_SLACK, LRU_BLOCK_W), F32),
        ],
        compiler_params=_cparams(2),
        name="lru",
    )(x, ctx, sh_l, sc_l, sh_c, sc_c, norm_g, w_in, w_in, conv_w, conv_b.reshape(1, width),
      gate_w, gate_b.reshape(2, -1, 1, 2 * LRU_BLOCK_W), lam)


def _rope(x, cos, sin_signed):
    lane = lax.broadcasted_iota(jnp.int32, x.shape, 1)
    quarter = HEAD_DIM // 4
    partner = jnp.where((lane & quarter) == 0,
                        pltpu.roll(x, HEAD_DIM - quarter, 1), pltpu.roll(x, quarter, 1))
    return x * cos + partner * sin_signed


def _qkv_kernel(x_ref, sh_ref, sc_ref, ng_ref, w_ref, cos_ref, sin_ref, q_ref, k_ref, vt_ref,
                *, n_heads, rotary):
    h = _norm_mod(x_ref[0], ng_ref[0], sh_ref[0], sc_ref[0]).astype(BF16)
    u = _dot(h, w_ref[...])
    scale = HEAD_DIM ** -0.5
    nq = n_heads * HEAD_DIM
    nk = N_KV_HEADS * HEAD_DIM
    for hd in range(n_heads + N_KV_HEADS):
        c = u[:, hd * HEAD_DIM:(hd + 1) * HEAD_DIM]
        if rotary:
            c = _rope(c, cos_ref[...], sin_ref[...])
        if hd < n_heads:
            q_ref[0, :, hd * HEAD_DIM:(hd + 1) * HEAD_DIM] = (c * scale).astype(BF16)
        else:
            kk = hd - n_heads
            k_ref[0, :, kk * HEAD_DIM:(kk + 1) * HEAD_DIM] = c.astype(BF16)
    vt_ref[0] = u[:, nq + nk:].T.astype(BF16)


def _qkv_call(x, sh, sc, norm_g, l, w_qkv, cos, sin_signed, rotary):
    bsz, t, d = x.shape
    nk = N_KV_HEADS * HEAD_DIM
    nq = w_qkv.shape[1] - 2 * nk
    tm = min(t, PROJ_ROWS)
    vec = lambda b, i: (b, 0, 0)
    return pl.pallas_call(
        functools.partial(_qkv_kernel, n_heads=nq // HEAD_DIM, rotary=rotary),
        grid=(bsz, t // tm),
        in_specs=[
            pl.BlockSpec((1, tm, d), lambda b, i: (b, i, 0)),
            pl.BlockSpec((1, 1, d), vec),
            pl.BlockSpec((1, 1, d), vec),
            pl.BlockSpec((1, 1, d), lambda b, i: (l, 0, 0)),
            pl.BlockSpec(w_qkv.shape, lambda b, i: (0, 0)),
            pl.BlockSpec((tm, HEAD_DIM), lambda b, i: (i, 0)),
            pl.BlockSpec((tm, HEAD_DIM), lambda b, i: (i, 0)),
        ],
        out_specs=[
            pl.BlockSpec((1, tm, nq), lambda b, i: (b, i, 0)),
            pl.BlockSpec((1, tm, nk), lambda b, i: (b, i, 0)),
            pl.BlockSpec((1, nk, tm), lambda b, i: (b, 0, i)),
        ],
        out_shape=[
            jax.ShapeDtypeStruct((bsz, t, nq), BF16),
            jax.ShapeDtypeStruct((bsz, t, nk), BF16),
            jax.ShapeDtypeStruct((bsz, nk, t), BF16),
        ],
        compiler_params=_cparams(2),
        name="qkv",
    )(x, sh, sc, norm_g, w_qkv, cos, sin_signed)


def _stack_heads(x, group):
    return jnp.concatenate([x[:, g * HEAD_DIM:(g + 1) * HEAD_DIM] for g in range(group)], axis=0)


def _attn_kernel(sink_ref, q_ref, k_ref, vt_ref, kc_ref, vct_ref, *rest, group, need_ctx):
    if need_ctx:
        qc_ref, o_ref, oc_ref = rest
    else:
        (o_ref,) = rest
    kh = pl.program_id(1)
    t = k_ref.shape[1]
    nblk = t // QBLOCK

    def sink_row(cols_per_head):
        col = lax.broadcasted_iota(jnp.int32, (1, group * cols_per_head), 1)
        row = jnp.zeros((1, group * cols_per_head), F32)
        for g in range(group):
            row = jnp.where(col >= g * cols_per_head, sink_ref[kh * group + g], row)
        return row

    kc = kc_ref[0]
    vct = vct_ref[0]
    sink_q = sink_row(QBLOCK)

    def softmax_vp(scores_values, sink):
        m = sink
        for sc, _ in scores_values:
            m = jnp.maximum(m, jnp.max(sc, axis=0, keepdims=True))
        den = jnp.exp(sink - m)
        acc = None
        for sc, val in scores_values:
            p = jnp.exp(sc - m)
            den = den + jnp.sum(p, axis=0, keepdims=True)
            vp = _dot(val, p.astype(BF16))
            acc = vp if acc is None else acc + vp
        return (acc / den).T

    def attend(r0, has_prev, has_next):
        lo = r0 - QBLOCK if has_prev else r0
        if not isinstance(lo, int):
            lo = pl.multiple_of(lo, QBLOCK)
        width = QBLOCK * (1 + has_prev + has_next)
        q_blk = q_ref[0, pl.ds(r0, QBLOCK), :]
        k_band = k_ref[0, pl.ds(lo, width), :]
        vt_band = vt_ref[0, :, pl.ds(lo, width)]
        sub = min(group, ATTN_HEADS)
        kj = lax.broadcasted_iota(jnp.int32, (QBLOCK, sub * QBLOCK), 0)
        qi = lax.broadcasted_iota(jnp.int32, (QBLOCK, sub * QBLOCK), 1) & (QBLOCK - 1)
        for h0 in range(0, group, sub):
            cols = slice(h0 * HEAD_DIM, (h0 + sub) * HEAD_DIM)
            qs = _stack_heads(q_blk[:, cols], sub)
            s = _dot_nt(k_band, qs)
            pieces = []
            if has_prev:
                pieces.append(jnp.where(QBLOCK + qi - kj <= WINDOW, s[:QBLOCK], NEG_INF))
            c0 = QBLOCK * has_prev
            pieces.append(s[c0:c0 + QBLOCK])
            if has_next:
                pieces.append(jnp.where(QBLOCK + kj - qi <= WINDOW, s[c0 + QBLOCK:], NEG_INF))
            s = jnp.concatenate(pieces, axis=0)
            o = softmax_vp([(s, vt_band), (_dot_nt(kc, qs), vct)], sink_q[:, cols])
            for g in range(sub):
                o_ref[0, pl.ds(r0, QBLOCK), (h0 + g) * HEAD_DIM:(h0 + g + 1) * HEAD_DIM] = (
                    o[g * QBLOCK:(g + 1) * QBLOCK].astype(BF16))

    attend(0, False, nblk > 1)
    if nblk > 1:
        attend((nblk - 1) * QBLOCK, True, False)

    def body(n, carry):
        attend(pl.multiple_of(n * QBLOCK, QBLOCK), True, True)
        return carry

    lax.fori_loop(1, nblk - 1, body, 0, unroll=7)

    if need_ctx:
        tc = qc_ref.shape[1]
        o = softmax_vp([(_dot_nt(kc, _stack_heads(qc_ref[0], group)), vct)], sink_row(tc))
        for g in range(group):
            oc_ref[0, :, g * HEAD_DIM:(g + 1) * HEAD_DIM] = o[g * tc:(g + 1) * tc].astype(BF16)


def _attn_call(sink, q, k, vt, qc, kc, vct, need_ctx):
    bsz, t, nq = q.shape
    tc = kc.shape[1]
    group = nq // HEAD_DIM // N_KV_HEADS
    gw = group * HEAD_DIM
    in_specs = [
        pl.BlockSpec(memory_space=pltpu.SMEM),
        pl.BlockSpec((1, t, gw), lambda b, h: (b, 0, h)),
        pl.BlockSpec((1, t, HEAD_DIM), lambda b, h: (b, 0, h)),
        pl.BlockSpec((1, HEAD_DIM, t), lambda b, h: (b, h, 0)),
        pl.BlockSpec((1, tc, HEAD_DIM), lambda b, h: (b, 0, h)),
        pl.BlockSpec((1, HEAD_DIM, tc), lambda b, h: (b, h, 0)),
    ]
    out_specs = [pl.BlockSpec((1, t, gw), lambda b, h: (b, 0, h))]
    out_shape = [jax.ShapeDtypeStruct((bsz, t, nq), BF16)]
    args = [sink, q, k, vt, kc, vct]
    if need_ctx:
        in_specs.append(pl.BlockSpec((1, tc, gw), lambda b, h: (b, 0, h)))
        out_specs.append(pl.BlockSpec((1, tc, gw), lambda b, h: (b, 0, h)))
        out_shape.append(jax.ShapeDtypeStruct((bsz, tc, nq), BF16))
        args.append(qc)
    outs = pl.pallas_call(
        functools.partial(_attn_kernel, group=group, need_ctx=need_ctx),
        grid=(bsz, N_KV_HEADS),
        in_specs=in_specs,
        out_specs=out_specs,
        out_shape=out_shape,
        compiler_params=_cparams(2),
        name="attn",
    )(*args)
    return (outs[0], outs[1]) if need_ctx else (outs[0], None)


def _rope_tables(t):
    half = HEAD_DIM // 2
    freqs = ROPE_BASE ** (-jnp.arange(0, half, 2, dtype=F32) / half)
    pos = jnp.arange(t)
    rows = (pos // GRID_W).astype(F32)[:, None] * freqs
    cols = (pos % GRID_W).astype(F32)[:, None] * freqs
    cos = jnp.concatenate([jnp.cos(rows), jnp.cos(rows), jnp.cos(cols), jnp.cos(cols)], axis=1)
    sin = jnp.concatenate([-jnp.sin(rows), jnp.sin(rows), -jnp.sin(cols), jnp.sin(cols)], axis=1)
    return cos, sin


def _count(mask):
    return jnp.sum(jnp.where(mask, 1.0, 0.0), axis=1, keepdims=True)


def _route_plan(x_ref, sh_ref, sc_ref, ng_ref, rw_ref, h_ref, slot_ref, starts_ref, slot_e_ref,
                gate_e_ref, aff_s, slotp_s, cap):
    t = x_ref.shape[1]
    n_exp = slot_e_ref.shape[1]
    chunk = min(t, ROW_CHUNK)
    rw = rw_ref[0]
    rw_hi = rw.astype(BF16)
    rw_lo = (rw - rw_hi.astype(F32)).astype(BF16)
    lane = lax.broadcasted_iota(jnp.int32, (chunk, LANES), 1)
    for r0 in range(0, t, chunk):
        h = _norm_mod(x_ref[0, r0:r0 + chunk], ng_ref[0], sh_ref[0], sc_ref[0])
        h_hi = h.astype(BF16)
        h_lo = (h - h_hi.astype(F32)).astype(BF16)
        h_ref[0, r0:r0 + chunk] = h_hi
        logits = _dot(h_hi, rw_hi) + (_dot(h_lo, rw_hi) + _dot(h_hi, rw_lo))
        logits = jnp.where(lane < n_exp, logits, NEG_INF)
        ex = jnp.exp(logits - jnp.max(logits, axis=1, keepdims=True))
        aff = ex / jnp.sum(ex, axis=1, keepdims=True)
        aff_s[:, r0:r0 + chunk] = aff.T
    aff_t = aff_s[0:n_exp, :]

    def as_f32(word):
        return lax.bitcast_convert_type(word, F32)

    def search(n_digits, accept):
        def body(i, w):
            shift = SEARCH_BITS * (n_digits - 1 - i)
            digit = jnp.zeros_like(w)
            for c in range(1, 1 << SEARCH_BITS):
                ok = accept(w | jnp.left_shift(jnp.int32(c), shift))
                digit = digit + jnp.where(ok, 1, 0)
            return w | jnp.left_shift(digit, shift)

        return lax.fori_loop(0, n_digits, body, jnp.zeros((n_exp, 1), jnp.int32))

    kth = search(30 // SEARCH_BITS, lambda w: _count(aff_t >= as_f32(w)) >= cap)
    above = aff_t >= as_f32(kth + 1)
    tie = (aff_t >= as_f32(kth)) & jnp.logical_not(above)
    need = cap - _count(above)
    idx = lax.broadcasted_iota(jnp.int32, aff_t.shape, 1)
    idx_digits = -(-max(1, (t - 1).bit_length()) // SEARCH_BITS)
    last = search(idx_digits, lambda w: _count(tie & (idx < w)) < need)
    sel = jnp.where(above | (tie & (idx <= last)), 1.0, 0.0)

    upper = jnp.where(lax.broadcasted_iota(jnp.int32, (chunk, chunk), 0)
                      < lax.broadcasted_iota(jnp.int32, (chunk, chunk), 1), 1.0, 0.0).astype(BF16)
    running = jnp.zeros((n_exp, 1), F32)
    starts = jnp.zeros((n_exp, LANES), F32)
    chunk_id = lax.broadcasted_iota(jnp.int32, (n_exp, LANES), 1)
    slotp_s[...] = jnp.full(slotp_s.shape, -1.0, F32)
    for r0 in range(0, t, chunk):
        starts = jnp.where(chunk_id == r0 // chunk, running, starts)
        sel_c = sel[:, r0:r0 + chunk]
        pos = _dot(sel_c.astype(BF16), upper) + running
        running = running + jnp.sum(sel_c, axis=1, keepdims=True)
        slotp_s[0:n_exp, r0:r0 + chunk] = jnp.where(sel_c > 0.5, pos, -1.0)
    starts_ref[0] = starts
    slot_e_ref[0] = slotp_s[0:n_exp, :]
    gate_e_ref[0] = sel * aff_t
    for r0 in range(0, t, chunk):
        slot_ref[0, r0:r0 + chunk] = slotp_s[:, r0:r0 + chunk].T


N_PLAN_OUTS = 5


def _plan_kernel(*refs, caps):
    ns = len(caps)
    ng_ref, rw_ref = refs[3 * ns:3 * ns + 2]
    outs0 = 3 * ns + 2
    scr0 = outs0 + N_PLAN_OUTS * ns
    for s in range(ns):
        _route_plan(*refs[3 * s:3 * s + 3], ng_ref, rw_ref,
                    *refs[outs0 + N_PLAN_OUTS * s:outs0 + N_PLAN_OUTS * (s + 1)],
                    *refs[scr0 + 2 * s:scr0 + 2 * s + 2], caps[s])


def _plan_call(streams, norm_g, l, router_p, n_exp):
    bsz, _, d = streams[0][0].shape
    vec = lambda b: (b, 0, 0)
    in_specs, args, out_specs, out_shapes, scratch = [], [], [], [], []
    for x, sh, sc, _ in streams:
        t = x.shape[1]
        in_specs += [pl.BlockSpec((1, t, d), vec), pl.BlockSpec((1, 1, d), vec),
                     pl.BlockSpec((1, 1, d), vec)]
        args += [x, sh, sc]
        for shape, dtype in (((t, d), BF16), ((t, LANES), F32), ((n_exp, LANES), F32),
                             ((n_exp, t), F32), ((n_exp, t), F32)):
            out_specs.append(pl.BlockSpec((1,) + shape, vec))
            out_shapes.append(jax.ShapeDtypeStruct((bsz,) + shape, dtype))
        scratch += [pltpu.VMEM((LANES, t), F32), pltpu.VMEM((LANES, t), F32)]
    in_specs += [pl.BlockSpec((1, 1, d), lambda b: (l, 0, 0)),
                 pl.BlockSpec((1, d, LANES), lambda b: (l, 0, 0))]
    outs = pl.pallas_call(
        functools.partial(_plan_kernel, caps=tuple(st[3] for st in streams)),
        grid=(bsz,),
        in_specs=in_specs,
        out_specs=out_specs,
        out_shape=out_shapes,
        scratch_shapes=scratch,
        compiler_params=_cparams(1),
        name="plan",
    )(*args, norm_g, router_p)
    plans = []
    for s, (x, _, _, _) in enumerate(streams):
        h, slot, starts, slot_e, gate_e = outs[N_PLAN_OUTS * s:N_PLAN_OUTS * (s + 1)]
        n_chunks = -(-x.shape[1] // ROW_CHUNK)
        starts = jnp.swapaxes(starts[:, :, :n_chunks].astype(jnp.int32), 1, 2)
        plans.append((h, slot, starts.reshape(bsz, n_chunks * n_exp), slot_e, gate_e))
    return plans


def _gather_stream(b, starts_ref, h_ref, slot_e_ref, gate_e_ref, xg_ref, gs_ref, row0, cap):
    n_exp, t = slot_e_ref.shape[1:]
    d = h_ref.shape[2]
    chunk = min(t, ROW_CHUNK)
    n_chunks = t // chunk
    win = min(cap, SLOT_WINDOW)
    rows_all = slice(row0, row0 + cap)
    for e in range(n_exp):
        xg_ref[e, 0, rows_all] = jnp.zeros((cap, d), BF16)
        gs_ref[e, 0, rows_all] = jnp.zeros((cap, LANES), F32)

    def scatter_rows(e, rows, n_rows, tokens, gates):
        cur = xg_ref[e, 0, rows].astype(F32)
        xg_ref[e, 0, rows] = (cur + tokens).astype(BF16)
        gs_ref[e, 0, rows] = gs_ref[e, 0, rows] + jnp.broadcast_to(gates, (n_rows, LANES))

    for c in range(n_chunks):
        toks = slice(c * chunk, (c + 1) * chunk)
        h_c = h_ref[0, toks]
        if cap > win:
            nxt = min(c + 1, n_chunks - 1)
            starts, misfit = [], jnp.int32(0)
            for e in range(n_exp):
                first = starts_ref[b, c * n_exp + e]
                end = starts_ref[b, nxt * n_exp + e] if c + 1 < n_chunks else cap
                w0 = jnp.minimum(first & -(2 * SUBLANES), cap - win)
                misfit = misfit | jnp.where(end - w0 > win, 1, 0)
                starts.append(w0)
        else:
            starts, misfit = [0] * n_exp, None

        def windowed(toks=toks, h_c=h_c, starts=starts):
            slot_id = lax.broadcasted_iota(jnp.int32, (win, chunk), 0).astype(F32)
            hits = [slot_e_ref[0, e:e + 1, toks] - jnp.float32(starts[e]) == slot_id
                    for e in range(n_exp)]
            onehot = jnp.concatenate([jnp.where(hit, 1.0, 0.0).astype(BF16) for hit in hits], axis=0)
            picked = _dot(onehot, h_c)
            for e in range(n_exp):
                w0 = starts[e] if isinstance(starts[e], int) else pl.multiple_of(starts[e], 2 * SUBLANES)
                gates = jnp.sum(jnp.where(hits[e], gate_e_ref[0, e:e + 1, toks], 0.0),
                                axis=1, keepdims=True)
                scatter_rows(e, pl.ds(row0 + w0, win), win, picked[e * win:(e + 1) * win], gates)

        def dense(toks=toks, h_c=h_c):
            slot_id = lax.broadcasted_iota(jnp.int32, (cap, chunk), 0).astype(F32)
            for e in range(n_exp):
                hit = slot_e_ref[0, e:e + 1, toks] == slot_id
                gates = jnp.sum(jnp.where(hit, gate_e_ref[0, e:e + 1, toks], 0.0),
                                axis=1, keepdims=True)
                scatter_rows(e, rows_all, cap, _dot(jnp.where(hit, 1.0, 0.0).astype(BF16), h_c), gates)

        if misfit is None:
            windowed()
        else:
            pl.when(misfit == 0)(windowed)
            pl.when(misfit != 0)(dense)


def _gather_kernel(*refs, caps):
    ns = len(caps)
    xg_ref, gs_ref = refs[4 * ns:4 * ns + 2]
    b = pl.program_id(0)
    row0 = 0
    for s in range(ns):
        _gather_stream(b, refs[s], *refs[ns + 3 * s:ns + 3 * s + 3], xg_ref, gs_ref, row0, caps[s])
        row0 += caps[s]


def _gather_call(plans, caps, n_exp):
    bsz, _, d = plans[0][0].shape
    rows = sum(caps)
    vec = lambda b, *_: (b, 0, 0)
    in_specs, args = [], []
    for h, _, _, slot_e, gate_e in plans:
        t = h.shape[1]
        in_specs += [pl.BlockSpec((1, t, d), vec), pl.BlockSpec((1, n_exp, t), vec),
                     pl.BlockSpec((1, n_exp, t), vec)]
        args += [h, slot_e, gate_e]
    grid_spec = pltpu.PrefetchScalarGridSpec(
        num_scalar_prefetch=len(plans),
        grid=(bsz,),
        in_specs=in_specs,
        out_specs=[
            pl.BlockSpec((n_exp, 1, rows, d), lambda b, *_: (0, b, 0, 0)),
            pl.BlockSpec((n_exp, 1, rows, LANES), lambda b, *_: (0, b, 0, 0)),
        ],
    )
    return pl.pallas_call(
        functools.partial(_gather_kernel, caps=caps),
        grid_spec=grid_spec,
        out_shape=[
            jax.ShapeDtypeStruct((n_exp, bsz, rows, d), BF16),
            jax.ShapeDtypeStruct((n_exp, bsz, rows, LANES), F32),
        ],
        compiler_params=_cparams(1),
        name="gather",
    )(*[p[2] for p in plans], *args)


def _ffn_kernel(x_ref, gs_ref, wg_ref, wu_ref, wd_ref, y_ref, w_s):
    @pl.when(pl.program_id(1) == 0)
    def _():
        w_s[0] = wg_ref[0, 0].astype(BF16)
        w_s[1] = wu_ref[0, 0].astype(BF16)
        w_s[2] = wd_ref[0, 0].astype(BF16)

    x = x_ref[0]
    f = w_s.shape[2]
    acc = None
    for c0 in range(0, f, FFN_COLS):
        a = _dot(x, w_s[0, :, c0:c0 + FFN_COLS])
        u = _dot(x, w_s[1, :, c0:c0 + FFN_COLS])
        hmid = (a * _sigmoid(a) * u).astype(BF16)
        part = _dot(hmid, w_s[2, c0:c0 + FFN_COLS, :])
        acc = part if acc is None else acc + part
    y_ref[0] = (acc * gs_ref[0][:, 0:1]).astype(BF16)


def _ffn_call(xg, gs, w_gate, w_up, w_down, l):
    n_exp, rows, d = xg.shape
    f = w_gate.shape[-1]
    tm = rows // max(1, -(-rows // FFN_ROWS))
    assert rows % tm == 0 and tm % (2 * SUBLANES) == 0, (rows, tm)
    w_map = lambda e, r: (l, e, 0, 0)
    return pl.pallas_call(
        _ffn_kernel,
        grid=(n_exp, rows // tm),
        in_specs=[
            pl.BlockSpec((1, tm, d), lambda e, r: (e, r, 0)),
            pl.BlockSpec((1, tm, LANES), lambda e, r: (e, r, 0)),
            pl.BlockSpec((1, 1, d, f), w_map),
            pl.BlockSpec((1, 1, d, f), w_map),
            pl.BlockSpec((1, 1, f, d), w_map),
        ],
        out_specs=pl.BlockSpec((1, tm, d), lambda e, r: (e, r, 0)),
        out_shape=jax.ShapeDtypeStruct((n_exp, rows, d), BF16),
        scratch_shapes=[pltpu.VMEM((3, d, f), BF16)],
        compiler_params=_cparams(2),
        name="ffn",
    )(xg, gs, w_gate, w_up, w_down)


def _combine_kernel(starts_ref, y_ref, slot_ref, x_ref, g_ref, fg_ref, o_ref, *, final_norm, windowed):
    n_exp, _, cap, d = y_ref.shape
    tk = x_ref.shape[1]
    tile = min(tk, ROW_CHUNK)
    b = pl.program_id(0)
    tile0 = pl.program_id(1) * (tk // tile)
    n_tiles = pl.num_programs(1) * (tk // tile)

    for sub in range(tk // tile):
        rows = slice(sub * tile, (sub + 1) * tile)
        slots = slot_ref[0, rows]

        def finish(acc, rows=rows):
            out = x_ref[0, rows] + g_ref[0] * acc
            if final_norm:
                ms = jnp.mean(out * out, axis=-1, keepdims=True)
                out = out * lax.rsqrt(ms + NORM_EPS) * fg_ref[...]
            o_ref[0, rows] = out

        def dense(slots=slots, finish=finish):
            lane = lax.broadcasted_iota(jnp.int32, (tile, cap), 1).astype(F32)
            acc = jnp.zeros((tile, d), F32)
            for e in range(n_exp):
                onehot = jnp.where(lane == slots[:, e:e + 1], 1.0, 0.0).astype(BF16)
                acc = acc + _dot(onehot, y_ref[e, 0])
            finish(acc)

        if not windowed:
            dense()
            continue

        t_idx = tile0 + sub
        nxt = jnp.minimum(t_idx + 1, n_tiles - 1)
        win, misfit = [], jnp.int32(0)
        for e in range(n_exp):
            first = starts_ref[b, t_idx * n_exp + e]
            end = jnp.where(t_idx + 1 < n_tiles, starts_ref[b, nxt * n_exp + e], cap)
            w0 = jnp.minimum(first & -(2 * SUBLANES), cap - SLOT_WINDOW)
            misfit = misfit | jnp.where(end - w0 > SLOT_WINDOW, 1, 0)
            win.append(w0)

        def sparse(slots=slots, finish=finish, win=win):
            lane = lax.broadcasted_iota(jnp.int32, (tile, LANES), 1)
            low = lane < SLOT_WINDOW
            pos = (lane & (SLOT_WINDOW - 1)).astype(F32)
            per_dot = 2 * LANES // SLOT_WINDOW
            acc = jnp.zeros((tile, d), F32)
            for e0 in range(0, n_exp, per_dot):
                halves = []
                for e in range(e0, e0 + per_dot, 2):
                    rel_a = slots[:, e:e + 1] - win[e].astype(F32)
                    rel_b = slots[:, e + 1:e + 2] - win[e + 1].astype(F32)
                    hit = jnp.where(low, rel_a, rel_b) == pos
                    halves.append(jnp.where(hit, 1.0, 0.0).astype(BF16))
                yk = jnp.concatenate(
                    [y_ref[e, 0, pl.ds(pl.multiple_of(win[e], 2 * SUBLANES), SLOT_WINDOW), :]
                     for e in range(e0, e0 + per_dot)], axis=0)
                acc = acc + _dot(jnp.concatenate(halves, axis=1), yk)
            finish(acc)

        pl.when(misfit == 0)(sparse)
        pl.when(misfit != 0)(dense)


def _combine_call(y, slot, starts, x, gate, final_g, cap, row0, final_norm):
    bsz, t, d = x.shape
    n_exp = y.shape[0]
    tk = min(t, COMBINE_ROWS)
    assert row0 % cap == 0
    windowed = (cap >= SLOT_WINDOW and n_exp % (2 * LANES // SLOT_WINDOW) == 0
                and t % ROW_CHUNK == 0)
    grid_spec = pltpu.PrefetchScalarGridSpec(
        num_scalar_prefetch=1,
        grid=(bsz, t // tk),
        in_specs=[
            pl.BlockSpec((n_exp, 1, cap, d), lambda b, i, st: (0, b, row0 // cap, 0)),
            pl.BlockSpec((1, tk, LANES), lambda b, i, st: (b, i, 0)),
            pl.BlockSpec((1, tk, d), lambda b, i, st: (b, i, 0)),
            pl.BlockSpec((1, 1, d), lambda b, i, st: (b, 0, 0)),
            pl.BlockSpec((1, d), lambda b, i, st: (0, 0)),
        ],
        out_specs=pl.BlockSpec((1, tk, d), lambda b, i, st: (b, i, 0)),
    )
    return pl.pallas_call(
        functools.partial(_combine_kernel, final_norm=final_norm, windowed=windowed),
        grid_spec=grid_spec,
        out_shape=jax.ShapeDtypeStruct(x.shape, F32),
        compiler_params=_cparams(2),
        name="combine",
    )(starts, y, slot, x, gate, final_g)


def kernel(x, c, ctx, c_ctx, ada_w, ada_b, norm1_g, norm2_g, lru_w_in, lru_conv_w, lru_conv_b,
           lru_gate_w, lru_gate_b, lru_lambda, lru_w_out, attn_w_qkv, attn_sink, attn_w_o,
           moe_router, moe_w_gate, moe_w_up, moe_w_down, final_g):
    bsz, t, d = x.shape
    tc = ctx.shape[1]
    depth = ada_w.shape[0]
    n_exp = moe_router.shape[-1]
    cap_l = CAPACITY_FACTOR * t // n_exp
    cap_c = CAPACITY_FACTOR * tc // n_exp

    cond_rows = 2 * SUBLANES
    cond = jnp.zeros((cond_rows, d), F32).at[:bsz].set(c).at[bsz].set(c_ctx)
    mod = _ada_call(cond, ada_w, ada_b).reshape(depth, cond_rows, 6, 1, d)
    norm1 = norm1_g.reshape(depth, 1, d)
    norm2 = norm2_g.reshape(depth, 1, d)
    router_p = jnp.pad(moe_router, ((0, 0), (0, 0), (0, LANES - n_exp)))
    final_g2 = final_g.reshape(1, d)
    cos, sin_signed = _rope_tables(t)
    ones_c = jnp.ones((tc, HEAD_DIM), F32)

    for l in range(depth):
        need_ctx = l < depth - 1
        lat = [mod[l, :bsz, i] for i in range(6)]
        con = [jnp.broadcast_to(mod[l, bsz, i], (bsz, 1, d)) for i in range(6)]
        j = l // 2
        if l % 2 == 0:
            s_l, s_c = _lru_call(x, ctx, lat[0], lat[1], con[0], con[1], norm1, l,
                                 lru_w_in[j].astype(BF16), lru_conv_w[j], lru_conv_b[j],
                                 (0.5 * lru_gate_w[j]).astype(BF16), 0.5 * lru_gate_b[j],
                                 lru_lambda[j])
            w_out = lru_w_out[j].astype(BF16)
        else:
            w_qkv = attn_w_qkv[j].astype(BF16)
            q, k, v = _qkv_call(x, lat[0], lat[1], norm1, l, w_qkv, cos, sin_signed, True)
            qc, kc, vc = _qkv_call(ctx, con[0], con[1], norm1, l, w_qkv, ones_c, ones_c, False)
            s_l, s_c = _attn_call(attn_sink[j], q, k, v, qc, kc, vc, need_ctx)
            w_out = attn_w_o[j].astype(BF16)
        x = _resid_call(s_l, w_out, x, lat[2])
        if need_ctx:
            ctx = _resid_call(s_c, w_out, ctx, con[2])

        streams = [(x, lat[3], lat[4], cap_l)]
        if need_ctx:
            streams.append((ctx, con[3], con[4], cap_c))
        plans = _plan_call(streams, norm2, l, router_p, n_exp)
        xg, gs = _gather_call(plans, tuple(st[3] for st in streams), n_exp)
        rows = xg.shape[2]
        y = _ffn_call(xg.reshape(n_exp, bsz * rows, d), gs.reshape(n_exp, bsz * rows, LANES),
                      moe_w_gate, moe_w_up, moe_w_down, l).reshape(xg.shape)
        x = _combine_call(y, plans[0][1], plans[0][2], x, lat[5], final_g2, cap_l, 0, not need_ctx)
        if need_ctx:
            ctx = _combine_call(y, plans[1][1], plans[1][2], ctx, con[5], final_g2, cap_c, cap_l,
                                False)
    return x
```

```python
import functools
import math

import jax
import jax.numpy as jnp
from jax import lax
from jax.experimental import pallas as pl
from jax.experimental.pallas import tpu as pltpu

F32 = jnp.float32
BF16 = jnp.bfloat16

LANES = 128
SUBLANES = 8
VMEM_LIMIT_BYTES = 58 * 1024 * 1024

NORM_EPS = 1e-6
NEG_INF = -1e30
LRU_C = 8.0
LRU_BLOCK_W = 128
CONV_W = 4
CONV_HALO = 16
LRU_ROWS = 256
HEAD_DIM = 128
N_KV_HEADS = 2
QBLOCK = 128
WINDOW = 128
GRID_W = 64
ROPE_BASE = 10000.0
CAPACITY_FACTOR = 2

ROW_CHUNK = 256
PROJ_ROWS = 1024
ATTN_HEADS = 4
SCAN_UNROLL = 8
MOD_TILE = 1536
SEARCH_BITS = 3
FFN_ROWS = 1152
FFN_COLS = 256
COMBINE_ROWS = 1024
SLOT_WINDOW = LANES // 2


def _cparams(n_axes):
    return pltpu.CompilerParams(
        dimension_semantics=("arbitrary",) * n_axes, vmem_limit_bytes=VMEM_LIMIT_BYTES)


def _dot(a, b):
    return jnp.dot(a, b, preferred_element_type=F32)


def _dot_nt(a, b):
    return lax.dot_general(a, b, (((1,), (1,)), ((), ())), preferred_element_type=F32)


def _sigmoid(x):
    return 0.5 * jnp.tanh(0.5 * x) + 0.5


def _sqrt_nonneg(z):
    return jnp.where(z > 0.0, z * lax.rsqrt(z), 0.0)


def _norm_mod(x, g, shift, scale):
    ms = jnp.mean(x * x, axis=-1, keepdims=True)
    y = x * lax.rsqrt(ms + NORM_EPS) * g
    return y * (1.0 + scale) + shift


def _ada_kernel(c_ref, w_ref, b_ref, o_ref):
    c = c_ref[...]
    s = (c * _sigmoid(c)).astype(BF16)
    o_ref[0] = _dot(s, w_ref[0].astype(BF16)) + b_ref[0]


def _ada_call(cond, ada_w, ada_b):
    n_layers, d, n_out = ada_w.shape
    rows = cond.shape[0]
    return pl.pallas_call(
        _ada_kernel,
        grid=(n_layers, n_out // MOD_TILE),
        in_specs=[
            pl.BlockSpec((rows, d), lambda l, n: (0, 0)),
            pl.BlockSpec((1, d, MOD_TILE), lambda l, n: (l, 0, n)),
            pl.BlockSpec((1, 1, MOD_TILE), lambda l, n: (l, 0, n)),
        ],
        out_specs=pl.BlockSpec((1, rows, MOD_TILE), lambda l, n: (l, 0, n)),
        out_shape=jax.ShapeDtypeStruct((n_layers, rows, n_out), F32),
        compiler_params=_cparams(2),
        name="ada",
    )(cond, ada_w, ada_b.reshape(n_layers, 1, n_out))


def _resid_kernel(a_ref, w_ref, x_ref, g_ref, o_ref):
    o_ref[0] = x_ref[0] + g_ref[0] * _dot(a_ref[0], w_ref[...])


def _resid_call(act, w, x, gate):
    bsz, t, d = x.shape
    k = act.shape[-1]
    tm = min(t, PROJ_ROWS)
    return pl.pallas_call(
        _resid_kernel,
        grid=(bsz, t // tm),
        in_specs=[
            pl.BlockSpec((1, tm, k), lambda b, i: (b, i, 0)),
            pl.BlockSpec((k, d), lambda b, i: (0, 0)),
            pl.BlockSpec((1, tm, d), lambda b, i: (b, i, 0)),
            pl.BlockSpec((1, 1, d), lambda b, i: (b, 0, 0)),
        ],
        out_specs=pl.BlockSpec((1, tm, d), lambda b, i: (b, i, 0)),
        out_shape=jax.ShapeDtypeStruct(x.shape, F32),
        compiler_params=_cparams(2),
        name="resid",
    )(act, w, x, gate)


LRU_CB = 2 * LRU_BLOCK_W


def _log_sigmoid(x):
    return jnp.minimum(x, 0.0) - jnp.log1p(jnp.exp(-jnp.abs(x)))


def _conv_time(u, cw, cb):
    n = u.shape[0]
    left = (CONV_W - 1) // 2
    acc = cb + u * cw[left:left + 1]
    for k in range(CONV_W):
        off = k - left
        if off != 0:
            acc = acc + pltpu.roll(u, (-off) % n, 0) * cw[k:k + 1]
    return acc[CONV_HALO:n - CONV_HALO]


N_SEG = SUBLANES
SEG_SLACK = N_SEG * 2 * SUBLANES


def _segments(n):
    seg = n // N_SEG
    assert n % (N_SEG * SUBLANES) == 0, n
    pad = SUBLANES if (seg // SUBLANES) % 2 == 0 else 2 * SUBLANES
    return seg, seg + pad


def _lru_kernel(xl_ref, xc_ref, shl_ref, scl_ref, shc_ref, scc_ref, ng_ref, wy_ref, wx_ref,
                cw_ref, cb_ref, gw_ref, gb_ref, lam_ref, ol_ref, oc_ref,
                hl_s, hc_s, a_s, b_s, hs_s, ps_s, y_s, o_s):
    j = pl.program_id(1)
    t_lat = xl_ref.shape[1]
    t_ctx = xc_ref.shape[1]
    gbk = LRU_CB // LRU_BLOCK_W

    @pl.when(j == 0)
    def _():
        g = ng_ref[0]
        for x_ref, h_s, sh_ref, sc_ref in ((xl_ref, hl_s, shl_ref, scl_ref),
                                           (xc_ref, hc_s, shc_ref, scc_ref)):
            n = x_ref.shape[1]
            step = min(n, ROW_CHUNK)
            halo = jnp.zeros((CONV_HALO, h_s.shape[1]), BF16)
            h_s[0:CONV_HALO] = halo
            for r0 in range(0, n, step):
                h_s[CONV_HALO + r0:CONV_HALO + r0 + step] = _norm_mod(
                    x_ref[0, r0:r0 + step], g, sh_ref[0], sc_ref[0]).astype(BF16)
            h_s[CONV_HALO + n:2 * CONV_HALO + n] = halo

    half_rate = (0.5 * LRU_C / math.log(2.0)) * _log_sigmoid(lam_ref[...])

    def branches(h_s, n):
        seg, pitch = _segments(n)
        rows = min(n, LRU_ROWS)
        assert n % rows == 0 and rows % seg == 0, (n, rows, seg)
        for r0 in range(0, n, rows):
            branch_rows(h_s, r0, rows, seg, pitch)

    def branch_rows(h_s, r0, rows, seg, pitch):
        def put(store, val):
            for s in range(r0 // seg, (r0 + rows) // seg):
                store(slice(s * pitch, s * pitch + seg), val[s * seg - r0:(s + 1) * seg - r0])

        y = jax.nn.gelu(_dot(h_s[CONV_HALO + r0:CONV_HALO + r0 + rows], wy_ref[...]))
        xb = _conv_time(_dot(h_s[r0:r0 + rows + 2 * CONV_HALO], wx_ref[...]),
                        cw_ref[...], cb_ref[...])
        for kb in range(gbk):
            lanes = slice(kb * LRU_BLOCK_W, (kb + 1) * LRU_BLOCK_W)

            def put_y(rows, v, kb=kb):
                y_s[kb, rows] = v

            put(put_y, y[:, lanes])
            xk = xb[:, lanes]
            xk16 = xk.astype(BF16)
            half_x = 0.5 * xk
            for d in range(2):
                th = jnp.tanh(_dot(xk16, gw_ref[d, kb]) + gb_ref[d, kb])
                th_r = th[:, :LRU_BLOCK_W]
                th_i = th[:, LRU_BLOCK_W:]
                a = jnp.exp2(th_r * half_rate[d:d + 1, lanes] + half_rate[d:d + 1, lanes])

                def put_a(rows, v, d=d, kb=kb):
                    a_s[d, kb, rows] = v

                def put_b(rows, v, d=d, kb=kb):
                    b_s[d, kb, rows] = v

                put(put_a, a)
                put(put_b, _sqrt_nonneg(1.0 - a * a) * (th_i * half_x + half_x))

    def scan(n, carry_in, o_ref):
        seg, pitch = _segments(n)
        chains = [(d, kb) for d in range(2) for kb in range(gbk)]

        def advance(jstep, state):
            out = []
            for c, (d, kb) in enumerate(chains):
                jj = (seg - 1 - jstep) if d == 1 else jstep
                h, p = state[2 * c], state[2 * c + 1]
                a = a_s[d, kb, pl.ds(jj, N_SEG, stride=pitch), :]
                b = b_s[d, kb, pl.ds(jj, N_SEG, stride=pitch), :]
                h = a * h + b
                p = p * a
                rows = pl.ds(pl.multiple_of(jj * N_SEG, N_SEG), N_SEG)
                hs_s[d, kb, rows] = h
                ps_s[d, kb, rows] = p
                out += [h, p]
            return tuple(out)

        init = (jnp.zeros((N_SEG, LRU_BLOCK_W), F32), jnp.ones((N_SEG, LRU_BLOCK_W), F32))
        state = lax.fori_loop(0, seg, advance, init * len(chains), unroll=SCAN_UNROLL)

        row = lax.broadcasted_iota(jnp.int32, (N_SEG, LRU_BLOCK_W), 0)
        enter, carry_out = [], []
        for c, (d, kb) in enumerate(chains):
            h_end, p_end = state[2 * c], state[2 * c + 1]
            cur = carry_in[c]
            vec = jnp.zeros((N_SEG, LRU_BLOCK_W), F32)
            for s in (range(N_SEG - 1, -1, -1) if d == 1 else range(N_SEG)):
                vec = jnp.where(row == s, cur, vec)
                cur = h_end[s:s + 1] + p_end[s:s + 1] * cur
            enter.append(vec)
            carry_out.append(cur)

        def finish(jstep, carry):
            rows = pl.ds(pl.multiple_of(jstep * N_SEG, N_SEG), N_SEG)
            strided = pl.ds(jstep, N_SEG, stride=pitch)
            for kb in range(gbk):
                tot = None
                for c, (d, kb2) in enumerate(chains):
                    if kb2 == kb:
                        part = hs_s[d, kb, rows] + ps_s[d, kb, rows] * enter[c]
                        tot = part if tot is None else tot + part
                o_s[kb, strided, :] = tot * y_s[kb, strided, :]
            return carry

        lax.fori_loop(0, seg, finish, 0, unroll=SCAN_UNROLL)
        for kb in range(gbk):
            for s in range(N_SEG):
                o_ref[0, s * seg:(s + 1) * seg, kb * LRU_BLOCK_W:(kb + 1) * LRU_BLOCK_W] = (
                    o_s[kb, s * pitch:s * pitch + seg].astype(BF16))
        return carry_out

    branches(hc_s, t_ctx)
    carries = scan(t_ctx, [jnp.zeros((1, LRU_BLOCK_W), F32)] * (2 * gbk), oc_ref)
    branches(hl_s, t_lat)
    scan(t_lat, carries, ol_ref)


def _lru_call(x, ctx, sh_l, sc_l, sh_c, sc_c, norm_g, l, w_in, conv_w, conv_b, gate_w, gate_b, lam):
    bsz, t, d = x.shape
    tc = ctx.shape[1]
    width = w_in.shape[1] // 2
    nblk = width // LRU_CB
    gbk = LRU_CB // LRU_BLOCK_W
    vec = lambda b, c: (b, 0, 0)
    return pl.pallas_call(
        _lru_kernel,
        grid=(bsz, nblk),
        in_specs=[
            pl.BlockSpec((1, t, d), vec),
            pl.BlockSpec((1, tc, d), vec),
            pl.BlockSpec((1, 1, d), vec),
            pl.BlockSpec((1, 1, d), vec),
            pl.BlockSpec((1, 1, d), vec),
            pl.BlockSpec((1, 1, d), vec),
            pl.BlockSpec((1, 1, d), lambda b, c: (l, 0, 0)),
            pl.BlockSpec((d, LRU_CB), lambda b, c: (0, c)),
            pl.BlockSpec((d, LRU_CB), lambda b, c: (0, nblk + c)),
            pl.BlockSpec((CONV_W, LRU_CB), lambda b, c: (0, c)),
            pl.BlockSpec((1, LRU_CB), lambda b, c: (0, c)),
            pl.BlockSpec((2, gbk, LRU_BLOCK_W, 2 * LRU_BLOCK_W), lambda b, c: (0, c, 0, 0)),
            pl.BlockSpec((2, gbk, 1, 2 * LRU_BLOCK_W), lambda b, c: (0, c, 0, 0)),
            pl.BlockSpec((2, LRU_CB), lambda b, c: (0, c)),
        ],
        out_specs=[
            pl.BlockSpec((1, t, LRU_CB), lambda b, c: (b, 0, c)),
            pl.BlockSpec((1, tc, LRU_CB), lambda b, c: (b, 0, c)),
        ],
        out_shape=[
            jax.ShapeDtypeStruct((bsz, t, width), BF16),
            jax.ShapeDtypeStruct((bsz, tc, width), BF16),
        ],
        scratch_shapes=[
            pltpu.VMEM((t + 2 * CONV_HALO, d), BF16),
            pltpu.VMEM((tc + 2 * CONV_HALO, d), BF16),
            pltpu.VMEM((2, gbk, t + SEG_SLACK, LRU_BLOCK_W), F32),
            pltpu.VMEM((2, gbk, t + SEG_SLACK, LRU_BLOCK_W), F32),
            pltpu.VMEM((2, gbk, t, LRU_BLOCK_W), F32),
            pltpu.VMEM((2, gbk, t, LRU_BLOCK_W), F32),
            pltpu.VMEM((gbk, t + SEG_SLACK, LRU_BLOCK_W), F32),
            pltpu.VMEM((gbk, t + SEG_SLACK, LRU_BLOCK_W), F32),
        ],
        compiler_params=_cparams(2),
        name="lru",
    )(x, ctx, sh_l, sc_l, sh_c, sc_c, norm_g, w_in, w_in, conv_w, conv_b.reshape(1, width),
      gate_w, gate_b.reshape(2, -1, 1, 2 * LRU_BLOCK_W), lam)


def _rope(x, cos, sin_signed):
    lane = lax.broadcasted_iota(jnp.int32, x.shape, 1)
    quarter = HEAD_DIM // 4
    partner = jnp.where((lane & quarter) == 0,
                        pltpu.roll(x, HEAD_DIM - quarter, 1), pltpu.roll(x, quarter, 1))
    return x * cos + partner * sin_signed


def _qkv_kernel(x_ref, sh_ref, sc_ref, ng_ref, w_ref, cos_ref, sin_ref, q_ref, k_ref, vt_ref,
                *, n_heads, rotary):
    h = _norm_mod(x_ref[0], ng_ref[0], sh_ref[0], sc_ref[0]).astype(BF16)
    u = _dot(h, w_ref[...])
    scale = HEAD_DIM ** -0.5
    nq = n_heads * HEAD_DIM
    nk = N_KV_HEADS * HEAD_DIM
    for hd in range(n_heads + N_KV_HEADS):
        c = u[:, hd * HEAD_DIM:(hd + 1) * HEAD_DIM]
        if rotary:
            c = _rope(c, cos_ref[...], sin_ref[...])
        if hd < n_heads:
            q_ref[0, :, hd * HEAD_DIM:(hd + 1) * HEAD_DIM] = (c * scale).astype(BF16)
        else:
            kk = hd - n_heads
            k_ref[0, :, kk * HEAD_DIM:(kk + 1) * HEAD_DIM] = c.astype(BF16)
    vt_ref[0] = u[:, nq + nk:].T.astype(BF16)


def _qkv_call(x, sh, sc, norm_g, l, w_qkv, cos, sin_signed, rotary):
    bsz, t, d = x.shape
    nk = N_KV_HEADS * HEAD_DIM
    nq = w_qkv.shape[1] - 2 * nk
    tm = min(t, PROJ_ROWS)
    vec = lambda b, i: (b, 0, 0)
    return pl.pallas_call(
        functools.partial(_qkv_kernel, n_heads=nq // HEAD_DIM, rotary=rotary),
        grid=(bsz, t // tm),
        in_specs=[
            pl.BlockSpec((1, tm, d), lambda b, i: (b, i, 0)),
            pl.BlockSpec((1, 1, d), vec),
            pl.BlockSpec((1, 1, d), vec),
            pl.BlockSpec((1, 1, d), lambda b, i: (l, 0, 0)),
            pl.BlockSpec(w_qkv.shape, lambda b, i: (0, 0)),
            pl.BlockSpec((tm, HEAD_DIM), lambda b, i: (i, 0)),
            pl.BlockSpec((tm, HEAD_DIM), lambda b, i: (i, 0)),
        ],
        out_specs=[
            pl.BlockSpec((1, tm, nq), lambda b, i: (b, i, 0)),
            pl.BlockSpec((1, tm, nk), lambda b, i: (b, i, 0)),
            pl.BlockSpec((1, nk, tm), lambda b, i: (b, 0, i)),
        ],
        out_shape=[
            jax.ShapeDtypeStruct((bsz, t, nq), BF16),
            jax.ShapeDtypeStruct((bsz, t, nk), BF16),
            jax.ShapeDtypeStruct((bsz, nk, t), BF16),
        ],
        compiler_params=_cparams(2),
        name="qkv",
    )(x, sh, sc, norm_g, w_qkv, cos, sin_signed)


def _stack_heads(x, group):
    return jnp.concatenate([x[:, g * HEAD_DIM:(g + 1) * HEAD_DIM] for g in range(group)], axis=0)


def _attn_kernel(sink_ref, q_ref, k_ref, vt_ref, kc_ref, vct_ref, *rest, group, need_ctx):
    if need_ctx:
        qc_ref, o_ref, oc_ref = rest
    else:
        (o_ref,) = rest
    kh = pl.program_id(1)
    t = k_ref.shape[1]
    nblk = t // QBLOCK

    def sink_row(cols_per_head):
        col = lax.broadcasted_iota(jnp.int32, (1, group * cols_per_head), 1)
        row = jnp.zeros((1, group * cols_per_head), F32)
        for g in range(group):
            row = jnp.where(col >= g * cols_per_head, sink_ref[kh * group + g], row)
        return row

    kc = kc_ref[0]
    vct = vct_ref[0]
    sink_q = sink_row(QBLOCK)

    def softmax_vp(scores_values, sink):
        m = sink
        for sc, _ in scores_values:
            m = jnp.maximum(m, jnp.max(sc, axis=0, keepdims=True))
        den = jnp.exp(sink - m)
        acc = None
        for sc, val in scores_values:
            p = jnp.exp(sc - m)
            den = den + jnp.sum(p, axis=0, keepdims=True)
            vp = _dot(val, p.astype(BF16))
            acc = vp if acc is None else acc + vp
        return (acc / den).T

    def attend(r0, has_prev, has_next):
        lo = r0 - QBLOCK if has_prev else r0
        if not isinstance(lo, int):
            lo = pl.multiple_of(lo, QBLOCK)
        width = QBLOCK * (1 + has_prev + has_next)
        q_blk = q_ref[0, pl.ds(r0, QBLOCK), :]
        k_band = k_ref[0, pl.ds(lo, width), :]
        vt_band = vt_ref[0, :, pl.ds(lo, width)]
        sub = min(group, ATTN_HEADS)
        kj = lax.broadcasted_iota(jnp.int32, (QBLOCK, sub * QBLOCK), 0)
        qi = lax.broadcasted_iota(jnp.int32, (QBLOCK, sub * QBLOCK), 1) & (QBLOCK - 1)
        for h0 in range(0, group, sub):
            cols = slice(h0 * HEAD_DIM, (h0 + sub) * HEAD_DIM)
            qs = _stack_heads(q_blk[:, cols], sub)
            s = _dot_nt(k_band, qs)
            pieces = []
            if has_prev:
                pieces.append(jnp.where(QBLOCK + qi - kj <= WINDOW, s[:QBLOCK], NEG_INF))
            c0 = QBLOCK * has_prev
            pieces.append(s[c0:c0 + QBLOCK])
            if has_next:
                pieces.append(jnp.where(QBLOCK + kj - qi <= WINDOW, s[c0 + QBLOCK:], NEG_INF))
            s = jnp.concatenate(pieces, axis=0)
            o = softmax_vp([(s, vt_band), (_dot_nt(kc, qs), vct)], sink_q[:, cols])
            for g in range(sub):
                o_ref[0, pl.ds(r0, QBLOCK), (h0 + g) * HEAD_DIM:(h0 + g + 1) * HEAD_DIM] = (
                    o[g * QBLOCK:(g + 1) * QBLOCK].astype(BF16))

    attend(0, False, nblk > 1)
    if nblk > 1:
        attend((nblk - 1) * QBLOCK, True, False)

    def body(n, carry):
        attend(pl.multiple_of(n * QBLOCK, QBLOCK), True, True)
        return carry

    lax.fori_loop(1, nblk - 1, body, 0, unroll=7)

    if need_ctx:
        tc = qc_ref.shape[1]
        o = softmax_vp([(_dot_nt(kc, _stack_heads(qc_ref[0], group)), vct)], sink_row(tc))
        for g in range(group):
            oc_ref[0, :, g * HEAD_DIM:(g + 1) * HEAD_DIM] = o[g * tc:(g + 1) * tc].astype(BF16)


def _attn_call(sink, q, k, vt, qc, kc, vct, need_ctx):
    bsz, t, nq = q.shape
    tc = kc.shape[1]
    group = nq // HEAD_DIM // N_KV_HEADS
    gw = group * HEAD_DIM
    in_specs = [
        pl.BlockSpec(memory_space=pltpu.SMEM),
        pl.BlockSpec((1, t, gw), lambda b, h: (b, 0, h)),
        pl.BlockSpec((1, t, HEAD_DIM), lambda b, h: (b, 0, h)),
        pl.BlockSpec((1, HEAD_DIM, t), lambda b, h: (b, h, 0)),
        pl.BlockSpec((1, tc, HEAD_DIM), lambda b, h: (b, 0, h)),
        pl.BlockSpec((1, HEAD_DIM, tc), lambda b, h: (b, h, 0)),
    ]
    out_specs = [pl.BlockSpec((1, t, gw), lambda b, h: (b, 0, h))]
    out_shape = [jax.ShapeDtypeStruct((bsz, t, nq), BF16)]
    args = [sink, q, k, vt, kc, vct]
    if need_ctx:
        in_specs.append(pl.BlockSpec((1, tc, gw), lambda b, h: (b, 0, h)))
        out_specs.append(pl.BlockSpec((1, tc, gw), lambda b, h: (b, 0, h)))
        out_shape.append(jax.ShapeDtypeStruct((bsz, tc, nq), BF16))
        args.append(qc)
    outs = pl.pallas_call(
        functools.partial(_attn_kernel, group=group, need_ctx=need_ctx),
        grid=(bsz, N_KV_HEADS),
        in_specs=in_specs,
        out_specs=out_specs,
        out_shape=out_shape,
        compiler_params=_cparams(2),
        name="attn",
    )(*args)
    return (outs[0], outs[1]) if need_ctx else (outs[0], None)


def _rope_tables(t):
    half = HEAD_DIM // 2
    freqs = ROPE_BASE ** (-jnp.arange(0, half, 2, dtype=F32) / half)
    pos = jnp.arange(t)
    rows = (pos // GRID_W).astype(F32)[:, None] * freqs
    cols = (pos % GRID_W).astype(F32)[:, None] * freqs
    cos = jnp.concatenate([jnp.cos(rows), jnp.cos(rows), jnp.cos(cols), jnp.cos(cols)], axis=1)
    sin = jnp.concatenate([-jnp.sin(rows), jnp.sin(rows), -jnp.sin(cols), jnp.sin(cols)], axis=1)
    return cos, sin


def _count(mask):
    return jnp.sum(jnp.where(mask, 1.0, 0.0), axis=1, keepdims=True)


def _route_plan(x_ref, sh_ref, sc_ref, ng_ref, rw_ref, h_ref, slot_ref, starts_ref, slot_e_ref,
                gate_e_ref, aff_s, slotp_s, cap):
    t = x_ref.shape[1]
    n_exp = slot_e_ref.shape[1]
    chunk = min(t, ROW_CHUNK)
    rw = rw_ref[0]
    rw_hi = rw.astype(BF16)
    rw_lo = (rw - rw_hi.astype(F32)).astype(BF16)
    lane = lax.broadcasted_iota(jnp.int32, (chunk, LANES), 1)
    for r0 in range(0, t, chunk):
        h = _norm_mod(x_ref[0, r0:r0 + chunk], ng_ref[0], sh_ref[0], sc_ref[0])
        h_hi = h.astype(BF16)
        h_lo = (h - h_hi.astype(F32)).astype(BF16)
        h_ref[0, r0:r0 + chunk] = h_hi
        logits = _dot(h_hi, rw_hi) + (_dot(h_lo, rw_hi) + _dot(h_hi, rw_lo))
        logits = jnp.where(lane < n_exp, logits, NEG_INF)
        ex = jnp.exp(logits - jnp.max(logits, axis=1, keepdims=True))
        aff = ex / jnp.sum(ex, axis=1, keepdims=True)
        aff_s[:, r0:r0 + chunk] = aff.T
    aff_t = aff_s[0:n_exp, :]

    def as_f32(word):
        return lax.bitcast_convert_type(word, F32)

    def search(n_digits, accept):
        def body(i, w):
            shift = SEARCH_BITS * (n_digits - 1 - i)
            digit = jnp.zeros_like(w)
            for c in range(1, 1 << SEARCH_BITS):
                ok = accept(w | jnp.left_shift(jnp.int32(c), shift))
                digit = digit + jnp.where(ok, 1, 0)
            return w | jnp.left_shift(digit, shift)

        return lax.fori_loop(0, n_digits, body, jnp.zeros((n_exp, 1), jnp.int32))

    kth = search(30 // SEARCH_BITS, lambda w: _count(aff_t >= as_f32(w)) >= cap)
    above = aff_t >= as_f32(kth + 1)
    tie = (aff_t >= as_f32(kth)) & jnp.logical_not(above)
    need = cap - _count(above)
    idx = lax.broadcasted_iota(jnp.int32, aff_t.shape, 1)
    idx_digits = -(-max(1, (t - 1).bit_length()) // SEARCH_BITS)
    last = search(idx_digits, lambda w: _count(tie & (idx < w)) < need)
    sel = jnp.where(above | (tie & (idx <= last)), 1.0, 0.0)

    upper = jnp.where(lax.broadcasted_iota(jnp.int32, (chunk, chunk), 0)
                      < lax.broadcasted_iota(jnp.int32, (chunk, chunk), 1), 1.0, 0.0).astype(BF16)
    running = jnp.zeros((n_exp, 1), F32)
    starts = jnp.zeros((n_exp, LANES), F32)
    chunk_id = lax.broadcasted_iota(jnp.int32, (n_exp, LANES), 1)
    slotp_s[...] = jnp.full(slotp_s.shape, -1.0, F32)
    for r0 in range(0, t, chunk):
        starts = jnp.where(chunk_id == r0 // chunk, running, starts)
        sel_c = sel[:, r0:r0 + chunk]
        pos = _dot(sel_c.astype(BF16), upper) + running
        running = running + jnp.sum(sel_c, axis=1, keepdims=True)
        slotp_s[0:n_exp, r0:r0 + chunk] = jnp.where(sel_c > 0.5, pos, -1.0)
    starts_ref[0] = starts
    slot_e_ref[0] = slotp_s[0:n_exp, :]
    gate_e_ref[0] = sel * aff_t
    for r0 in range(0, t, chunk):
        slot_ref[0, r0:r0 + chunk] = slotp_s[:, r0:r0 + chunk].T


N_PLAN_OUTS = 5


def _plan_kernel(*refs, caps):
    ns = len(caps)
    ng_ref, rw_ref = refs[3 * ns:3 * ns + 2]
    outs0 = 3 * ns + 2
    scr0 = outs0 + N_PLAN_OUTS * ns
    for s in range(ns):
        _route_plan(*refs[3 * s:3 * s + 3], ng_ref, rw_ref,
                    *refs[outs0 + N_PLAN_OUTS * s:outs0 + N_PLAN_OUTS * (s + 1)],
                    *refs[scr0 + 2 * s:scr0 + 2 * s + 2], caps[s])


def _plan_call(streams, norm_g, l, router_p, n_exp):
    bsz, _, d = streams[0][0].shape
    vec = lambda b: (b, 0, 0)
    in_specs, args, out_specs, out_shapes, scratch = [], [], [], [], []
    for x, sh, sc, _ in streams:
        t = x.shape[1]
        in_specs += [pl.BlockSpec((1, t, d), vec), pl.BlockSpec((1, 1, d), vec),
                     pl.BlockSpec((1, 1, d), vec)]
        args += [x, sh, sc]
        for shape, dtype in (((t, d), BF16), ((t, LANES), F32), ((n_exp, LANES), F32),
                             ((n_exp, t), F32), ((n_exp, t), F32)):
            out_specs.append(pl.BlockSpec((1,) + shape, vec))
            out_shapes.append(jax.ShapeDtypeStruct((bsz,) + shape, dtype))
        scratch += [pltpu.VMEM((LANES, t), F32), pltpu.VMEM((LANES, t), F32)]
    in_specs += [pl.BlockSpec((1, 1, d), lambda b: (l, 0, 0)),
                 pl.BlockSpec((1, d, LANES), lambda b: (l, 0, 0))]
    outs = pl.pallas_call(
        functools.partial(_plan_kernel, caps=tuple(st[3] for st in streams)),
        grid=(bsz,),
        in_specs=in_specs,
        out_specs=out_specs,
        out_shape=out_shapes,
        scratch_shapes=scratch,
        compiler_params=_cparams(1),
        name="plan",
    )(*args, norm_g, router_p)
    plans = []
    for s, (x, _, _, _) in enumerate(streams):
        h, slot, starts, slot_e, gate_e = outs[N_PLAN_OUTS * s:N_PLAN_OUTS * (s + 1)]
        n_chunks = -(-x.shape[1] // ROW_CHUNK)
        starts = jnp.swapaxes(starts[:, :, :n_chunks].astype(jnp.int32), 1, 2)
        plans.append((h, slot, starts.reshape(bsz, n_chunks * n_exp), slot_e, gate_e))
    return plans


def _gather_stream(b, starts_ref, h_ref, slot_e_ref, gate_e_ref, xg_ref, gs_ref, row0, cap):
    n_exp, t = slot_e_ref.shape[1:]
    d = h_ref.shape[2]
    chunk = min(t, ROW_CHUNK)
    n_chunks = t // chunk
    win = min(cap, SLOT_WINDOW)
    rows_all = slice(row0, row0 + cap)
    for e in range(n_exp):
        xg_ref[e, 0, rows_all] = jnp.zeros((cap, d), BF16)
        gs_ref[e, 0, rows_all] = jnp.zeros((cap, LANES), F32)

    def scatter_rows(e, rows, n_rows, tokens, gates):
        cur = xg_ref[e, 0, rows].astype(F32)
        xg_ref[e, 0, rows] = (cur + tokens).astype(BF16)
        gs_ref[e, 0, rows] = gs_ref[e, 0, rows] + jnp.broadcast_to(gates, (n_rows, LANES))

    for c in range(n_chunks):
        toks = slice(c * chunk, (c + 1) * chunk)
        h_c = h_ref[0, toks]
        if cap > win:
            nxt = min(c + 1, n_chunks - 1)
            starts, misfit = [], jnp.int32(0)
            for e in range(n_exp):
                first = starts_ref[b, c * n_exp + e]
                end = starts_ref[b, nxt * n_exp + e] if c + 1 < n_chunks else cap
                w0 = jnp.minimum(first & -(2 * SUBLANES), cap - win)
                misfit = misfit | jnp.where(end - w0 > win, 1, 0)
                starts.append(w0)
        else:
            starts, misfit = [0] * n_exp, None

        def windowed(toks=toks, h_c=h_c, starts=starts):
            slot_id = lax.broadcasted_iota(jnp.int32, (win, chunk), 0).astype(F32)
            hits = [slot_e_ref[0, e:e + 1, toks] - jnp.float32(starts[e]) == slot_id
                    for e in range(n_exp)]
            onehot = jnp.concatenate([jnp.where(hit, 1.0, 0.0).astype(BF16) for hit in hits], axis=0)
            picked = _dot(onehot, h_c)
            for e in range(n_exp):
                w0 = starts[e] if isinstance(starts[e], int) else pl.multiple_of(starts[e], 2 * SUBLANES)
                gates = jnp.sum(jnp.where(hits[e], gate_e_ref[0, e:e + 1, toks], 0.0),
                                axis=1, keepdims=True)
                scatter_rows(e, pl.ds(row0 + w0, win), win, picked[e * win:(e + 1) * win], gates)

        def dense(toks=toks, h_c=h_c):
            slot_id = lax.broadcasted_iota(jnp.int32, (cap, chunk), 0).astype(F32)
            for e in range(n_exp):
                hit = slot_e_ref[0, e:e + 1, toks] == slot_id
                gates = jnp.sum(jnp.where(hit, gate_e_ref[0, e:e + 1, toks], 0.0),
                                axis=1, keepdims=True)
                scatter_rows(e, rows_all, cap, _dot(jnp.where(hit, 1.0, 0.0).astype(BF16), h_c), gates)

        if misfit is None:
            windowed()
        else:
            pl.when(misfit == 0)(windowed)
            pl.when(misfit != 0)(dense)


def _gather_kernel(*refs, caps):
    ns = len(caps)
    xg_ref, gs_ref = refs[4 * ns:4 * ns + 2]
    b = pl.program_id(0)
    row0 = 0
    for s in range(ns):
        _gather_stream(b, refs[s], *refs[ns + 3 * s:ns + 3 * s + 3], xg_ref, gs_ref, row0, caps[s])
        row0 += caps[s]


def _gather_call(plans, caps, n_exp):
    bsz, _, d = plans[0][0].shape
    rows = sum(caps)
    vec = lambda b, *_: (b, 0, 0)
    in_specs, args = [], []
    for h, _, _, slot_e, gate_e in plans:
        t = h.shape[1]
        in_specs += [pl.BlockSpec((1, t, d), vec), pl.BlockSpec((1, n_exp, t), vec),
                     pl.BlockSpec((1, n_exp, t), vec)]
        args += [h, slot_e, gate_e]
    grid_spec = pltpu.PrefetchScalarGridSpec(
        num_scalar_prefetch=len(plans),
        grid=(bsz,),
        in_specs=in_specs,
        out_specs=[
            pl.BlockSpec((n_exp, 1, rows, d), lambda b, *_: (0, b, 0, 0)),
            pl.BlockSpec((n_exp, 1, rows, LANES), lambda b, *_: (0, b, 0, 0)),
        ],
    )
    return pl.pallas_call(
        functools.partial(_gather_kernel, caps=caps),
        grid_spec=grid_spec,
        out_shape=[
            jax.ShapeDtypeStruct((n_exp, bsz, rows, d), BF16),
            jax.ShapeDtypeStruct((n_exp, bsz, rows, LANES), F32),
        ],
        compiler_params=_cparams(1),
        name="gather",
    )(*[p[2] for p in plans], *args)


def _ffn_kernel(x_ref, gs_ref, wg_ref, wu_ref, wd_ref, y_ref, w_s):
    @pl.when(pl.program_id(1) == 0)
    def _():
        w_s[0] = wg_ref[0, 0].astype(BF16)
        w_s[1] = wu_ref[0, 0].astype(BF16)
        w_s[2] = wd_ref[0, 0].astype(BF16)

    x = x_ref[0]
    f = w_s.shape[2]
    acc = None
    for c0 in range(0, f, FFN_COLS):
        a = _dot(x, w_s[0, :, c0:c0 + FFN_COLS])
        u = _dot(x, w_s[1, :, c0:c0 + FFN_COLS])
        hmid = (a * _sigmoid(a) * u).astype(BF16)
        part = _dot(hmid, w_s[2, c0:c0 + FFN_COLS, :])
        acc = part if acc is None else acc + part
    y_ref[0] = (acc * gs_ref[0][:, 0:1]).astype(BF16)


def _ffn_call(xg, gs, w_gate, w_up, w_down, l):
    n_exp, rows, d = xg.shape
    f = w_gate.shape[-1]
    tm = rows // max(1, -(-rows // FFN_ROWS))
    assert rows % tm == 0 and tm % (2 * SUBLANES) == 0, (rows, tm)
    w_map = lambda e, r: (l, e, 0, 0)
    return pl.pallas_call(
        _ffn_kernel,
        grid=(n_exp, rows // tm),
        in_specs=[
            pl.BlockSpec((1, tm, d), lambda e, r: (e, r, 0)),
            pl.BlockSpec((1, tm, LANES), lambda e, r: (e, r, 0)),
            pl.BlockSpec((1, 1, d, f), w_map),
            pl.BlockSpec((1, 1, d, f), w_map),
            pl.BlockSpec((1, 1, f, d), w_map),
        ],
        out_specs=pl.BlockSpec((1, tm, d), lambda e, r: (e, r, 0)),
        out_shape=jax.ShapeDtypeStruct((n_exp, rows, d), BF16),
        scratch_shapes=[pltpu.VMEM((3, d, f), BF16)],
        compiler_params=_cparams(2),
        name="ffn",
    )(xg, gs, w_gate, w_up, w_down)


def _combine_kernel(starts_ref, y_ref, slot_ref, x_ref, g_ref, fg_ref, o_ref, *, final_norm, windowed):
    n_exp, _, cap, d = y_ref.shape
    tk = x_ref.shape[1]
    tile = min(tk, ROW_CHUNK)
    b = pl.program_id(0)
    tile0 = pl.program_id(1) * (tk // tile)
    n_tiles = pl.num_programs(1) * (tk // tile)

    for sub in range(tk // tile):
        rows = slice(sub * tile, (sub + 1) * tile)
        slots = slot_ref[0, rows]

        def finish(acc, rows=rows):
            out = x_ref[0, rows] + g_ref[0] * acc
            if final_norm:
                ms = jnp.mean(out * out, axis=-1, keepdims=True)
                out = out * lax.rsqrt(ms + NORM_EPS) * fg_ref[...]
            o_ref[0, rows] = out

        def dense(slots=slots, finish=finish):
            lane = lax.broadcasted_iota(jnp.int32, (tile, cap), 1).astype(F32)
            acc = jnp.zeros((tile, d), F32)
            for e in range(n_exp):
                onehot = jnp.where(lane == slots[:, e:e + 1], 1.0, 0.0).astype(BF16)
                acc = acc + _dot(onehot, y_ref[e, 0])
            finish(acc)

        if not windowed:
            dense()
            continue

        t_idx = tile0 + sub
        nxt = jnp.minimum(t_idx + 1, n_tiles - 1)
        win, misfit = [], jnp.int32(0)
        for e in range(n_exp):
            first = starts_ref[b, t_idx * n_exp + e]
            end = jnp.where(t_idx + 1 < n_tiles, starts_ref[b, nxt * n_exp + e], cap)
            w0 = jnp.minimum(first & -(2 * SUBLANES), cap - SLOT_WINDOW)
            misfit = misfit | jnp.where(end - w0 > SLOT_WINDOW, 1, 0)
            win.append(w0)

        def sparse(slots=slots, finish=finish, win=win):
            lane = lax.broadcasted_iota(jnp.int32, (tile, LANES), 1)
            low = lane < SLOT_WINDOW
            pos = (lane & (SLOT_WINDOW - 1)).astype(F32)
            per_dot = 2 * LANES // SLOT_WINDOW
            acc = jnp.zeros((tile, d), F32)
            for e0 in range(0, n_exp, per_dot):
                halves = []
                for e in range(e0, e0 + per_dot, 2):
                    rel_a = slots[:, e:e + 1] - win[e].astype(F32)
                    rel_b = slots[:, e + 1:e + 2] - win[e + 1].astype(F32)
                    hit = jnp.where(low, rel_a, rel_b) == pos
                    halves.append(jnp.where(hit, 1.0, 0.0).astype(BF16))
                yk = jnp.concatenate(
                    [y_ref[e, 0, pl.ds(pl.multiple_of(win[e], 2 * SUBLANES), SLOT_WINDOW), :]
                     for e in range(e0, e0 + per_dot)], axis=0)
                acc = acc + _dot(jnp.concatenate(halves, axis=1), yk)
            finish(acc)

        pl.when(misfit == 0)(sparse)
        pl.when(misfit != 0)(dense)


def _combine_call(y, slot, starts, x, gate, final_g, cap, row0, final_norm):
    bsz, t, d = x.shape
    n_exp = y.shape[0]
    tk = min(t, COMBINE_ROWS)
    assert row0 % cap == 0
    windowed = (cap >= SLOT_WINDOW and n_exp % (2 * LANES // SLOT_WINDOW) == 0
                and t % ROW_CHUNK == 0)
    grid_spec = pltpu.PrefetchScalarGridSpec(
        num_scalar_prefetch=1,
        grid=(bsz, t // tk),
        in_specs=[
            pl.BlockSpec((n_exp, 1, cap, d), lambda b, i, st: (0, b, row0 // cap, 0)),
            pl.BlockSpec((1, tk, LANES), lambda b, i, st: (b, i, 0)),
            pl.BlockSpec((1, tk, d), lambda b, i, st: (b, i, 0)),
            pl.BlockSpec((1, 1, d), lambda b, i, st: (b, 0, 0)),
            pl.BlockSpec((1, d), lambda b, i, st: (0, 0)),
        ],
        out_specs=pl.BlockSpec((1, tk, d), lambda b, i, st: (b, i, 0)),
    )
    return pl.pallas_call(
        functools.partial(_combine_kernel, final_norm=final_norm, windowed=windowed),
        grid_spec=grid_spec,
        out_shape=jax.ShapeDtypeStruct(x.shape, F32),
        compiler_params=_cparams(2),
        name="combine",
    )(starts, y, slot, x, gate, final_g)


def kernel(x, c, ctx, c_ctx, ada_w, ada_b, norm1_g, norm2_g, lru_w_in, lru_conv_w, lru_conv_b,
           lru_gate_w, lru_gate_b, lru_lambda, lru_w_out, attn_w_qkv, attn_sink, attn_w_o,
           moe_router, moe_w_gate, moe_w_up, moe_w_down, final_g):
    bsz, t, d = x.shape
    tc = ctx.shape[1]
    depth = ada_w.shape[0]
    n_exp = moe_router.shape[-1]
    cap_l = CAPACITY_FACTOR * t // n_exp
    cap_c = CAPACITY_FACTOR * tc // n_exp

    cond_rows = 2 * SUBLANES
    cond = jnp.zeros((cond_rows, d), F32).at[:bsz].set(c).at[bsz].set(c_ctx)
    mod = _ada_call(cond, ada_w, ada_b).reshape(depth, cond_rows, 6, 1, d)
    norm1 = norm1_g.reshape(depth, 1, d)
    norm2 = norm2_g.reshape(depth, 1, d)
    router_p = jnp.pad(moe_router, ((0, 0), (0, 0), (0, LANES - n_exp)))
    final_g2 = final_g.reshape(1, d)
    cos, sin_signed = _rope_tables(t)
    ones_c = jnp.ones((tc, HEAD_DIM), F32)

    for l in range(depth):
        need_ctx = l < depth - 1
        lat = [mod[l, :bsz, i] for i in range(6)]
        con = [jnp.broadcast_to(mod[l, bsz, i], (bsz, 1, d)) for i in range(6)]
        j = l // 2
        if l % 2 == 0:
            s_l, s_c = _lru_call(x, ctx, lat[0], lat[1], con[0], con[1], norm1, l,
                                 lru_w_in[j].astype(BF16), lru_conv_w[j], lru_conv_b[j],
                                 (0.5 * lru_gate_w[j]).astype(BF16), 0.5 * lru_gate_b[j],
                                 lru_lambda[j])
            w_out = lru_w_out[j].astype(BF16)
        else:
            w_qkv = attn_w_qkv[j].astype(BF16)
            q, k, v = _qkv_call(x, lat[0], lat[1], norm1, l, w_qkv, cos, sin_signed, True)
            qc, kc, vc = _qkv_call(ctx, con[0], con[1], norm1, l, w_qkv, ones_c, ones_c, False)
            s_l, s_c = _attn_call(attn_sink[j], q, k, v, qc, kc, vc, need_ctx)
            w_out = attn_w_o[j].astype(BF16)
        x = _resid_call(s_l, w_out, x, lat[2])
        if need_ctx:
            ctx = _resid_call(s_c, w_out, ctx, con[2])

        streams = [(x, lat[3], lat[4], cap_l)]
        if need_ctx:
            streams.append((ctx, con[3], con[4], cap_c))
        plans = _plan_call(streams, norm2, l, router_p, n_exp)
        xg, gs = _gather_call(plans, tuple(st[3] for st in streams), n_exp)
        rows = xg.shape[2]
        y = _ffn_call(xg.reshape(n_exp, bsz * rows, d), gs.reshape(n_exp, bsz * rows, LANES),
                      moe_w_gate, moe_w_up, moe_w_down, l).reshape(xg.shape)
        x = _combine_call(y, plans[0][1], plans[0][2], x, lat[5], final_g2, cap_l, 0, not need_ctx)
        if need_ctx:
            ctx = _combine_call(y, plans[1][1], plans[1][2], ctx, con[5], final_g2, cap_c, cap_l,
                                False)
    return x
```
